```python
import jax, jax.numpy as jnp
from jax import lax
import numpy as np

D_MODEL = 2048
BATCH = 4
SEQ = 8192
DEPTH = 1
DEC_BATCH = 32
DEC_SEQ = 64
PAST_LEN = 1024

CHUNK = 64
Q_BLOCK = 128
H_FOX = 8
HD_FOX = 128
H_RET = 8
HD_RET = 128
D_FOX = H_FOX * HD_FOX
D_RET = H_RET * HD_RET
D_MIX = D_FOX + D_RET
D_FF = 5632
CONV_W = 3
ROPE_BASE = 10000.0
EPS = 1e-6
IN_COLS = 3 * D_FOX + H_FOX + 4 * D_RET

kernel_name = 'fox_retention_parallel_heads_stream_step'


def rms_norm(x, g):
    xf = x.astype(jnp.float32)
    y = xf * lax.rsqrt(jnp.mean(xf * xf, axis=-1, keepdims=True) + EPS)
    return (y * g.astype(jnp.float32)).astype(x.dtype)


def rope(x, pos):
    half = x.shape[-1] // 2
    inv = jnp.power(ROPE_BASE, -jnp.arange(half, dtype=jnp.float32) / half)
    ang = pos.astype(jnp.float32)[:, None] * inv[None, :]
    cos = jnp.cos(ang)[None, :, None, :]
    sin = jnp.sin(ang)[None, :, None, :]
    xf = x.astype(jnp.float32)
    x1, x2 = xf[..., :half], xf[..., half:]
    return jnp.concatenate([x1 * cos - x2 * sin, x1 * sin + x2 * cos], axis=-1).astype(x.dtype)


def project_heads(x, pos, ln1, w_in, b_f, fox_qn, fox_kn):
    B, T, _ = x.shape
    p = jnp.einsum('btd,dc->btc', rms_norm(x, ln1), w_in)
    cuts = [D_FOX, 2 * D_FOX, 3 * D_FOX, 3 * D_FOX + H_FOX,
            3 * D_FOX + H_FOX + D_RET, 3 * D_FOX + H_FOX + 2 * D_RET, 3 * D_FOX + H_FOX + 3 * D_RET]
    fq, fk, fv, ff, rq, rk, rv, rg = jnp.split(p, cuts, axis=-1)
    fq = rms_norm(fq.reshape(B, T, H_FOX, HD_FOX), fox_qn)
    fk = rms_norm(fk.reshape(B, T, H_FOX, HD_FOX), fox_kn)
    fv = fv.reshape(B, T, H_FOX, HD_FOX)
    logf = jax.nn.log_sigmoid((ff + b_f).astype(jnp.float32))
    rq = rope(rq.reshape(B, T, H_RET, HD_RET), pos)
    rk = rope(rk.reshape(B, T, H_RET, HD_RET), pos) * (HD_RET ** -0.5)
    rv = rv.reshape(B, T, H_RET, HD_RET)
    return fq, fk, fv, logf, rq, rk, rv, rg


def fox_prompt(q, k, v, logf):
    B, T, H, D = q.shape
    nb = T // Q_BLOCK
    c = jnp.cumsum(logf, axis=1)
    c_k = c.transpose(0, 2, 1)[:, :, None, :]
    qb = q.reshape(B, nb, Q_BLOCK, H, D).transpose(1, 0, 2, 3, 4)
    cb = c.reshape(B, nb, Q_BLOCK, H).transpose(1, 0, 2, 3)
    kpos = jnp.arange(T)
    scale = HD_FOX ** -0.5

    def block(args):
        qi, ci, i = args
        qpos = i * Q_BLOCK + jnp.arange(Q_BLOCK)
        s = jnp.einsum('bqhd,bkhd->bhqk', qi, k).astype(jnp.float32) * scale
        s = s + ci.transpose(0, 2, 1)[..., None] - c_k
        s = jnp.where(kpos[None, None, None, :] <= qpos[None, None, :, None], s, -jnp.inf)
        p = jax.nn.softmax(s, axis=-1)
        return jnp.einsum('bhqk,bkhd->bqhd', p.astype(v.dtype), v)

    o = lax.map(block, (qb, cb, jnp.arange(nb)))
    return o.transpose(1, 0, 2, 3, 4).reshape(B, T, H * D)


def fox_sample(q, k_new, v_new, logf_new, cache_k, cache_v, cache_logf):
    B, L, H, D = q.shape
    P = cache_k.shape[1]
    k = jnp.concatenate([cache_k.astype(k_new.dtype), k_new], axis=1)
    v = jnp.concatenate([cache_v.astype(v_new.dtype), v_new], axis=1)
    c = jnp.cumsum(jnp.concatenate([cache_logf.astype(jnp.float32), logf_new], axis=1), axis=1)
    cq = c[:, P:]
    s = jnp.einsum('bqhd,bkhd->bhqk', q, k).astype(jnp.float32) * (HD_FOX ** -0.5)
    s = s + cq.transpose(0, 2, 1)[..., None] - c.transpose(0, 2, 1)[:, :, None, :]
    mask = jnp.arange(P + L)[None, :] <= (P + jnp.arange(L))[:, None]
    s = jnp.where(mask[None, None], s, -jnp.inf)
    p = jax.nn.softmax(s, axis=-1)
    o = jnp.einsum('bhqk,bkhd->bqhd', p.astype(v.dtype), v)
    return o.reshape(B, L, H * D)


def retention_chunk(q, k, v, s_prev, log_gamma):
    q, k, v = q.astype(jnp.float32), k.astype(jnp.float32), v.astype(jnp.float32)
    s_prev = s_prev.astype(jnp.float32)
    L = q.shape[1]
    i = jnp.arange(L)
    diff = i[:, None] - i[None, :]
    decay = jnp.where(diff[None] >= 0,
                      jnp.exp(jnp.maximum(diff, 0)[None] * log_gamma[:, None, None]), 0.0)
    scores = jnp.einsum('bihd,bjhd->bhij', q, k) * decay[None]
    o_intra = jnp.einsum('bhij,bjhe->bihe', scores, v)
    inter = jnp.exp((i + 1)[None, :] * log_gamma[:, None])
    o_inter = jnp.einsum('bihd,bhde->bihe', q, s_prev) * inter.T[None, :, :, None]
    upd = jnp.exp((L - 1 - i)[None, :] * log_gamma[:, None])
    s_new = jnp.exp(L * log_gamma)[None, :, None, None] * s_prev + jnp.einsum('bjhd,hj,bjhe->bhde', k, upd, v)
    return o_intra + o_inter, s_new


def retention_prompt(q, k, v, log_gamma):
    B, T, H, D = q.shape
    nc = T // CHUNK

    def to_blocks(a):
        return a.reshape(B, nc, CHUNK, H, D).transpose(1, 0, 2, 3, 4)

    def step(s, qkv):
        o, s_new = retention_chunk(qkv[0], qkv[1], qkv[2], s, log_gamma)
        return s_new, o

    s0 = jnp.zeros((B, H, D, D), jnp.float32)
    s_fin, o = lax.scan(step, s0, (to_blocks(q), to_blocks(k), to_blocks(v)))
    return o.transpose(1, 0, 2, 3, 4).reshape(B, T, H, D), s_fin


def merge_heads(o_fox, o_ret, rg, ret_gn, w_out):
    B, T, _ = o_fox.shape
    o_ret = rms_norm(o_ret, ret_gn).reshape(B, T, D_RET)
    o_ret = (o_ret * jax.nn.silu(rg.astype(jnp.float32))).astype(o_fox.dtype)
    return jnp.einsum('btc,cd->btd', jnp.concatenate([o_fox, o_ret], axis=-1), w_out)


def conv_ffn(x, conv_prev, ln2, w_up, conv_w, conv_b, w_down):
    T = x.shape[1]
    u = jnp.einsum('btd,df->btf', rms_norm(x, ln2), w_up)
    full = jnp.concatenate([conv_prev.astype(u.dtype), u], axis=1)
    uc = conv_b + sum(full[:, j:j + T] * conv_w[j] for j in range(CONV_W))
    a, g = jnp.split(uc, 2, axis=-1)
    y = jnp.einsum('btf,fd->btd', jax.nn.silu(g) * a, w_down)
    return y, full[:, T:]


def setup_inputs(seed: int = 0) -> dict:
    key = jax.random.key(seed)
    ks = jax.random.split(key, 20)
    nrm = jax.random.normal
    f32 = jnp.float32
    return {
        'x_prompt': nrm(ks[0], (BATCH, SEQ, D_MODEL), f32),
        'x_sample': nrm(ks[1], (DEC_BATCH, DEC_SEQ, D_MODEL), f32),
        'cache_fox_k': nrm(ks[2], (DEPTH, DEC_BATCH, PAST_LEN, H_FOX, HD_FOX), f32),
        'cache_fox_v': nrm(ks[3], (DEPTH, DEC_BATCH, PAST_LEN, H_FOX, HD_FOX), f32),
        'cache_fox_logf': jax.nn.log_sigmoid(2.0 + nrm(ks[4], (DEPTH, DEC_BATCH, PAST_LEN, H_FOX), f32)),
        'state_ret': 0.5 * nrm(ks[5], (DEPTH, DEC_BATCH, H_RET, HD_RET, HD_RET), f32),
        'state_conv': nrm(ks[6], (DEPTH, DEC_BATCH, CONV_W - 1, 2 * D_FF), f32),
        'ln1': 1.0 + 0.02 * nrm(ks[7], (DEPTH, D_MODEL), f32),
        'w_in': nrm(ks[8], (DEPTH, D_MODEL, IN_COLS), f32) * D_MODEL ** -0.5,
        'b_f': 2.0 + 0.5 * nrm(ks[9], (DEPTH, H_FOX), f32),
        'fox_qn': 1.0 + 0.02 * nrm(ks[10], (DEPTH, HD_FOX), f32),
        'fox_kn': 1.0 + 0.02 * nrm(ks[11], (DEPTH, HD_FOX), f32),
        'ret_gn': 1.0 + 0.02 * nrm(ks[12], (DEPTH, H_RET, HD_RET), f32),
        'w_out': nrm(ks[13], (DEPTH, D_MIX, D_MODEL), f32) * D_MIX ** -0.5,
        'ln2': 1.0 + 0.02 * nrm(ks[14], (DEPTH, D_MODEL), f32),
        'w_up': nrm(ks[15], (DEPTH, D_MODEL, 2 * D_FF), f32) * D_MODEL ** -0.5,
        'conv_w': nrm(ks[16], (DEPTH, CONV_W, 2 * D_FF), f32) * CONV_W ** -0.5,
        'conv_b': 0.02 * nrm(ks[17], (DEPTH, 2 * D_FF), f32),
        'w_down': nrm(ks[18], (DEPTH, D_FF, D_MODEL), f32) * D_FF ** -0.5,
    }


def reference(x_prompt, x_sample, cache_fox_k, cache_fox_v, cache_fox_logf, state_ret, state_conv,
              ln1, w_in, b_f, fox_qn, fox_kn, ret_gn, w_out, ln2, w_up, conv_w, conv_b, w_down):
    log_gamma = jnp.log(1.0 - jnp.power(2.0, -5.0 - jnp.arange(H_RET, dtype=jnp.float32)))
    Bp, Tp, _ = x_prompt.shape
    Ts = x_sample.shape[1]
    pos_p = jnp.arange(Tp)
    pos_s = PAST_LEN + jnp.arange(Ts)
    xp, xs = x_prompt, x_sample
    st_p, st_s = [], []
    for l in range(DEPTH):
        fq, fk, fv, lf, rq, rk, rv, rg = project_heads(xp, pos_p, ln1[l], w_in[l], b_f[l], fox_qn[l], fox_kn[l])
        o_fox = fox_prompt(fq, fk, fv, lf)
        o_ret, s_ret_p = retention_prompt(rq, rk, rv, log_gamma)
        xp = xp + merge_heads(o_fox, o_ret, rg, ret_gn[l], w_out[l])
        conv0 = jnp.zeros((Bp, CONV_W - 1, 2 * D_FF), xp.dtype)
        y, conv_p = conv_ffn(xp, conv0, ln2[l], w_up[l], conv_w[l], conv_b[l], w_down[l])
        xp = xp + y
        st_p.append((fk, fv, lf, s_ret_p, conv_p))
        fq, fk, fv, lf, rq, rk, rv, rg = project_heads(xs, pos_s, ln1[l], w_in[l], b_f[l], fox_qn[l], fox_kn[l])
        o_fox = fox_sample(fq, fk, fv, lf, cache_fox_k[l], cache_fox_v[l], cache_fox_logf[l])
        o_ret, s_ret_s = retention_chunk(rq, rk, rv, state_ret[l], log_gamma)
        xs = xs + merge_heads(o_fox, o_ret, rg, ret_gn[l], w_out[l])
        y, conv_s = conv_ffn(xs, state_conv[l], ln2[l], w_up[l], conv_w[l], conv_b[l], w_down[l])
        xs = xs + y
        st_s.append((fk, fv, lf, s_ret_s, conv_s))
    new_k_p = jnp.stack([s[0] for s in st_p])
    new_v_p = jnp.stack([s[1] for s in st_p])
    new_logf_p = jnp.stack([s[2] for s in st_p])
    new_ret_p = jnp.stack([s[3] for s in st_p])
    new_conv_p = jnp.stack([s[4] for s in st_p])
    new_k_s = jnp.stack([s[0] for s in st_s])
    new_v_s = jnp.stack([s[1] for s in st_s])
    new_logf_s = jnp.stack([s[2] for s in st_s])
    new_ret_s = jnp.stack([s[3] for s in st_s])
    new_conv_s = jnp.stack([s[4] for s in st_s])
    return (xp, xs, new_k_p, new_v_p, new_logf_p, new_ret_p, new_conv_p,
            new_k_s, new_v_s, new_logf_s, new_ret_s, new_conv_s)
```

```python
import functools
import math

import jax
import jax.numpy as jnp
from jax import lax
from jax.experimental import pallas as pl
from jax.experimental.pallas import tpu as pltpu

EPS = 1e-6
ROPE_BASE = 10000.0
MASKED_LOGIT = -1e30
LANES = 128
VMEM_LIMIT_BYTES = 56 * 1024 * 1024

F32 = jnp.float32
BF16 = jnp.bfloat16

N_KINDS = 7
HEADS_PER_STEP = 2


def _params(n_axes):
    return pltpu.CompilerParams(dimension_semantics=("arbitrary",) * n_axes,
                                vmem_limit_bytes=VMEM_LIMIT_BYTES)


def _resident(block_shape, index_map):
    return pl.BlockSpec(block_shape, index_map, pipeline_mode=pl.Buffered(1))


def _rms(x):
    return x * lax.rsqrt(jnp.mean(x * x, axis=-1, keepdims=True) + EPS)


def _sigmoid(x):
    return 1.0 / (1.0 + jnp.exp(-x))


def _inproj_kernel(x_ref, ln1_ref, w_ref, wf_ref, bf_ref, qn_ref, kn_ref, cos_ref, sin_ref,
                   fq_ref, fkb_ref, fvb_ref, rq_ref, rk_ref, rv_ref, rg_ref, fk_ref, fv_ref, logf_ref,
                   h_ref, *, nb, seg, n_fox):
    j = pl.program_id(1)

    @pl.when(j == 0)
    def _():
        h = (_rms(x_ref[...]) * ln1_ref[...]).astype(BF16)
        h_ref[...] = h
        z = jnp.dot(h, wf_ref[...], preferred_element_type=F32) + bf_ref[...]
        logf = jnp.minimum(z, 0.0) - jnp.log1p(jnp.exp(-jnp.abs(z)))
        logf_ref[...] = logf[:, :n_fox]

    p = jnp.dot(h_ref[...], w_ref[0], preferred_element_type=F32)
    cos = cos_ref[...]
    sin = sin_ref[...]
    hd = LANES

    def slab(kind, e):
        c0 = (kind * HEADS_PER_STEP + e) * hd
        return p[:, c0:c0 + hd]

    def head_major(a):
        return a.reshape(nb, seg, hd).astype(BF16)

    def rope(a):
        return a * cos + pltpu.roll(a, hd // 2, 1) * sin

    for e in range(HEADS_PER_STEP):
        fq_ref[:, e] = head_major(_rms(slab(0, e)) * qn_ref[...])
        fk = _rms(slab(1, e)) * kn_ref[...]
        fk_ref[:, e * hd:(e + 1) * hd] = fk
        fkb_ref[:, e] = head_major(fk)
        fv = slab(2, e)
        fv_ref[:, e * hd:(e + 1) * hd] = fv
        fvb_ref[:, e] = head_major(fv)
        rq_ref[:, e] = head_major(rope(slab(3, e)))
        rk_ref[:, e] = head_major(rope(slab(4, e)) * (hd ** -0.5))
        rv_ref[:, e] = head_major(slab(5, e))
        rg_ref[:, e] = head_major(slab(6, e))


def _inproj(x, ln1, w_groups, wf, bfp, qn, kn, cos_t, sin_t, *, batch, seq, n_heads):
    tokens, d = x.shape
    hd = LANES
    tm = min(512, tokens)
    seg = min(seq, tm)
    nb = tm // seg
    tps = seq // seg
    n_groups = n_heads // HEADS_PER_STEP
    gw = N_KINDS * HEADS_PER_STEP * hd
    grid = (tokens // tm, n_groups)

    hm_shape = jax.ShapeDtypeStruct((batch, n_heads, seq, hd), BF16)
    hm_spec = pl.BlockSpec((nb, HEADS_PER_STEP, seg, hd), lambda i, j: (i // tps, j, i % tps, 0))
    tok_shape = jax.ShapeDtypeStruct((tokens, n_heads * hd), F32)
    tok_spec = pl.BlockSpec((tm, HEADS_PER_STEP * hd), lambda i, j: (i, j))

    return pl.pallas_call(
        functools.partial(_inproj_kernel, nb=nb, seg=seg, n_fox=n_heads),
        grid=grid,
        in_specs=[
            pl.BlockSpec((tm, d), lambda i, j: (i, 0)),
            _resident((1, d), lambda i, j: (0, 0)),
            pl.BlockSpec((1, d, gw), lambda i, j: (j, 0, 0)),
            _resident((d, hd), lambda i, j: (0, 0)),
            _resident((1, hd), lambda i, j: (0, 0)),
            _resident((1, hd), lambda i, j: (0, 0)),
            _resident((1, hd), lambda i, j: (0, 0)),
            pl.BlockSpec((tm, hd), lambda i, j: (i % tps, 0)),
            pl.BlockSpec((tm, hd), lambda i, j: (i % tps, 0)),
        ],
        out_specs=[hm_spec] * 7 + [tok_spec, tok_spec,
                                   pl.BlockSpec((tm, n_heads), lambda i, j: (i, 0))],
        out_shape=[hm_shape] * 7 + [tok_shape, tok_shape,
                                    jax.ShapeDtypeStruct((tokens, n_heads), F32)],
        scratch_shapes=[pltpu.VMEM((tm, d), BF16)],
        compiler_params=_params(2),
    )(x, ln1, w_groups, wf, bfp, qn, kn, cos_t, sin_t)


def _cumsum_kernel(lf_ref, c_ref, *, ch):
    total = lf_ref.shape[2]
    r = lax.broadcasted_iota(jnp.int32, (ch, ch), 0)
    c = lax.broadcasted_iota(jnp.int32, (ch, ch), 1)
    upper = (r <= c).astype(F32)
    carry = jnp.zeros((lf_ref.shape[1], 1), F32)
    for k in range(total // ch):
        x = lf_ref[0, :, k * ch:(k + 1) * ch]
        cs = jnp.dot(x, upper, precision=lax.Precision.HIGHEST, preferred_element_type=F32) + carry
        c_ref[0, :, k * ch:(k + 1) * ch] = cs
        carry = cs[:, ch - 1:ch]


def _cumsum(lf_t):
    batch, n_heads, total = lf_t.shape
    ch = 256 if total % 256 == 0 else LANES
    spec = pl.BlockSpec((1, n_heads, total), lambda b: (b, 0, 0))
    return pl.pallas_call(
        functools.partial(_cumsum_kernel, ch=ch),
        grid=(batch,), in_specs=[spec], out_specs=spec,
        out_shape=jax.ShapeDtypeStruct(lf_t.shape, F32),
        compiler_params=_params(1),
    )(lf_t)


def _softmax_step(z, v, m_ref, l_ref, acc_ref):
    tk = z.shape[1]
    m_prev = m_ref[...]
    m_next = jnp.maximum(m_prev, jnp.max(z, axis=1, keepdims=True))
    p = jnp.exp(z - pltpu.repeat(m_next, tk // LANES, 1))
    alpha = jnp.exp(m_prev - m_next)
    l_ref[...] = alpha * l_ref[...] + jnp.sum(p, axis=1, keepdims=True)
    acc_ref[...] = alpha * acc_ref[...] + jnp.dot(p.astype(BF16), v, preferred_element_type=F32)
    m_ref[...] = m_next


def _fox_prompt_kernel(q_ref, k_ref, v_ref, c_ref, o_ref, m_ref, l_ref, acc_ref, *, tq, scale):
    qi = pl.program_id(2)
    q = q_ref[0, 0]
    m_ref[...] = jnp.full(m_ref.shape, MASKED_LOGIT, F32)
    l_ref[...] = jnp.zeros(l_ref.shape, F32)
    acc_ref[...] = jnp.zeros(acc_ref.shape, F32)

    def logits(k0):
        k = k_ref[0, 0, pl.ds(k0, tq), :]
        s = lax.dot_general(q, k, (((1,), (1,)), ((), ())), preferred_element_type=F32)
        return s * scale - c_ref[0, 0, :, pl.ds(k0, tq)]

    def body(j, carry):
        k0 = pl.multiple_of(j * tq, tq)
        _softmax_step(logits(k0), v_ref[0, 0, pl.ds(k0, tq), :], m_ref, l_ref, acc_ref)
        return carry

    lax.fori_loop(0, qi, body, 0)

    k0 = pl.multiple_of(qi * tq, tq)
    row = lax.broadcasted_iota(jnp.int32, (tq, tq), 0)
    col = lax.broadcasted_iota(jnp.int32, (tq, tq), 1)
    z = jnp.where(col <= row, logits(k0), MASKED_LOGIT)
    _softmax_step(z, v_ref[0, 0, pl.ds(k0, tq), :], m_ref, l_ref, acc_ref)
    o_ref[...] = (acc_ref[...] / l_ref[...]).astype(o_ref.dtype)


def _fox_prompt(q, k, v, c_row):
    batch, n_heads, seq, hd = q.shape
    tq = min(512, seq)
    nq = seq // tq
    kv_spec = pl.BlockSpec((1, 1, seq, hd), lambda b, h, i: (b, h, 0, 0))
    return pl.pallas_call(
        functools.partial(_fox_prompt_kernel, tq=tq, scale=hd ** -0.5),
        grid=(batch, n_heads, nq),
        in_specs=[pl.BlockSpec((1, 1, tq, hd), lambda b, h, i: (b, h, i, 0)), kv_spec, kv_spec,
                  pl.BlockSpec((1, 1, 1, seq), lambda b, h, i: (b, h, 0, 0))],
        out_specs=pl.BlockSpec((tq, hd), lambda b, h, i: (b * nq + i, h)),
        out_shape=jax.ShapeDtypeStruct((batch * seq, n_heads * hd), BF16),
        scratch_shapes=[pltpu.VMEM((tq, LANES), F32), pltpu.VMEM((tq, LANES), F32),
                        pltpu.VMEM((tq, hd), F32)],
        compiler_params=_params(3),
    )(q, k, v, c_row)


def _fox_sample_kernel(q_ref, kn_ref, vn_ref, kc_ref, vc_ref, c_ref, o_ref, *, past, scale):
    n_heads, new, hd = q_ref.shape[1:]
    row = lax.broadcasted_iota(jnp.int32, (new, new), 0)
    col = lax.broadcasted_iota(jnp.int32, (new, new), 1)
    nt = (((1,), (1,)), ((), ()))
    for h in range(n_heads):
        q = q_ref[0, h]
        kc = kc_ref[0, :, h * hd:(h + 1) * hd].astype(BF16)
        vc = vc_ref[0, :, h * hd:(h + 1) * hd].astype(BF16)
        z_c = (lax.dot_general(q, kc, nt, preferred_element_type=F32) * scale
               - c_ref[0, h:h + 1, 0:past])
        z_n = (lax.dot_general(q, kn_ref[0, h], nt, preferred_element_type=F32) * scale
               - c_ref[0, h:h + 1, past:past + new])
        z_n = jnp.where(col <= row, z_n, MASKED_LOGIT)
        m = jnp.maximum(jnp.max(z_c, axis=1, keepdims=True), jnp.max(z_n, axis=1, keepdims=True))
        p_c = jnp.exp(z_c - m)
        p_n = jnp.exp(z_n - m)
        l = jnp.sum(p_c, axis=1, keepdims=True) + jnp.sum(p_n, axis=1, keepdims=True)
        acc = (jnp.dot(p_c.astype(BF16), vc, preferred_element_type=F32)
               + jnp.dot(p_n.astype(BF16), vn_ref[0, h], preferred_element_type=F32))
        o_ref[:, h * hd:(h + 1) * hd] = (acc / l).astype(o_ref.dtype)


def _fox_sample(q, k_new, v_new, cache_k, cache_v, c_row):
    batch, n_heads, new, hd = q.shape
    past = cache_k.shape[1]
    new_spec = pl.BlockSpec((1, n_heads, new, hd), lambda b: (b, 0, 0, 0))
    cache_spec = pl.BlockSpec((1, past, n_heads * hd), lambda b: (b, 0, 0))
    return pl.pallas_call(
        functools.partial(_fox_sample_kernel, past=past, scale=hd ** -0.5),
        grid=(batch,),
        in_specs=[new_spec, new_spec, new_spec, cache_spec, cache_spec,
                  pl.BlockSpec((1, n_heads, c_row.shape[2]), lambda b: (b, 0, 0))],
        out_specs=pl.BlockSpec((new, n_heads * hd), lambda b: (b, 0)),
        out_shape=jax.ShapeDtypeStruct((batch * new, n_heads * hd), BF16),
        compiler_params=_params(1),
    )(q, k_new, v_new, cache_k, cache_v, c_row)


def _retention_kernel(q_ref, k_ref, v_ref, g_ref, s0_ref, gn_ref, o_ref, s_out_ref,
                      s_ref, decay_ref, inter_ref, upd_ref, *, log_gammas):
    b = pl.program_id(0)
    c = pl.program_id(1)
    n_heads, chunk, hd = q_ref.shape[1:]

    @pl.when((b == 0) & (c == 0))
    def _():
        i = lax.broadcasted_iota(jnp.int32, (chunk, chunk), 0)
        jj = lax.broadcasted_iota(jnp.int32, (chunk, chunk), 1)
        diff = (i - jj).astype(F32)
        pos = lax.broadcasted_iota(jnp.int32, (chunk, hd), 0).astype(F32)
        for h, lg in enumerate(log_gammas):
            decay_ref[h] = jnp.where(diff >= 0.0, jnp.exp(jnp.maximum(diff, 0.0) * lg), 0.0)
            inter_ref[h] = jnp.exp((pos + 1.0) * lg)
            upd_ref[h] = jnp.exp((chunk - 1.0 - pos) * lg)

    @pl.when(c == 0)
    def _():
        s_ref[...] = s0_ref[0]

    nt = (((1,), (1,)), ((), ()))
    tn = (((0,), (0,)), ((), ()))
    for h, lg in enumerate(log_gammas):
        q = q_ref[0, h]
        k = k_ref[0, h]
        v = v_ref[0, h]
        s_prev = s_ref[h]
        scores = lax.dot_general(q, k, nt, preferred_element_type=F32) * decay_ref[h]
        o = jnp.dot(scores.astype(BF16), v, preferred_element_type=F32)
        o = o + jnp.dot(q, s_prev.astype(BF16), preferred_element_type=F32) * inter_ref[h]
        kd = (k.astype(F32) * upd_ref[h]).astype(BF16)
        s_ref[h] = math.exp(chunk * lg) * s_prev + lax.dot_general(kd, v, tn, preferred_element_type=F32)
        g = g_ref[0, h].astype(F32)
        y = _rms(o) * gn_ref[h:h + 1, :]
        o_ref[:, h * hd:(h + 1) * hd] = (y * (g * _sigmoid(g))).astype(o_ref.dtype)

    @pl.when(c == pl.num_programs(1) - 1)
    def _():
        s_out_ref[0] = s_ref[...]


def _retention(q, k, v, g, s0, gn):
    batch, n_heads, seq, hd = q.shape
    chunk = min(256, seq)
    nc = seq // chunk
    log_gammas = tuple(math.log(1.0 - 2.0 ** (-5.0 - h)) for h in range(n_heads))
    qkv_spec = pl.BlockSpec((1, n_heads, chunk, hd), lambda b, c: (b, 0, c, 0))
    state_spec = pl.BlockSpec((1, n_heads, hd, hd), lambda b, c: (b, 0, 0, 0))
    return pl.pallas_call(
        functools.partial(_retention_kernel, log_gammas=log_gammas),
        grid=(batch, nc),
        in_specs=[qkv_spec] * 4 + [state_spec, _resident((n_heads, hd), lambda b, c: (0, 0))],
        out_specs=[pl.BlockSpec((chunk, n_heads * hd), lambda b, c: (b * nc + c, 0)), state_spec],
        out_shape=[jax.ShapeDtypeStruct((batch * seq, n_heads * hd), BF16),
                   jax.ShapeDtypeStruct((batch, n_heads, hd, hd), F32)],
        scratch_shapes=[pltpu.VMEM((n_heads, hd, hd), F32), pltpu.VMEM((n_heads, chunk, chunk), F32),
                        pltpu.VMEM((n_heads, chunk, hd), F32), pltpu.VMEM((n_heads, chunk, hd), F32)],
        compiler_params=_params(2),
    )(q, k, v, g, s0, gn)


def _outproj_kernel(of_ref, or_ref, x_ref, w_ref, ln2_ref, xm_ref, h2_ref):
    half = of_ref.shape[1]
    y = jnp.dot(of_ref[...], w_ref[0:half, :], preferred_element_type=F32)
    y = y + jnp.dot(or_ref[...], w_ref[half:2 * half, :], preferred_element_type=F32)
    xm = x_ref[...] + y
    xm_ref[...] = xm
    h2_ref[...] = (_rms(xm) * ln2_ref[...]).astype(BF16)


def _outproj(o_fox, o_ret, x, w_out, ln2):
    tokens, d = x.shape
    half = o_fox.shape[1]
    tm = min(512, tokens)
    row = lambda i: (i, 0)
    return pl.pallas_call(
        _outproj_kernel,
        grid=(tokens // tm,),
        in_specs=[pl.BlockSpec((tm, half), row), pl.BlockSpec((tm, half), row),
                  pl.BlockSpec((tm, d), row),
                  _resident((2 * half, d), lambda i: (0, 0)), _resident((1, d), lambda i: (0, 0))],
        out_specs=[pl.BlockSpec((tm, d), row), pl.BlockSpec((tm, d), row)],
        out_shape=[jax.ShapeDtypeStruct((tokens, d), F32), jax.ShapeDtypeStruct((tokens, d), BF16)],
        compiler_params=_params(1),
    )(o_fox, o_ret, x, w_out, ln2)


def _convffn_kernel(*refs, n_seg, seg, tps, carried):
    if carried:
        h2_ref, xm_ref, wu_ref, cw_ref, cb_ref, wd_ref, out_ref, conv_ref, carry_ref = refs
    else:
        h2_ref, xm_ref, wu_ref, cw_ref, cb_ref, wd_ref, hist_ref, out_ref, conv_ref = refs
    i = pl.program_id(0)
    j = pl.program_id(1)
    width = wu_ref.shape[2]
    tf = width // 2
    u = jnp.dot(h2_ref[...], wu_ref[0], preferred_element_type=F32)
    w0 = cw_ref[0, 0:1, :]
    w1 = cw_ref[0, 1:2, :]
    w2 = cw_ref[0, 2:3, :]
    row = lax.broadcasted_iota(jnp.int32, (seg, width), 0)
    col0 = pl.multiple_of(j * width, width)

    if carried:
        @pl.when(i % tps == 0)
        def _():
            carry_ref[0:2, pl.ds(col0, width)] = jnp.zeros((2, width), F32)

    pieces = []
    for s in range(n_seg):
        us = u[s * seg:(s + 1) * seg]
        if carried:
            h0 = carry_ref[0:1, pl.ds(col0, width)]
            h1 = carry_ref[1:2, pl.ds(col0, width)]
        else:
            h0 = hist_ref[s, 0:1, :]
            h1 = hist_ref[s, 1:2, :]
        um1 = jnp.where(row == 0, h1, pltpu.roll(us, 1, 0))
        um2 = jnp.where(row == 0, h0, jnp.where(row == 1, h1, pltpu.roll(us, 2, 0)))
        uc = cb_ref[0] + (um2 * w0 + um1 * w1 + us * w2)
        a = uc[:, :tf]
        g = uc[:, tf:]
        pieces.append(((g * _sigmoid(g)) * a).astype(BF16))
        conv_ref[s] = us[seg - 2:seg, :]
    if carried:
        carry_ref[0:2, pl.ds(col0, width)] = u[n_seg * seg - 2:n_seg * seg, :]
    z = pieces[0] if n_seg == 1 else jnp.concatenate(pieces, axis=0)
    y = jnp.dot(z, wd_ref[0], preferred_element_type=F32)

    @pl.when(j == 0)
    def _():
        out_ref[...] = xm_ref[...] + y

    @pl.when(j > 0)
    def _():
        out_ref[...] += y


def _convffn(h2, xm, wu_tiles, cw_tiles, cb_tiles, wd_tiles, hist, *, batch, seq):
    tokens, d = xm.shape
    nf, _, width = wu_tiles.shape
    tm = min(512, tokens)
    seg = min(seq, tm)
    n_seg = tm // seg
    tps = seq // seg
    carried = hist is None
    assert not carried or n_seg == 1, "a carried conv state needs one sequence per token tile"
    row = lambda i, j: (i, 0)
    in_specs = [pl.BlockSpec((tm, d), row), pl.BlockSpec((tm, d), row),
                pl.BlockSpec((1, d, width), lambda i, j: (j, 0, 0)),
                pl.BlockSpec((1, 3, width), lambda i, j: (j, 0, 0)),
                pl.BlockSpec((1, 1, width), lambda i, j: (j, 0, 0)),
                pl.BlockSpec((1, width // 2, d), lambda i, j: (j, 0, 0))]
    args = [h2, xm, wu_tiles, cw_tiles, cb_tiles, wd_tiles]
    scratch = []
    if carried:
        scratch = [pltpu.VMEM((8, nf * width), F32)]
    else:
        in_specs.append(pl.BlockSpec((n_seg, 2, width), lambda i, j: (i, 0, j)))
        args.append(hist)
    out, tails = pl.pallas_call(
        functools.partial(_convffn_kernel, n_seg=n_seg, seg=seg, tps=tps, carried=carried),
        grid=(tokens // tm, nf),
        in_specs=in_specs,
        out_specs=[pl.BlockSpec((tm, d), row),
                   pl.BlockSpec((n_seg, 2, width), lambda i, j: (i, 0, j))],
        out_shape=[jax.ShapeDtypeStruct((tokens, d), F32),
                   jax.ShapeDtypeStruct((batch * tps, 2, nf * width), F32)],
        scratch_shapes=scratch,
        compiler_params=_params(2),
    )(*args)
    return out, tails.reshape(batch, tps, 2, nf * width)[:, tps - 1]


def _rope_tables(pos, rows):
    half = LANES // 2
    inv = jnp.power(ROPE_BASE, -jnp.arange(half, dtype=F32) / half)
    ang = pos.astype(F32)[:, None] * inv[None, :]
    cos = jnp.cos(ang)
    sin = jnp.sin(ang)
    cos_t = jnp.concatenate([cos, cos], axis=1)
    sin_t = jnp.concatenate([-sin, sin], axis=1)
    reps = rows // pos.shape[0]
    return jnp.tile(cos_t, (reps, 1)), jnp.tile(sin_t, (reps, 1))


def _to_tiles(a, tf):
    f = a.shape[-1] // 2
    lead = a.shape[:-1]
    return a.reshape(*lead, 2, f // tf, tf).swapaxes(-3, -2).reshape(*lead, 2 * f)


def _from_tiles(a, tf):
    f = a.shape[-1] // 2
    lead = a.shape[:-1]
    return a.reshape(*lead, f // tf, 2, tf).swapaxes(-3, -2).reshape(*lead, 2 * f)


def kernel(x_prompt, x_sample, cache_fox_k, cache_fox_v, cache_fox_logf, state_ret, state_conv,
           ln1, w_in, b_f, fox_qn, fox_kn, ret_gn, w_out, ln2, w_up, conv_w, conv_b, w_down):
    depth = ln1.shape[0]
    bp, tp, d = x_prompt.shape
    bs, ts, _ = x_sample.shape
    past = cache_fox_k.shape[2]
    n_heads, hd = cache_fox_k.shape[3], cache_fox_k.shape[4]
    d_head = n_heads * hd
    f = w_down.shape[1]
    tf = min(512, f)
    nf = f // tf
    assert hd == LANES and n_heads % HEADS_PER_STEP == 0 and ret_gn.shape[1:] == (n_heads, hd)

    cos_p, sin_p = _rope_tables(jnp.arange(tp), max(tp, min(512, bp * tp)))
    cos_s, sin_s = _rope_tables(past + jnp.arange(ts), max(ts, min(512, bs * ts)))

    xp = x_prompt.reshape(bp * tp, d)
    xs = x_sample.reshape(bs * ts, d)
    st_p, st_s = [], []
    for l in range(depth):
        kinds = [w_in[l][:, 0:d_head], w_in[l][:, d_head:2 * d_head], w_in[l][:, 2 * d_head:3 * d_head]]
        r0 = 3 * d_head + n_heads
        kinds += [w_in[l][:, r0 + k * d_head:r0 + (k + 1) * d_head] for k in range(4)]
        gcols = HEADS_PER_STEP * hd
        w_groups = (jnp.stack(kinds, axis=0).reshape(N_KINDS, d, d_head // gcols, gcols)
                    .transpose(2, 1, 0, 3).reshape(d_head // gcols, d, N_KINDS * gcols).astype(BF16))
        wf = jnp.pad(w_in[l][:, 3 * d_head:r0], ((0, 0), (0, hd - n_heads))).astype(BF16)
        bfp = jnp.pad(b_f[l], (0, hd - n_heads)).reshape(1, hd)
        qn = fox_qn[l].reshape(1, hd)
        kn = fox_kn[l].reshape(1, hd)
        wo = w_out[l].astype(BF16)
        wu_tiles = _to_tiles(w_up[l], tf).reshape(d, nf, 2 * tf).transpose(1, 0, 2).astype(BF16)
        cw_tiles = _to_tiles(conv_w[l], tf).reshape(3, nf, 2 * tf).transpose(1, 0, 2)
        cb_tiles = _to_tiles(conv_b[l], tf).reshape(nf, 1, 2 * tf)
        wd_tiles = w_down[l].reshape(nf, tf, d).astype(BF16)
        ln1_l = ln1[l].reshape(1, d)
        ln2_l = ln2[l].reshape(1, d)

        def group(x, cos_t, sin_t, batch, seq):
            return _inproj(x, ln1_l, w_groups, wf, bfp, qn, kn, cos_t, sin_t,
                           batch=batch, seq=seq, n_heads=n_heads)

        fq, fkb, fvb, rq, rk, rv, rg, fk, fv, lf = group(xp, cos_p, sin_p, bp, tp)
        c_row = _cumsum(lf.reshape(bp, tp, n_heads).transpose(0, 2, 1))
        o_fox = _fox_prompt(fq, fkb, fvb, c_row.reshape(bp, n_heads, 1, tp))
        o_ret, s_ret_p = _retention(rq, rk, rv, rg, jnp.zeros((bp, n_heads, hd, hd), F32), ret_gn[l])
        xm, h2 = _outproj(o_fox, o_ret, xp, wo, ln2_l)
        xp, conv_p = _convffn(h2, xm, wu_tiles, cw_tiles, cb_tiles, wd_tiles, None, batch=bp, seq=tp)
        st_p.append((fk.reshape(bp, tp, n_heads, hd), fv.reshape(bp, tp, n_heads, hd),
                     lf.reshape(bp, tp, n_heads), s_ret_p, _from_tiles(conv_p, tf)))

        fq, fkb, fvb, rq, rk, rv, rg, fk, fv, lf = group(xs, cos_s, sin_s, bs, ts)
        lf_all = jnp.concatenate([cache_fox_logf[l], lf.reshape(bs, ts, n_heads)], axis=1)
        pad = (-lf_all.shape[1]) % LANES
        lf_all = jnp.pad(lf_all, ((0, 0), (0, pad), (0, 0)))
        c_row = _cumsum(lf_all.transpose(0, 2, 1))
        o_fox = _fox_sample(fq, fkb, fvb, cache_fox_k[l].reshape(bs, past, d_head),
                            cache_fox_v[l].reshape(bs, past, d_head), c_row)
        o_ret, s_ret_s = _retention(rq, rk, rv, rg, state_ret[l], ret_gn[l])
        xm, h2 = _outproj(o_fox, o_ret, xs, wo, ln2_l)
        xs, conv_s = _convffn(h2, xm, wu_tiles, cw_tiles, cb_tiles, wd_tiles,
                              _to_tiles(state_conv[l], tf), batch=bs, seq=ts)
        st_s.append((fk.reshape(bs, ts, n_heads, hd), fv.reshape(bs, ts, n_heads, hd),
                     lf.reshape(bs, ts, n_heads), s_ret_s, _from_tiles(conv_s, tf)))

    stack = lambda st, k: jnp.stack([s[k] for s in st])
    return (xp.reshape(bp, tp, d), xs.reshape(bs, ts, d),
            stack(st_p, 0), stack(st_p, 1), stack(st_p, 2), stack(st_p, 3), stack(st_p, 4),
            stack(st_s, 0), stack(st_s, 1), stack(st_s, 2), stack(st_s, 3), stack(st_s, 4))
```

```python
import functools
import math

import jax
import jax.numpy as jnp
from jax import lax
from jax.experimental import pallas as pl
from jax.experimental.pallas import tpu as pltpu

EPS = 1e-6
ROPE_BASE = 10000.0
MASKED_LOGIT = -1e30
LANES = 128
VMEM_LIMIT_BYTES = 56 * 1024 * 1024
LOG2E = math.log2(math.e)

F32 = jnp.float32
BF16 = jnp.bfloat16

N_FOX_KINDS = 3
N_RET_KINDS = 4
HEADS_PER_STEP = 2
FOX_HEADS_PER_STEP = 2

NT_DIMS = (((1,), (1,)), ((), ()))
TN_DIMS = (((0,), (0,)), ((), ()))


def _params(n_axes):
    return pltpu.CompilerParams(dimension_semantics=("arbitrary",) * n_axes,
                                vmem_limit_bytes=VMEM_LIMIT_BYTES)


def _resident(block_shape, index_map):
    return pl.BlockSpec(block_shape, index_map, pipeline_mode=pl.Buffered(1))


def _rms(x):
    return x * lax.rsqrt(jnp.mean(x * x, axis=-1, keepdims=True) + EPS)


def _sigmoid(x):
    return 1.0 / (1.0 + jnp.exp(-x))


def _inproj_kernel(x_ref, ln1_ref, wfq_ref, wfk_ref, wfv_ref, wrq_ref, wrk_ref, wrv_ref, wrg_ref,
                   wf_ref, bf_ref, qn_ref, kn_ref, cos_ref, sin_ref,
                   fq_ref, fkb_ref, fvb_ref, rq_ref, rk_ref, rv_ref, rg_ref, fk_ref, fv_ref, logf_ref,
                   h_ref, *, nb, seg, n_fox):
    j = pl.program_id(1)

    @pl.when(j == 0)
    def _():
        h = (_rms(x_ref[...]) * ln1_ref[...]).astype(BF16)
        h_ref[...] = h
        z = jnp.dot(h, wf_ref[...], preferred_element_type=F32) + bf_ref[...]
        logf = jnp.minimum(z, 0.0) - jnp.log1p(jnp.exp(-jnp.abs(z)))
        logf_ref[...] = logf[:, :n_fox]

    cos = cos_ref[...]
    sin = sin_ref[...]
    hd = LANES

    def project(w_ref):
        p = jnp.dot(h_ref[...], w_ref[...], preferred_element_type=F32)
        return [p[:, e * hd:(e + 1) * hd] for e in range(HEADS_PER_STEP)]

    def head_major(a):
        return a.reshape(nb, seg, hd).astype(BF16)

    def rope(a):
        return a * cos + pltpu.roll(a, hd // 2, 1) * sin

    for e, a in enumerate(project(wfq_ref)):
        fq_ref[:, e] = head_major(_rms(a) * qn_ref[...])
    for e, a in enumerate(project(wfk_ref)):
        fk = _rms(a) * kn_ref[...]
        fk_ref[:, e * hd:(e + 1) * hd] = fk
        fkb_ref[:, e] = head_major(fk)
    for e, a in enumerate(project(wfv_ref)):
        fv_ref[:, e * hd:(e + 1) * hd] = a
        fvb_ref[:, e] = head_major(a)
    for e, a in enumerate(project(wrq_ref)):
        rq_ref[:, e] = head_major(rope(a))
    for e, a in enumerate(project(wrk_ref)):
        rk_ref[:, e] = head_major(rope(a) * (hd ** -0.5))
    for e, a in enumerate(project(wrv_ref)):
        rv_ref[:, e] = head_major(a)
    for e, a in enumerate(project(wrg_ref)):
        rg_ref[:, e] = head_major(a)


def _inproj(x, ln1, w_fox, w_ret, wf, bfp, qn, kn, cos_t, sin_t, *, batch, seq, n_heads):
    tokens, d = x.shape
    hd = LANES
    tm = min(512, tokens)
    seg = min(seq, tm)
    nb = tm // seg
    tps = seq // seg
    n_groups = n_heads // HEADS_PER_STEP
    gcols = HEADS_PER_STEP * hd
    grid = (tokens // tm, n_groups)

    def w_spec(kind):
        return pl.BlockSpec((d, gcols), lambda i, j: (0, kind * n_groups + j))

    hm_shape = jax.ShapeDtypeStruct((batch, n_heads, seq, hd), BF16)
    hm_spec = pl.BlockSpec((nb, HEADS_PER_STEP, seg, hd), lambda i, j: (i // tps, j, i % tps, 0))
    tok_shape = jax.ShapeDtypeStruct((tokens, n_heads * hd), F32)
    tok_spec = pl.BlockSpec((tm, gcols), lambda i, j: (i, j))

    return pl.pallas_call(
        functools.partial(_inproj_kernel, nb=nb, seg=seg, n_fox=n_heads),
        grid=grid,
        in_specs=[pl.BlockSpec((tm, d), lambda i, j: (i, 0)),
                  _resident((1, d), lambda i, j: (0, 0))]
                 + [w_spec(k) for k in range(N_FOX_KINDS)] + [w_spec(k) for k in range(N_RET_KINDS)]
                 + [_resident((d, hd), lambda i, j: (0, 0)),
                    _resident((1, hd), lambda i, j: (0, 0)),
                    _resident((1, hd), lambda i, j: (0, 0)),
                    _resident((1, hd), lambda i, j: (0, 0)),
                    pl.BlockSpec((tm, hd), lambda i, j: (i % tps, 0)),
                    pl.BlockSpec((tm, hd), lambda i, j: (i % tps, 0))],
        out_specs=[hm_spec] * 7 + [tok_spec, tok_spec,
                                   pl.BlockSpec((tm, n_heads), lambda i, j: (i, 0))],
        out_shape=[hm_shape] * 7 + [tok_shape, tok_shape,
                                    jax.ShapeDtypeStruct((tokens, n_heads), F32)],
        scratch_shapes=[pltpu.VMEM((tm, d), BF16)],
        compiler_params=_params(2),
    )(x, ln1, w_fox, w_fox, w_fox, w_ret, w_ret, w_ret, w_ret, wf, bfp, qn, kn, cos_t, sin_t)


def _cumsum_kernel(lf_ref, c_ref, *, ch):
    total = lf_ref.shape[2]
    r = lax.broadcasted_iota(jnp.int32, (ch, ch), 0)
    c = lax.broadcasted_iota(jnp.int32, (ch, ch), 1)
    upper = (r <= c).astype(F32)
    carry = jnp.zeros((lf_ref.shape[1], 1), F32)
    for k in range(total // ch):
        x = lf_ref[0, :, k * ch:(k + 1) * ch]
        cs = jnp.dot(x, upper, precision=lax.Precision.HIGHEST, preferred_element_type=F32) + carry
        c_ref[0, :, k * ch:(k + 1) * ch] = cs
        carry = cs[:, ch - 1:ch]


def _cumsum(lf_t):
    batch, n_heads, total = lf_t.shape
    ch = 256 if total % 256 == 0 else LANES
    spec = pl.BlockSpec((1, n_heads, total), lambda b: (b, 0, 0))
    return pl.pallas_call(
        functools.partial(_cumsum_kernel, ch=ch),
        grid=(batch,), in_specs=[spec], out_specs=spec,
        out_shape=jax.ShapeDtypeStruct(lf_t.shape, F32),
        compiler_params=_params(1),
    )(lf_t)


def _softmax_step(z, v, m_ref, l_ref, acc_ref):
    tk = z.shape[1]
    m_prev = m_ref[...]
    m_next = jnp.maximum(m_prev, jnp.max(z, axis=1, keepdims=True))
    p = jnp.exp2(z - pltpu.repeat(m_next, tk // LANES, 1))
    alpha = jnp.exp2(m_prev - m_next)
    l_ref[...] = alpha * l_ref[...] + jnp.sum(p, axis=1, keepdims=True)
    acc_ref[...] = alpha * acc_ref[...] + jnp.dot(p.astype(BF16), v, preferred_element_type=F32)
    m_ref[...] = m_next


def _fox_prompt_kernel(q_ref, k_ref, v_ref, c_ref, o_ref, m_ref, l_ref, acc_ref, *, tq, scale):
    qi = pl.program_id(2)
    group, hd = q_ref.shape[1], q_ref.shape[3]
    m_ref[...] = jnp.full(m_ref.shape, MASKED_LOGIT, F32)
    l_ref[...] = jnp.zeros(l_ref.shape, F32)
    acc_ref[...] = jnp.zeros(acc_ref.shape, F32)

    def logits(e, k0):
        k = k_ref[0, e, pl.ds(k0, tq), :]
        s = lax.dot_general(q_ref[0, e], k, NT_DIMS, preferred_element_type=F32)
        return s * (scale * LOG2E) - c_ref[0, e, :, pl.ds(k0, tq)] * LOG2E

    def body(j, carry):
        k0 = pl.multiple_of(j * tq, tq)
        for e in range(group):
            _softmax_step(logits(e, k0), v_ref[0, e, pl.ds(k0, tq), :],
                          m_ref.at[e], l_ref.at[e], acc_ref.at[e])
        return carry

    lax.fori_loop(0, qi, body, 0)

    k0 = pl.multiple_of(qi * tq, tq)
    row = lax.broadcasted_iota(jnp.int32, (tq, tq), 0)
    col = lax.broadcasted_iota(jnp.int32, (tq, tq), 1)
    for e in range(group):
        z = jnp.where(col <= row, logits(e, k0), MASKED_LOGIT)
        _softmax_step(z, v_ref[0, e, pl.ds(k0, tq), :], m_ref.at[e], l_ref.at[e], acc_ref.at[e])
        o_ref[:, e * hd:(e + 1) * hd] = (acc_ref[e] / l_ref[e]).astype(o_ref.dtype)


def _fox_prompt(q, k, v, c_row):
    batch, n_heads, seq, hd = q.shape
    tq = min(512, seq)
    nq = seq // tq
    group = FOX_HEADS_PER_STEP
    kv_spec = pl.BlockSpec((1, group, seq, hd), lambda b, h, i: (b, h, 0, 0))
    return pl.pallas_call(
        functools.partial(_fox_prompt_kernel, tq=tq, scale=hd ** -0.5),
        grid=(batch, n_heads // group, nq),
        in_specs=[pl.BlockSpec((1, group, tq, hd), lambda b, h, i: (b, h, i, 0)), kv_spec, kv_spec,
                  pl.BlockSpec((1, group, 1, seq), lambda b, h, i: (b, h, 0, 0))],
        out_specs=pl.BlockSpec((tq, group * hd), lambda b, h, i: (b * nq + i, h)),
        out_shape=jax.ShapeDtypeStruct((batch * seq, n_heads * hd), BF16),
        scratch_shapes=[pltpu.VMEM((group, tq, LANES), F32), pltpu.VMEM((group, tq, LANES), F32),
                        pltpu.VMEM((group, tq, hd), F32)],
        compiler_params=_params(3),
    )(q, k, v, c_row)


def _fox_sample_kernel(q_ref, kn_ref, vn_ref, kc_ref, vc_ref, c_ref, o_ref, *, past, scale):
    n_heads, new, hd = q_ref.shape[1:]
    row = lax.broadcasted_iota(jnp.int32, (new, new), 0)
    col = lax.broadcasted_iota(jnp.int32, (new, new), 1)
    for h in range(n_heads):
        q = q_ref[0, h]
        kc = kc_ref[0, pl.ds(h, past, stride=n_heads), :].astype(BF16)
        vc = vc_ref[0, pl.ds(h, past, stride=n_heads), :].astype(BF16)
        z_c = (lax.dot_general(q, kc, NT_DIMS, preferred_element_type=F32) * scale
               - c_ref[0, h:h + 1, 0:past])
        z_n = (lax.dot_general(q, kn_ref[0, h], NT_DIMS, preferred_element_type=F32) * scale
               - c_ref[0, h:h + 1, past:past + new])
        z_n = jnp.where(col <= row, z_n, MASKED_LOGIT)
        m = jnp.maximum(jnp.max(z_c, axis=1, keepdims=True), jnp.max(z_n, axis=1, keepdims=True))
        p_c = jnp.exp(z_c - m)
        p_n = jnp.exp(z_n - m)
        l = jnp.sum(p_c, axis=1, keepdims=True) + jnp.sum(p_n, axis=1, keepdims=True)
        acc = (jnp.dot(p_c.astype(BF16), vc, preferred_element_type=F32)
               + jnp.dot(p_n.astype(BF16), vn_ref[0, h], preferred_element_type=F32))
        o_ref[:, h * hd:(h + 1) * hd] = (acc / l).astype(o_ref.dtype)


def _fox_sample(q, k_new, v_new, cache_k, cache_v, c_row, *, layer):
    batch, n_heads, new, hd = q.shape
    past = cache_k.shape[1] // n_heads
    new_spec = pl.BlockSpec((1, n_heads, new, hd), lambda b: (b, 0, 0, 0))
    cache_spec = pl.BlockSpec((1, past * n_heads, hd), lambda b: (layer * batch + b, 0, 0))
    return pl.pallas_call(
        functools.partial(_fox_sample_kernel, past=past, scale=hd ** -0.5),
        grid=(batch,),
        in_specs=[new_spec, new_spec, new_spec, cache_spec, cache_spec,
                  pl.BlockSpec((1, n_heads, c_row.shape[2]), lambda b: (b, 0, 0))],
        out_specs=pl.BlockSpec((new, n_heads * hd), lambda b: (b, 0)),
        out_shape=jax.ShapeDtypeStruct((batch * new, n_heads * hd), BF16),
        compiler_params=_params(1),
    )(q, k_new, v_new, cache_k, cache_v, c_row)


def _retention_kernel(q_ref, k_ref, v_ref, g_ref, s0_ref, gn_ref, o_ref, s_out_ref,
                      s_ref, decay_ref, inter_ref, upd_ref, *, log_gammas):
    b = pl.program_id(0)
    c = pl.program_id(1)
    n_heads, chunk, hd = q_ref.shape[1:]

    @pl.when((b == 0) & (c == 0))
    def _():
        i = lax.broadcasted_iota(jnp.int32, (chunk, chunk), 0)
        jj = lax.broadcasted_iota(jnp.int32, (chunk, chunk), 1)
        diff = (i - jj).astype(F32)
        pos = lax.broadcasted_iota(jnp.int32, (chunk, hd), 0).astype(F32)
        for h, lg in enumerate(log_gammas):
            decay_ref[h] = jnp.where(diff >= 0.0, jnp.exp(jnp.maximum(diff, 0.0) * lg), 0.0)
            inter_ref[h] = jnp.exp((pos + 1.0) * lg)
            upd_ref[h] = jnp.exp((chunk - 1.0 - pos) * lg)

    @pl.when(c == 0)
    def _():
        s_ref[...] = s0_ref[0]

    for h, lg in enumerate(log_gammas):
        q = q_ref[0, h]
        k = k_ref[0, h]
        v = v_ref[0, h]
        s_prev = s_ref[h]
        scores = lax.dot_general(q, k, NT_DIMS, preferred_element_type=F32) * decay_ref[h]
        o = jnp.dot(scores.astype(BF16), v, preferred_element_type=F32)
        o = o + jnp.dot(q, s_prev.astype(BF16), preferred_element_type=F32) * inter_ref[h]
        kd = (k.astype(F32) * upd_ref[h]).astype(BF16)
        s_ref[h] = math.exp(chunk * lg) * s_prev + lax.dot_general(kd, v, TN_DIMS, preferred_element_type=F32)
        g = g_ref[0, h].astype(F32)
        y = _rms(o) * gn_ref[h:h + 1, :]
        o_ref[:, h * hd:(h + 1) * hd] = (y * (g * _sigmoid(g))).astype(o_ref.dtype)

    @pl.when(c == pl.num_programs(1) - 1)
    def _():
        s_out_ref[0] = s_ref[...]


def _retention(q, k, v, g, s0, gn, *, s0_block0):
    batch, n_heads, seq, hd = q.shape
    chunk = min(256, seq)
    nc = seq // chunk
    log_gammas = tuple(math.log(1.0 - 2.0 ** (-5.0 - h)) for h in range(n_heads))
    qkv_spec = pl.BlockSpec((1, n_heads, chunk, hd), lambda b, c: (b, 0, c, 0))
    return pl.pallas_call(
        functools.partial(_retention_kernel, log_gammas=log_gammas),
        grid=(batch, nc),
        in_specs=[qkv_spec] * 4 + [pl.BlockSpec((1, n_heads, hd, hd), lambda b, c: (s0_block0 + b, 0, 0, 0)),
                                   _resident((n_heads, hd), lambda b, c: (0, 0))],
        out_specs=[pl.BlockSpec((chunk, n_heads * hd), lambda b, c: (b * nc + c, 0)),
                   pl.BlockSpec((1, n_heads, hd, hd), lambda b, c: (b, 0, 0, 0))],
        out_shape=[jax.ShapeDtypeStruct((batch * seq, n_heads * hd), BF16),
                   jax.ShapeDtypeStruct((batch, n_heads, hd, hd), F32)],
        scratch_shapes=[pltpu.VMEM((n_heads, hd, hd), F32), pltpu.VMEM((n_heads, chunk, chunk), F32),
                        pltpu.VMEM((n_heads, chunk, hd), F32), pltpu.VMEM((n_heads, chunk, hd), F32)],
        compiler_params=_params(2),
    )(q, k, v, g, s0, gn)


def _outproj_kernel(of_ref, or_ref, x_ref, w_ref, ln2_ref, xm_ref, h2_ref, wb_ref):
    @pl.when(pl.program_id(0) == 0)
    def _():
        wb_ref[...] = w_ref[...].astype(BF16)

    half = of_ref.shape[1]
    y = jnp.dot(of_ref[...], wb_ref[0:half, :], preferred_element_type=F32)
    y = y + jnp.dot(or_ref[...], wb_ref[half:2 * half, :], preferred_element_type=F32)
    xm = x_ref[...] + y
    xm_ref[...] = xm
    h2_ref[...] = (_rms(xm) * ln2_ref[...]).astype(BF16)


def _outproj(o_fox, o_ret, x, w_out, ln2):
    tokens, d = x.shape
    half = o_fox.shape[1]
    tm = min(512, tokens)
    row = lambda i: (i, 0)
    return pl.pallas_call(
        _outproj_kernel,
        grid=(tokens // tm,),
        in_specs=[pl.BlockSpec((tm, half), row), pl.BlockSpec((tm, half), row),
                  pl.BlockSpec((tm, d), row),
                  _resident((2 * half, d), lambda i: (0, 0)), _resident((1, d), lambda i: (0, 0))],
        out_specs=[pl.BlockSpec((tm, d), row), pl.BlockSpec((tm, d), row)],
        out_shape=[jax.ShapeDtypeStruct((tokens, d), F32), jax.ShapeDtypeStruct((tokens, d), BF16)],
        scratch_shapes=[pltpu.VMEM((2 * half, d), BF16)],
        compiler_params=_params(1),
    )(o_fox, o_ret, x, w_out, ln2)


def _causal_conv3(u, h0, h1, cw_ref, cb_ref):
    row = lax.broadcasted_iota(jnp.int32, u.shape, 0)
    um1 = jnp.where(row == 0, h1, pltpu.roll(u, 1, 0))
    um2 = jnp.where(row == 0, h0, jnp.where(row == 1, h1, pltpu.roll(u, 2, 0)))
    return cb_ref[...] + (um2 * cw_ref[0:1, :] + um1 * cw_ref[1:2, :] + u * cw_ref[2:3, :])


def _convffn_kernel(*refs, n_seg, seg, tps, carried):
    if carried:
        (h2_ref, xm_ref, wa_ref, wg_ref, cwa_ref, cwg_ref, cba_ref, cbg_ref, wd_ref,
         out_ref, ta_ref, tg_ref, ca_ref, cg_ref) = refs
    else:
        (h2_ref, xm_ref, wa_ref, wg_ref, cwa_ref, cwg_ref, cba_ref, cbg_ref, wd_ref, ha_ref, hg_ref,
         out_ref, ta_ref, tg_ref) = refs
    i = pl.program_id(0)
    j = pl.program_id(1)
    tf = wa_ref.shape[1]
    tm = n_seg * seg
    h2 = h2_ref[...]
    col0 = pl.multiple_of(j * tf, tf)

    def half(w_ref, cw_ref, cb_ref, tail_ref, carry_ref, hist_ref):
        u = jnp.dot(h2, w_ref[...], preferred_element_type=F32)
        if carried:
            @pl.when(i % tps == 0)
            def _():
                carry_ref[0:2, pl.ds(col0, tf)] = jnp.zeros((2, tf), F32)
        pieces = []
        for s in range(n_seg):
            us = u[s * seg:(s + 1) * seg]
            if carried:
                h0 = carry_ref[0:1, pl.ds(col0, tf)]
                h1 = carry_ref[1:2, pl.ds(col0, tf)]
            else:
                h0 = hist_ref[s, 0:1, :]
                h1 = hist_ref[s, 1:2, :]
            pieces.append(_causal_conv3(us, h0, h1, cw_ref, cb_ref))
            tail_ref[s] = us[seg - 2:seg, :]
        if carried:
            carry_ref[0:2, pl.ds(col0, tf)] = u[tm - 2:tm, :]
        return pieces[0] if n_seg == 1 else jnp.concatenate(pieces, axis=0)

    a = half(wa_ref, cwa_ref, cba_ref, ta_ref, ca_ref if carried else None, None if carried else ha_ref)
    g = half(wg_ref, cwg_ref, cbg_ref, tg_ref, cg_ref if carried else None, None if carried else hg_ref)
    z = ((g * _sigmoid(g)) * a).astype(BF16)
    y = jnp.dot(z, wd_ref[...], preferred_element_type=F32)

    @pl.when(j == 0)
    def _():
        out_ref[...] = xm_ref[...] + y

    @pl.when(j > 0)
    def _():
        out_ref[...] += y


def _convffn(h2, xm, w_up, conv_w, conv_b, w_down, hist, *, batch, seq):
    tokens, d = xm.shape
    f = w_down.shape[0]
    tf = min(512, f)
    nf = f // tf
    tm = min(512, tokens)
    seg = min(seq, tm)
    n_seg = tm // seg
    tps = seq // seg
    carried = hist is None
    assert not carried or n_seg == 1, "a carried conv state needs one sequence per token tile"
    row = lambda i, j: (i, 0)
    a_col = lambda i, j: (0, j)
    g_col = lambda i, j: (0, nf + j)
    in_specs = [pl.BlockSpec((tm, d), row), pl.BlockSpec((tm, d), row),
                pl.BlockSpec((d, tf), a_col), pl.BlockSpec((d, tf), g_col),
                pl.BlockSpec((3, tf), a_col), pl.BlockSpec((3, tf), g_col),
                pl.BlockSpec((1, tf), a_col), pl.BlockSpec((1, tf), g_col),
                pl.BlockSpec((tf, d), lambda i, j: (j, 0))]
    args = [h2, xm, w_up, w_up, conv_w, conv_w, conv_b, conv_b, w_down]
    scratch = []
    if carried:
        scratch = [pltpu.VMEM((8, f), F32), pltpu.VMEM((8, f), F32)]
    else:
        in_specs += [pl.BlockSpec((n_seg, 2, tf), lambda i, j: (i, 0, j)),
                     pl.BlockSpec((n_seg, 2, tf), lambda i, j: (i, 0, nf + j))]
        args += [hist, hist]
    tail_spec = pl.BlockSpec((n_seg, 2, tf), lambda i, j: (i, 0, j))
    tail_shape = jax.ShapeDtypeStruct((batch * tps, 2, f), F32)
    out, tails_a, tails_g = pl.pallas_call(
        functools.partial(_convffn_kernel, n_seg=n_seg, seg=seg, tps=tps, carried=carried),
        grid=(tokens // tm, nf),
        in_specs=in_specs,
        out_specs=[pl.BlockSpec((tm, d), row), tail_spec, tail_spec],
        out_shape=[jax.ShapeDtypeStruct((tokens, d), F32), tail_shape, tail_shape],
        scratch_shapes=scratch,
        compiler_params=_params(2),
    )(*args)
    last = lambda t: t.reshape(batch, tps, 2, f)[:, tps - 1]
    return out, jnp.concatenate([last(tails_a), last(tails_g)], axis=-1)


def _rope_tables(pos, rows):
    half = LANES // 2
    inv = jnp.power(ROPE_BASE, -jnp.arange(half, dtype=F32) / half)
    ang = pos.astype(F32)[:, None] * inv[None, :]
    cos = jnp.cos(ang)
    sin = jnp.sin(ang)
    cos_t = jnp.concatenate([cos, cos], axis=1)
    sin_t = jnp.concatenate([-sin, sin], axis=1)
    reps = rows // pos.shape[0]
    return jnp.tile(cos_t, (reps, 1)), jnp.tile(sin_t, (reps, 1))


def kernel(x_prompt, x_sample, cache_fox_k, cache_fox_v, cache_fox_logf, state_ret, state_conv,
           ln1, w_in, b_f, fox_qn, fox_kn, ret_gn, w_out, ln2, w_up, conv_w, conv_b, w_down):
    depth = ln1.shape[0]
    bp, tp, d = x_prompt.shape
    bs, ts, _ = x_sample.shape
    past = cache_fox_k.shape[2]
    n_heads, hd = cache_fox_k.shape[3], cache_fox_k.shape[4]
    d_head = n_heads * hd
    assert hd == LANES and n_heads % HEADS_PER_STEP == 0 and ret_gn.shape[1:] == (n_heads, hd)

    cos_p, sin_p = _rope_tables(jnp.arange(tp), max(tp, min(512, bp * tp)))
    cos_s, sin_s = _rope_tables(past + jnp.arange(ts), max(ts, min(512, bs * ts)))
    cache_k = cache_fox_k.reshape(depth * bs, past * n_heads, hd)
    cache_v = cache_fox_v.reshape(depth * bs, past * n_heads, hd)
    state_r = state_ret.reshape(depth * bs, n_heads, hd, hd)
    zero_state = jnp.zeros((bp, n_heads, hd, hd), F32)

    xp = x_prompt.reshape(bp * tp, d)
    xs = x_sample.reshape(bs * ts, d)
    st_p, st_s = [], []
    for l in range(depth):
        r0 = N_FOX_KINDS * d_head + n_heads
        w_fox = w_in[l][:, :N_FOX_KINDS * d_head].astype(BF16)
        w_ret = w_in[l][:, r0:].astype(BF16)
        wf = jnp.pad(w_in[l][:, N_FOX_KINDS * d_head:r0], ((0, 0), (0, hd - n_heads))).astype(BF16)
        bfp = jnp.pad(b_f[l], (0, hd - n_heads)).reshape(1, hd)
        qn = fox_qn[l].reshape(1, hd)
        kn = fox_kn[l].reshape(1, hd)
        wu = w_up[l].astype(BF16)
        wd = w_down[l].astype(BF16)
        cw = conv_w[l]
        cb = conv_b[l].reshape(1, -1)
        ln1_l = ln1[l].reshape(1, d)
        ln2_l = ln2[l].reshape(1, d)

        def group(x, cos_t, sin_t, batch, seq):
            return _inproj(x, ln1_l, w_fox, w_ret, wf, bfp, qn, kn, cos_t, sin_t,
                           batch=batch, seq=seq, n_heads=n_heads)

        fq, fkb, fvb, rq, rk, rv, rg, fk, fv, lf = group(xp, cos_p, sin_p, bp, tp)
        c_row = _cumsum(lf.reshape(bp, tp, n_heads).transpose(0, 2, 1))
        o_fox = _fox_prompt(fq, fkb, fvb, c_row.reshape(bp, n_heads, 1, tp))
        o_ret, s_ret_p = _retention(rq, rk, rv, rg, zero_state, ret_gn[l], s0_block0=0)
        xm, h2 = _outproj(o_fox, o_ret, xp, w_out[l], ln2_l)
        xp, conv_p = _convffn(h2, xm, wu, cw, cb, wd, None, batch=bp, seq=tp)
        st_p.append((fk.reshape(bp, tp, n_heads, hd), fv.reshape(bp, tp, n_heads, hd),
                     lf.reshape(bp, tp, n_heads), s_ret_p, conv_p))

        fq, fkb, fvb, rq, rk, rv, rg, fk, fv, lf = group(xs, cos_s, sin_s, bs, ts)
        lf_all = jnp.concatenate([cache_fox_logf[l], lf.reshape(bs, ts, n_heads)], axis=1)
        pad = (-lf_all.shape[1]) % LANES
        lf_all = jnp.pad(lf_all, ((0, 0), (0, pad), (0, 0)))
        c_row = _cumsum(lf_all.transpose(0, 2, 1))
        o_fox = _fox_sample(fq, fkb, fvb, cache_k, cache_v, c_row, layer=l)
        o_ret, s_ret_s = _retention(rq, rk, rv, rg, state_r, ret_gn[l], s0_block0=l * bs)
        xm, h2 = _outproj(o_fox, o_ret, xs, w_out[l], ln2_l)
        xs, conv_s = _convffn(h2, xm, wu, cw, cb, wd, state_conv[l], batch=bs, seq=ts)
        st_s.append((fk.reshape(bs, ts, n_heads, hd), fv.reshape(bs, ts, n_heads, hd),
                     lf.reshape(bs, ts, n_heads), s_ret_s, conv_s))

    stack = lambda st, k: jnp.stack([s[k] for s in st])
    return (xp.reshape(bp, tp, d), xs.reshape(bs, ts, d),
            stack(st_p, 0), stack(st_p, 1), stack(st_p, 2), stack(st_p, 3), stack(st_p, 4),
            stack(st_s, 0), stack(st_s, 1), stack(st_s, 2), stack(st_s, 3), stack(st_s, 4))
```

```python
import functools
import math

import jax
import jax.numpy as jnp
from jax import lax
from jax.experimental import pallas as pl
from jax.experimental.pallas import tpu as pltpu

EPS = 1e-6
ROPE_BASE = 10000.0
MASKED_LOGIT = -1e30
LANES = 128
VMEM_LIMIT_BYTES = 56 * 1024 * 1024
LOG2E = math.log2(math.e)

F32 = jnp.float32
BF16 = jnp.bfloat16

N_FOX_KINDS = 3
N_RET_KINDS = 4
HEADS_PER_STEP = 2
FOX_HEADS_PER_STEP = 2

NT_DIMS = (((1,), (1,)), ((), ()))
TN_DIMS = (((0,), (0,)), ((), ()))


def _params(n_axes):
    return pltpu.CompilerParams(dimension_semantics=("arbitrary",) * n_axes,
                                vmem_limit_bytes=VMEM_LIMIT_BYTES)


def _resident(block_shape, index_map):
    return pl.BlockSpec(block_shape, index_map, pipeline_mode=pl.Buffered(1))


def _rms(x):
    return x * lax.rsqrt(jnp.mean(x * x, axis=-1, keepdims=True) + EPS)


def _sigmoid(x):
    return 1.0 / (1.0 + jnp.exp(-x))


def _inproj_kernel(x_ref, ln1_ref, wfq_ref, wfk_ref, wfv_ref, wrq_ref, wrk_ref, wrv_ref, wrg_ref,
                   wf_ref, bf_ref, qn_ref, kn_ref, cos_ref, sin_ref,
                   fq_ref, fkb_ref, fvb_ref, rq_ref, rk_ref, rv_ref, rg_ref, fk_ref, fv_ref, logf_ref,
                   h_ref, *, nb, seg, n_fox):
    j = pl.program_id(1)

    @pl.when(j == 0)
    def _():
        h = (_rms(x_ref[...]) * ln1_ref[...]).astype(BF16)
        h_ref[...] = h
        z = jnp.dot(h, wf_ref[...], preferred_element_type=F32) + bf_ref[...]
        logf = jnp.minimum(z, 0.0) - jnp.log1p(jnp.exp(-jnp.abs(z)))
        logf_ref[...] = logf[:, :n_fox]

    cos = cos_ref[...]
    sin = sin_ref[...]
    hd = LANES

    def project(w_ref):
        p = jnp.dot(h_ref[...], w_ref[...], preferred_element_type=F32)
        return [p[:, e * hd:(e + 1) * hd] for e in range(HEADS_PER_STEP)]

    def head_major(a):
        return a.reshape(nb, seg, hd).astype(BF16)

    def rope(a):
        return a * cos + pltpu.roll(a, hd // 2, 1) * sin

    for e, a in enumerate(project(wfq_ref)):
        fq_ref[:, e] = head_major(_rms(a) * qn_ref[...])
    for e, a in enumerate(project(wfk_ref)):
        fk = _rms(a) * kn_ref[...]
        fk_ref[:, e * hd:(e + 1) * hd] = fk
        fkb_ref[:, e] = head_major(fk)
    for e, a in enumerate(project(wfv_ref)):
        fv_ref[:, e * hd:(e + 1) * hd] = a
        fvb_ref[:, e] = head_major(a)
    for e, a in enumerate(project(wrq_ref)):
        rq_ref[:, e] = head_major(rope(a))
    for e, a in enumerate(project(wrk_ref)):
        rk_ref[:, e] = head_major(rope(a) * (hd ** -0.5))
    for e, a in enumerate(project(wrv_ref)):
        rv_ref[:, e] = head_major(a)
    for e, a in enumerate(project(wrg_ref)):
        rg_ref[:, e] = head_major(a)


def _inproj(x, ln1, w_fox, w_ret, wf, bfp, qn, kn, cos_t, sin_t, *, batch, seq, n_heads):
    tokens, d = x.shape
    hd = LANES
    tm = min(512, tokens)
    seg = min(seq, tm)
    nb = tm // seg
    tps = seq // seg
    n_groups = n_heads // HEADS_PER_STEP
    gcols = HEADS_PER_STEP * hd
    grid = (tokens // tm, n_groups)

    def w_spec(kind):
        return pl.BlockSpec((d, gcols), lambda i, j: (0, kind * n_groups + j))

    hm_shape = jax.ShapeDtypeStruct((batch, n_heads, seq, hd), BF16)
    hm_spec = pl.BlockSpec((nb, HEADS_PER_STEP, seg, hd), lambda i, j: (i // tps, j, i % tps, 0))
    tok_shape = jax.ShapeDtypeStruct((tokens, n_heads * hd), F32)
    tok_spec = pl.BlockSpec((tm, gcols), lambda i, j: (i, j))

    return pl.pallas_call(
        functools.partial(_inproj_kernel, nb=nb, seg=seg, n_fox=n_heads),
        grid=grid,
        in_specs=[pl.BlockSpec((tm, d), lambda i, j: (i, 0)),
                  _resident((1, d), lambda i, j: (0, 0))]
                 + [w_spec(k) for k in range(N_FOX_KINDS)] + [w_spec(k) for k in range(N_RET_KINDS)]
                 + [_resident((d, hd), lambda i, j: (0, 0)),
                    _resident((1, hd), lambda i, j: (0, 0)),
                    _resident((1, hd), lambda i, j: (0, 0)),
                    _resident((1, hd), lambda i, j: (0, 0)),
                    pl.BlockSpec((tm, hd), lambda i, j: (i % tps, 0)),
                    pl.BlockSpec((tm, hd), lambda i, j: (i % tps, 0))],
        out_specs=[hm_spec] * 7 + [tok_spec, tok_spec,
                                   pl.BlockSpec((tm, n_heads), lambda i, j: (i, 0))],
        out_shape=[hm_shape] * 7 + [tok_shape, tok_shape,
                                    jax.ShapeDtypeStruct((tokens, n_heads), F32)],
        scratch_shapes=[pltpu.VMEM((tm, d), BF16)],
        compiler_params=_params(2),
    )(x, ln1, w_fox, w_fox, w_fox, w_ret, w_ret, w_ret, w_ret, wf, bfp, qn, kn, cos_t, sin_t)


def _cumsum_kernel(lf_ref, c_ref, *, ch):
    total = lf_ref.shape[2]
    r = lax.broadcasted_iota(jnp.int32, (ch, ch), 0)
    c = lax.broadcasted_iota(jnp.int32, (ch, ch), 1)
    upper = (r <= c).astype(F32)
    carry = jnp.zeros((lf_ref.shape[1], 1), F32)
    for k in range(total // ch):
        x = lf_ref[0, :, k * ch:(k + 1) * ch]
        cs = jnp.dot(x, upper, precision=lax.Precision.HIGHEST, preferred_element_type=F32) + carry
        c_ref[0, :, k * ch:(k + 1) * ch] = cs
        carry = cs[:, ch - 1:ch]


def _cumsum(lf_t):
    batch, n_heads, total = lf_t.shape
    ch = 256 if total % 256 == 0 else LANES
    spec = pl.BlockSpec((1, n_heads, total), lambda b: (b, 0, 0))
    return pl.pallas_call(
        functools.partial(_cumsum_kernel, ch=ch),
        grid=(batch,), in_specs=[spec], out_specs=spec,
        out_shape=jax.ShapeDtypeStruct(lf_t.shape, F32),
        compiler_params=_params(1),
    )(lf_t)


def _softmax_step(z, v, m_ref, l_ref, acc_ref):
    tk = z.shape[1]
    m_prev = m_ref[...]
    m_next = jnp.maximum(m_prev, jnp.max(z, axis=1, keepdims=True))
    p = jnp.exp2(z - pltpu.repeat(m_next, tk // LANES, 1))
    alpha = jnp.exp2(m_prev - m_next)
    l_ref[...] = alpha * l_ref[...] + jnp.sum(p, axis=1, keepdims=True)
    acc_ref[...] = alpha * acc_ref[...] + jnp.dot(p.astype(BF16), v, preferred_element_type=F32)
    m_ref[...] = m_next


def _fox_prompt_kernel(q_ref, k_ref, v_ref, c_ref, o_ref, m_ref, l_ref, acc_ref, *, tq, scale):
    qi = pl.program_id(2)
    group, hd = q_ref.shape[1], q_ref.shape[3]
    m_ref[...] = jnp.full(m_ref.shape, MASKED_LOGIT, F32)
    l_ref[...] = jnp.zeros(l_ref.shape, F32)
    acc_ref[...] = jnp.zeros(acc_ref.shape, F32)

    def logits(e, k0):
        k = k_ref[0, e, pl.ds(k0, tq), :]
        s = lax.dot_general(q_ref[0, e], k, NT_DIMS, preferred_element_type=F32)
        return s * (scale * LOG2E) - c_ref[0, e, :, pl.ds(k0, tq)] * LOG2E

    def body(j, carry):
        k0 = pl.multiple_of(j * tq, tq)
        for e in range(group):
            _softmax_step(logits(e, k0), v_ref[0, e, pl.ds(k0, tq), :],
                          m_ref.at[e], l_ref.at[e], acc_ref.at[e])
        return carry

    lax.fori_loop(0, qi, body, 0)

    k0 = pl.multiple_of(qi * tq, tq)
    row = lax.broadcasted_iota(jnp.int32, (tq, tq), 0)
    col = lax.broadcasted_iota(jnp.int32, (tq, tq), 1)
    for e in range(group):
        z = jnp.where(col <= row, logits(e, k0), MASKED_LOGIT)
        _softmax_step(z, v_ref[0, e, pl.ds(k0, tq), :], m_ref.at[e], l_ref.at[e], acc_ref.at[e])
        o_ref[:, e * hd:(e + 1) * hd] = (acc_ref[e] / l_ref[e]).astype(o_ref.dtype)


def _fox_prompt(q, k, v, c_row):
    batch, n_heads, seq, hd = q.shape
    tq = min(512, seq)
    nq = seq // tq
    group = FOX_HEADS_PER_STEP
    kv_spec = pl.BlockSpec((1, group, seq, hd), lambda b, h, i: (b, h, 0, 0))
    return pl.pallas_call(
        functools.partial(_fox_prompt_kernel, tq=tq, scale=hd ** -0.5),
        grid=(batch, n_heads // group, nq),
        in_specs=[pl.BlockSpec((1, group, tq, hd), lambda b, h, i: (b, h, i, 0)), kv_spec, kv_spec,
                  pl.BlockSpec((1, group, 1, seq), lambda b, h, i: (b, h, 0, 0))],
        out_specs=pl.BlockSpec((tq, group * hd), lambda b, h, i: (b * nq + i, h)),
        out_shape=jax.ShapeDtypeStruct((batch * seq, n_heads * hd), BF16),
        scratch_shapes=[pltpu.VMEM((group, tq, LANES), F32), pltpu.VMEM((group, tq, LANES), F32),
                        pltpu.VMEM((group, tq, hd), F32)],
        compiler_params=_params(3),
    )(q, k, v, c_row)


def _fox_sample_kernel(q_ref, kn_ref, vn_ref, kc_ref, vc_ref, c_ref, o_ref, *, past, scale):
    n_heads, new, hd = q_ref.shape[1:]
    row = lax.broadcasted_iota(jnp.int32, (new, new), 0)
    col = lax.broadcasted_iota(jnp.int32, (new, new), 1)
    for h in range(n_heads):
        q = q_ref[0, h]
        kc = kc_ref[0, pl.ds(h, past, stride=n_heads), :].astype(BF16)
        vc = vc_ref[0, pl.ds(h, past, stride=n_heads), :].astype(BF16)
        z_c = (lax.dot_general(q, kc, NT_DIMS, preferred_element_type=F32) * scale
               - c_ref[0, h:h + 1, 0:past])
        z_n = (lax.dot_general(q, kn_ref[0, h], NT_DIMS, preferred_element_type=F32) * scale
               - c_ref[0, h:h + 1, past:past + new])
        z_n = jnp.where(col <= row, z_n, MASKED_LOGIT)
        m = jnp.maximum(jnp.max(z_c, axis=1, keepdims=True), jnp.max(z_n, axis=1, keepdims=True))
        p_c = jnp.exp(z_c - m)
        p_n = jnp.exp(z_n - m)
        l = jnp.sum(p_c, axis=1, keepdims=True) + jnp.sum(p_n, axis=1, keepdims=True)
        acc = (jnp.dot(p_c.astype(BF16), vc, preferred_element_type=F32)
               + jnp.dot(p_n.astype(BF16), vn_ref[0, h], preferred_element_type=F32))
        o_ref[:, h * hd:(h + 1) * hd] = (acc / l).astype(o_ref.dtype)


def _fox_sample(q, k_new, v_new, cache_k, cache_v, c_row, *, layer):
    batch, n_heads, new, hd = q.shape
    past = cache_k.shape[1] // n_heads
    new_spec = pl.BlockSpec((1, n_heads, new, hd), lambda b: (b, 0, 0, 0))
    cache_spec = pl.BlockSpec((1, past * n_heads, hd), lambda b: (layer * batch + b, 0, 0))
    return pl.pallas_call(
        functools.partial(_fox_sample_kernel, past=past, scale=hd ** -0.5),
        grid=(batch,),
        in_specs=[new_spec, new_spec, new_spec, cache_spec, cache_spec,
                  pl.BlockSpec((1, n_heads, c_row.shape[2]), lambda b: (b, 0, 0))],
        out_specs=pl.BlockSpec((new, n_heads * hd), lambda b: (b, 0)),
        out_shape=jax.ShapeDtypeStruct((batch * new, n_heads * hd), BF16),
        compiler_params=_params(1),
    )(q, k_new, v_new, cache_k, cache_v, c_row)


def _retention_kernel(q_ref, k_ref, v_ref, g_ref, s0_ref, gn_ref, o_ref, s_out_ref,
                      s_ref, decay_ref, inter_ref, upd_ref, *, log_gammas):
    b = pl.program_id(0)
    c = pl.program_id(1)
    n_heads, chunk, hd = q_ref.shape[1:]

    @pl.when((b == 0) & (c == 0))
    def _():
        i = lax.broadcasted_iota(jnp.int32, (chunk, chunk), 0)
        jj = lax.broadcasted_iota(jnp.int32, (chunk, chunk), 1)
        diff = (i - jj).astype(F32)
        pos = lax.broadcasted_iota(jnp.int32, (chunk, hd), 0).astype(F32)
        for h, lg in enumerate(log_gammas):
            decay_ref[h] = jnp.where(diff >= 0.0, jnp.exp(jnp.maximum(diff, 0.0) * lg), 0.0)
            inter_ref[h] = jnp.exp((pos + 1.0) * lg)
            upd_ref[h] = jnp.exp((chunk - 1.0 - pos) * lg)

    @pl.when(c == 0)
    def _():
        s_ref[...] = s0_ref[0]

    for h, lg in enumerate(log_gammas):
        q = q_ref[0, h]
        k = k_ref[0, h]
        v = v_ref[0, h]
        s_prev = s_ref[h]
        scores = lax.dot_general(q, k, NT_DIMS, preferred_element_type=F32) * decay_ref[h]
        o = jnp.dot(scores.astype(BF16), v, preferred_element_type=F32)
        o = o + jnp.dot(q, s_prev.astype(BF16), preferred_element_type=F32) * inter_ref[h]
        kd = (k.astype(F32) * upd_ref[h]).astype(BF16)
        s_ref[h] = math.exp(chunk * lg) * s_prev + lax.dot_general(kd, v, TN_DIMS, preferred_element_type=F32)
        g = g_ref[0, h].astype(F32)
        y = _rms(o) * gn_ref[h:h + 1, :]
        o_ref[:, h * hd:(h + 1) * hd] = (y * (g * _sigmoid(g))).astype(o_ref.dtype)

    @pl.when(c == pl.num_programs(1) - 1)
    def _():
        s_out_ref[0] = s_ref[...]


def _retention(q, k, v, g, s0, gn, *, s0_block0):
    batch, n_heads, seq, hd = q.shape
    chunk = min(256, seq)
    nc = seq // chunk
    log_gammas = tuple(math.log(1.0 - 2.0 ** (-5.0 - h)) for h in range(n_heads))
    qkv_spec = pl.BlockSpec((1, n_heads, chunk, hd), lambda b, c: (b, 0, c, 0))
    return pl.pallas_call(
        functools.partial(_retention_kernel, log_gammas=log_gammas),
        grid=(batch, nc),
        in_specs=[qkv_spec] * 4 + [pl.BlockSpec((1, n_heads, hd, hd), lambda b, c: (s0_block0 + b, 0, 0, 0)),
                                   _resident((n_heads, hd), lambda b, c: (0, 0))],
        out_specs=[pl.BlockSpec((chunk, n_heads * hd), lambda b, c: (b * nc + c, 0)),
                   pl.BlockSpec((1, n_heads, hd, hd), lambda b, c: (b, 0, 0, 0))],
        out_shape=[jax.ShapeDtypeStruct((batch * seq, n_heads * hd), BF16),
                   jax.ShapeDtypeStruct((batch, n_heads, hd, hd), F32)],
        scratch_shapes=[pltpu.VMEM((n_heads, hd, hd), F32), pltpu.VMEM((n_heads, chunk, chunk), F32),
                        pltpu.VMEM((n_heads, chunk, hd), F32), pltpu.VMEM((n_heads, chunk, hd), F32)],
        compiler_params=_params(2),
    )(q, k, v, g, s0, gn)


def _outproj_kernel(of_ref, or_ref, x_ref, w_ref, ln2_ref, xm_ref, h2_ref, wb_ref):
    @pl.when(pl.program_id(0) == 0)
    def _():
        wb_ref[...] = w_ref[...].astype(BF16)

    half = of_ref.shape[1]
    y = jnp.dot(of_ref[...], wb_ref[0:half, :], preferred_element_type=F32)
    y = y + jnp.dot(or_ref[...], wb_ref[half:2 * half, :], preferred_element_type=F32)
    xm = x_ref[...] + y
    xm_ref[...] = xm
    h2_ref[...] = (_rms(xm) * ln2_ref[...]).astype(BF16)


def _outproj(o_fox, o_ret, x, w_out, ln2):
    tokens, d = x.shape
    half = o_fox.shape[1]
    tm = min(512, tokens)
    row = lambda i: (i, 0)
    return pl.pallas_call(
        _outproj_kernel,
        grid=(tokens // tm,),
        in_specs=[pl.BlockSpec((tm, half), row), pl.BlockSpec((tm, half), row),
                  pl.BlockSpec((tm, d), row),
                  _resident((2 * half, d), lambda i: (0, 0)), _resident((1, d), lambda i: (0, 0))],
        out_specs=[pl.BlockSpec((tm, d), row), pl.BlockSpec((tm, d), row)],
        out_shape=[jax.ShapeDtypeStruct((tokens, d), F32), jax.ShapeDtypeStruct((tokens, d), BF16)],
        scratch_shapes=[pltpu.VMEM((2 * half, d), BF16)],
        compiler_params=_params(1),
    )(o_fox, o_ret, x, w_out, ln2)


ROW_BLOCK = 8
DOWN_CHUNK = 512
SUB_ROWS = 512
FFN_TOKEN_TILE = 1024


def _convffn_kernel(*refs, n_seg, seg, tps, carried):
    if carried:
        (h2_ref, xm_ref, wa_ref, wg_ref, cwa_ref, cwg_ref, cba_ref, cbg_ref, wd_ref,
         out_ref, ta_ref, tg_ref, z_ref, ca_ref, cg_ref) = refs
        hists = (None, None)
        carries = (ca_ref, cg_ref)
    else:
        (h2_ref, xm_ref, wa_ref, wg_ref, cwa_ref, cwg_ref, cba_ref, cbg_ref, wd_ref, ha_ref, hg_ref,
         out_ref, ta_ref, tg_ref, z_ref) = refs
        hists = (ha_ref, hg_ref)
        carries = (None, None)
    i = pl.program_id(0)
    j = pl.program_id(1)
    tf = wa_ref.shape[1]
    d = out_ref.shape[1]
    tm = n_seg * seg
    col0 = pl.multiple_of(j * tf, tf)

    @pl.when(j == 0)
    def _():
        out_ref[...] = xm_ref[...]

    if carried:
        @pl.when(i % tps == 0)
        def _():
            for carry_ref in carries:
                carry_ref[0:2, pl.ds(col0, tf)] = jnp.zeros((2, tf), F32)

    sub = min(tm, SUB_ROWS)
    halves = ((wa_ref, cwa_ref, cba_ref, ta_ref, carries[0], hists[0]),
              (wg_ref, cwg_ref, cbg_ref, tg_ref, carries[1], hists[1]))
    ups = [[jnp.dot(h2_ref[r:r + sub, :], w_ref[...], preferred_element_type=F32)
            for w_ref, *_ in halves] for r in range(0, tm, sub)]

    row = lax.broadcasted_iota(jnp.int32, (ROW_BLOCK, tf), 0)
    taps = [[jnp.broadcast_to(cw_ref[t:t + 1, :], (ROW_BLOCK, tf)) for t in range(3)]
            for _, cw_ref, *_ in halves]
    bias = [jnp.broadcast_to(cb_ref[...], (ROW_BLOCK, tf)) for _, _, cb_ref, *_ in halves]
    prev = [None, None]

    def conv_block(x, r0):
        _, _, _, tail_ref, carry_ref, hist_ref = halves[x]
        s, off = divmod(r0, seg)
        if off == 0:
            if carried:
                h0 = carry_ref[0:1, pl.ds(col0, tf)]
                h1 = carry_ref[1:2, pl.ds(col0, tf)]
            else:
                h0 = hist_ref[s, 0:1, :]
                h1 = hist_ref[s, 1:2, :]
            prev[x] = (jnp.broadcast_to(h1, (ROW_BLOCK, tf)), jnp.where(row == 0, h0, h1))
        u = ups[r0 // sub][x]
        cur = u[r0 % sub:r0 % sub + ROW_BLOCK]
        rot1 = pltpu.roll(cur, 1, 0)
        rot2 = pltpu.roll(cur, 2, 0)
        um1 = jnp.where(row == 0, prev[x][0], rot1)
        um2 = jnp.where(row <= 1, prev[x][1], rot2)
        prev[x] = (rot1, rot2)
        if off == seg - ROW_BLOCK:
            tail_ref[s] = cur[ROW_BLOCK - 2:ROW_BLOCK, :]
        if carried and r0 == tm - ROW_BLOCK:
            carry_ref[0:2, pl.ds(col0, tf)] = cur[ROW_BLOCK - 2:ROW_BLOCK, :]
        return bias[x] + (um2 * taps[x][0] + um1 * taps[x][1] + cur * taps[x][2])

    pack = 2 * ROW_BLOCK
    for r_sub in range(0, tm, sub):
        for r0 in range(r_sub, r_sub + sub, pack):
            a = jnp.concatenate([conv_block(0, r0), conv_block(0, r0 + ROW_BLOCK)], axis=0)
            g = jnp.concatenate([conv_block(1, r0), conv_block(1, r0 + ROW_BLOCK)], axis=0)
            z_ref[r0:r0 + pack, :] = ((g * _sigmoid(g)) * a).astype(BF16)
        z = z_ref[r_sub:r_sub + sub, :]
        for c0 in range(0, d, DOWN_CHUNK):
            out_ref[r_sub:r_sub + sub, c0:c0 + DOWN_CHUNK] += jnp.dot(
                z, wd_ref[:, c0:c0 + DOWN_CHUNK], preferred_element_type=F32)


def _convffn(h2, xm, w_up, conv_w, conv_b, w_down, hist, *, batch, seq):
    tokens, d = xm.shape
    f = w_down.shape[0]
    tf = min(512, f)
    nf = f // tf
    tm = min(FFN_TOKEN_TILE, tokens)
    seg = min(seq, tm)
    n_seg = tm // seg
    tps = seq // seg
    carried = hist is None
    assert not carried or n_seg == 1, "a carried conv state needs one sequence per token tile"
    row = lambda i, j: (i, 0)
    a_col = lambda i, j: (0, j)
    g_col = lambda i, j: (0, nf + j)
    in_specs = [pl.BlockSpec((tm, d), row), _resident((tm, d), row),
                pl.BlockSpec((d, tf), a_col), pl.BlockSpec((d, tf), g_col),
                pl.BlockSpec((3, tf), a_col), pl.BlockSpec((3, tf), g_col),
                pl.BlockSpec((1, tf), a_col), pl.BlockSpec((1, tf), g_col),
                pl.BlockSpec((tf, d), lambda i, j: (j, 0))]
    args = [h2, xm, w_up, w_up, conv_w, conv_w, conv_b, conv_b, w_down]
    scratch = [pltpu.VMEM((tm, tf), BF16)]
    if carried:
        scratch += [pltpu.VMEM((8, f), F32), pltpu.VMEM((8, f), F32)]
    else:
        in_specs += [pl.BlockSpec((n_seg, 2, tf), lambda i, j: (i, 0, j)),
                     pl.BlockSpec((n_seg, 2, tf), lambda i, j: (i, 0, nf + j))]
        args += [hist, hist]
    tail_spec = pl.BlockSpec((n_seg, 2, tf), lambda i, j: (i, 0, j))
    tail_shape = jax.ShapeDtypeStruct((batch * tps, 2, f), F32)
    out, tails_a, tails_g = pl.pallas_call(
        functools.partial(_convffn_kernel, n_seg=n_seg, seg=seg, tps=tps, carried=carried),
        grid=(tokens // tm, nf),
        in_specs=in_specs,
        out_specs=[pl.BlockSpec((tm, d), row), tail_spec, tail_spec],
        out_shape=[jax.ShapeDtypeStruct((tokens, d), F32), tail_shape, tail_shape],
        scratch_shapes=scratch,
        compiler_params=_params(2),
    )(*args)
    last = lambda t: t.reshape(batch, tps, 2, f)[:, tps - 1]
    return out, jnp.concatenate([last(tails_a), last(tails_g)], axis=-1)


def _rope_tables(pos, rows):
    half = LANES // 2
    inv = jnp.power(ROPE_BASE, -jnp.arange(half, dtype=F32) / half)
    ang = pos.astype(F32)[:, None] * inv[None, :]
    cos = jnp.cos(ang)
    sin = jnp.sin(ang)
    cos_t = jnp.concatenate([cos, cos], axis=1)
    sin_t = jnp.concatenate([-sin, sin], axis=1)
    reps = rows // pos.shape[0]
    return jnp.tile(cos_t, (reps, 1)), jnp.tile(sin_t, (reps, 1))


def kernel(x_prompt, x_sample, cache_fox_k, cache_fox_v, cache_fox_logf, state_ret, state_conv,
           ln1, w_in, b_f, fox_qn, fox_kn, ret_gn, w_out, ln2, w_up, conv_w, conv_b, w_down):
    depth = ln1.shape[0]
    bp, tp, d = x_prompt.shape
    bs, ts, _ = x_sample.shape
    past = cache_fox_k.shape[2]
    n_heads, hd = cache_fox_k.shape[3], cache_fox_k.shape[4]
    d_head = n_heads * hd
    assert hd == LANES and n_heads % HEADS_PER_STEP == 0 and ret_gn.shape[1:] == (n_heads, hd)

    cos_p, sin_p = _rope_tables(jnp.arange(tp), max(tp, min(512, bp * tp)))
    cos_s, sin_s = _rope_tables(past + jnp.arange(ts), max(ts, min(512, bs * ts)))
    cache_k = cache_fox_k.reshape(depth * bs, past * n_heads, hd)
    cache_v = cache_fox_v.reshape(depth * bs, past * n_heads, hd)
    state_r = state_ret.reshape(depth * bs, n_heads, hd, hd)
    zero_state = jnp.zeros((bp, n_heads, hd, hd), F32)

    xp = x_prompt.reshape(bp * tp, d)
    xs = x_sample.reshape(bs * ts, d)
    st_p, st_s = [], []
    for l in range(depth):
        r0 = N_FOX_KINDS * d_head + n_heads
        w_fox = w_in[l][:, :N_FOX_KINDS * d_head].astype(BF16)
        w_ret = w_in[l][:, r0:].astype(BF16)
        wf = jnp.pad(w_in[l][:, N_FOX_KINDS * d_head:r0], ((0, 0), (0, hd - n_heads))).astype(BF16)
        bfp = jnp.pad(b_f[l], (0, hd - n_heads)).reshape(1, hd)
        qn = fox_qn[l].reshape(1, hd)
        kn = fox_kn[l].reshape(1, hd)
        wu = w_up[l].astype(BF16)
        wd = w_down[l].astype(BF16)
        cw = conv_w[l]
        cb = conv_b[l].reshape(1, -1)
        ln1_l = ln1[l].reshape(1, d)
        ln2_l = ln2[l].reshape(1, d)

        def group(x, cos_t, sin_t, batch, seq):
            return _inproj(x, ln1_l, w_fox, w_ret, wf, bfp, qn, kn, cos_t, sin_t,
                           batch=batch, seq=seq, n_heads=n_heads)

        fq, fkb, fvb, rq, rk, rv, rg, fk, fv, lf = group(xp, cos_p, sin_p, bp, tp)
        c_row = _cumsum(lf.reshape(bp, tp, n_heads).transpose(0, 2, 1))
        o_fox = _fox_prompt(fq, fkb, fvb, c_row.reshape(bp, n_heads, 1, tp))
        o_ret, s_ret_p = _retention(rq, rk, rv, rg, zero_state, ret_gn[l], s0_block0=0)
        xm, h2 = _outproj(o_fox, o_ret, xp, w_out[l], ln2_l)
        xp, conv_p = _convffn(h2, xm, wu, cw, cb, wd, None, batch=bp, seq=tp)
        st_p.append((fk.reshape(bp, tp, n_heads, hd), fv.reshape(bp, tp, n_heads, hd),
                     lf.reshape(bp, tp, n_heads), s_ret_p, conv_p))

        fq, fkb, fvb, rq, rk, rv, rg, fk, fv, lf = group(xs, cos_s, sin_s, bs, ts)
        lf_all = jnp.concatenate([cache_fox_logf[l], lf.reshape(bs, ts, n_heads)], axis=1)
        pad = (-lf_all.shape[1]) % LANES
        lf_all = jnp.pad(lf_all, ((0, 0), (0, pad), (0, 0)))
        c_row = _cumsum(lf_all.transpose(0, 2, 1))
        o_fox = _fox_sample(fq, fkb, fvb, cache_k, cache_v, c_row, layer=l)
        o_ret, s_ret_s = _retention(rq, rk, rv, rg, state_r, ret_gn[l], s0_block0=l * bs)
        xm, h2 = _outproj(o_fox, o_ret, xs, w_out[l], ln2_l)
        xs, conv_s = _convffn(h2, xm, wu, cw, cb, wd, state_conv[l], batch=bs, seq=ts)
        st_s.append((fk.reshape(bs, ts, n_heads, hd), fv.reshape(bs, ts, n_heads, hd),
                     lf.reshape(bs, ts, n_heads), s_ret_s, conv_s))

    stack = lambda st, k: jnp.stack([s[k] for s in st])
    return (xp.reshape(bp, tp, d), xs.reshape(bs, ts, d),
            stack(st_p, 0), stack(st_p, 1), stack(st_p, 2), stack(st_p, 3), stack(st_p, 4),
            stack(st_s, 0), stack(st_s, 1), stack(st_s, 2), stack(st_s, 3), stack(st_s, 4))
```

```python
import functools
import math

import jax
import jax.numpy as jnp
from jax import lax
from jax.experimental import pallas as pl
from jax.experimental.pallas import tpu as pltpu

EPS = 1e-6
ROPE_BASE = 10000.0
MASKED_LOGIT = -1e30
LANES = 128
VMEM_LIMIT_BYTES = 56 * 1024 * 1024
LOG2E = math.log2(math.e)

F32 = jnp.float32
BF16 = jnp.bfloat16

N_FOX_KINDS = 3
N_RET_KINDS = 4
HEADS_PER_STEP = 2
FOX_HEADS_PER_STEP = 2

NT_DIMS = (((1,), (1,)), ((), ()))
TN_DIMS = (((0,), (0,)), ((), ()))


def _params(n_axes):
    return pltpu.CompilerParams(dimension_semantics=("arbitrary",) * n_axes,
                                vmem_limit_bytes=VMEM_LIMIT_BYTES)


def _resident(block_shape, index_map):
    return pl.BlockSpec(block_shape, index_map, pipeline_mode=pl.Buffered(1))


def _rms(x):
    return x * lax.rsqrt(jnp.mean(x * x, axis=-1, keepdims=True) + EPS)


def _sigmoid(x):
    return 1.0 / (1.0 + jnp.exp(-x))


def _tile_cols_kernel(*refs):
    *src_refs, out_ref = refs
    c0 = 0
    for src in src_refs:
        width = src.shape[1]
        out_ref[0, :, c0:c0 + width] = src[...].astype(BF16)
        c0 += width


def _tile_cols(sources, width, n_tiles):
    d = sources[0][0].shape[0]
    specs = [pl.BlockSpec((d, width), functools.partial(lambda first, t: (0, first + t), first))
             for _, first in sources]
    return pl.pallas_call(
        _tile_cols_kernel,
        grid=(n_tiles,),
        in_specs=specs,
        out_specs=pl.BlockSpec((1, d, width * len(sources)), lambda t: (t, 0, 0)),
        out_shape=jax.ShapeDtypeStruct((n_tiles, d, width * len(sources)), BF16),
        compiler_params=_params(1),
    )(*[w for w, _ in sources])


def _inproj_kernel(x_ref, ln1_ref, w_ref, wf_ref, bf_ref, qn_ref, kn_ref, cos_ref, sin_ref,
                   fq_ref, fkb_ref, fvb_ref, rq_ref, rk_ref, rv_ref, rg_ref, fk_ref, fv_ref, logf_ref,
                   h_ref, *, nb, seg, n_fox):
    j = pl.program_id(1)

    @pl.when(j == 0)
    def _():
        h = (_rms(x_ref[...]) * ln1_ref[...]).astype(BF16)
        h_ref[...] = h
        z = jnp.dot(h, wf_ref[...], preferred_element_type=F32) + bf_ref[...]
        logf = jnp.minimum(z, 0.0) - jnp.log1p(jnp.exp(-jnp.abs(z)))
        logf_ref[...] = logf[:, :n_fox]

    cos = cos_ref[...]
    sin = sin_ref[...]
    hd = LANES

    def project(kind):
        c0 = kind * HEADS_PER_STEP * hd
        p = jnp.dot(h_ref[...], w_ref[0, :, c0:c0 + HEADS_PER_STEP * hd],
                    preferred_element_type=F32)
        return [p[:, e * hd:(e + 1) * hd] for e in range(HEADS_PER_STEP)]

    def head_major(a):
        return a.reshape(nb, seg, hd).astype(BF16)

    def rope(a):
        return a * cos + pltpu.roll(a, hd // 2, 1) * sin

    for e, a in enumerate(project(0)):
        fq_ref[:, e] = head_major(_rms(a) * qn_ref[...])
    for e, a in enumerate(project(1)):
        fk = _rms(a) * kn_ref[...]
        fk_ref[:, e * hd:(e + 1) * hd] = fk
        fkb_ref[:, e] = head_major(fk)
    for e, a in enumerate(project(2)):
        fv_ref[:, e * hd:(e + 1) * hd] = a
        fvb_ref[:, e] = head_major(a)
    for e, a in enumerate(project(3)):
        rq_ref[:, e] = head_major(rope(a))
    for e, a in enumerate(project(4)):
        rk_ref[:, e] = head_major(rope(a) * (hd ** -0.5))
    for e, a in enumerate(project(5)):
        rv_ref[:, e] = head_major(a)
    for e, a in enumerate(project(6)):
        rg_ref[:, e] = head_major(a)


def _inproj(x, ln1, w_tiles, wf, bfp, qn, kn, cos_t, sin_t, *, batch, seq, n_heads):
    tokens, d = x.shape
    hd = LANES
    tm = min(512, tokens)
    seg = min(seq, tm)
    nb = tm // seg
    tps = seq // seg
    n_groups = n_heads // HEADS_PER_STEP
    gcols = HEADS_PER_STEP * hd
    grid = (tokens // tm, n_groups)

    hm_shape = jax.ShapeDtypeStruct((batch, n_heads, seq, hd), BF16)
    hm_spec = pl.BlockSpec((nb, HEADS_PER_STEP, seg, hd), lambda i, j: (i // tps, j, i % tps, 0))
    tok_shape = jax.ShapeDtypeStruct((tokens, n_heads * hd), F32)
    tok_spec = pl.BlockSpec((tm, gcols), lambda i, j: (i, j))

    return pl.pallas_call(
        functools.partial(_inproj_kernel, nb=nb, seg=seg, n_fox=n_heads),
        grid=grid,
        in_specs=[pl.BlockSpec((tm, d), lambda i, j: (i, 0)),
                  _resident((1, d), lambda i, j: (0, 0))]
                 + [pl.BlockSpec((1, d, w_tiles.shape[2]), lambda i, j: (j, 0, 0)),
                    _resident((d, hd), lambda i, j: (0, 0)),
                    _resident((1, hd), lambda i, j: (0, 0)),
                    _resident((1, hd), lambda i, j: (0, 0)),
                    _resident((1, hd), lambda i, j: (0, 0)),
                    pl.BlockSpec((tm, hd), lambda i, j: (i % tps, 0)),
                    pl.BlockSpec((tm, hd), lambda i, j: (i % tps, 0))],
        out_specs=[hm_spec] * 7 + [tok_spec, tok_spec,
                                   pl.BlockSpec((tm, n_heads), lambda i, j: (i, 0))],
        out_shape=[hm_shape] * 7 + [tok_shape, tok_shape,
                                    jax.ShapeDtypeStruct((tokens, n_heads), F32)],
        scratch_shapes=[pltpu.VMEM((tm, d), BF16)],
        compiler_params=_params(2),
    )(x, ln1, w_tiles, wf, bfp, qn, kn, cos_t, sin_t)


def _cumsum_kernel(lf_ref, c_ref, *, ch):
    total = lf_ref.shape[2]
    r = lax.broadcasted_iota(jnp.int32, (ch, ch), 0)
    c = lax.broadcasted_iota(jnp.int32, (ch, ch), 1)
    upper = (r <= c).astype(F32)
    carry = jnp.zeros((lf_ref.shape[1], 1), F32)
    for k in range(total // ch):
        x = lf_ref[0, :, k * ch:(k + 1) * ch]
        cs = jnp.dot(x, upper, precision=lax.Precision.HIGHEST, preferred_element_type=F32) + carry
        c_ref[0, :, k * ch:(k + 1) * ch] = cs
        carry = cs[:, ch - 1:ch]


def _cumsum(lf_t):
    batch, n_heads, total = lf_t.shape
    ch = 256 if total % 256 == 0 else LANES
    spec = pl.BlockSpec((1, n_heads, total), lambda b: (b, 0, 0))
    return pl.pallas_call(
        functools.partial(_cumsum_kernel, ch=ch),
        grid=(batch,), in_specs=[spec], out_specs=spec,
        out_shape=jax.ShapeDtypeStruct(lf_t.shape, F32),
        compiler_params=_params(1),
    )(lf_t)


def _softmax_step(z, v, m_ref, l_ref, acc_ref):
    tk = z.shape[1]
    m_prev = m_ref[...]
    m_next = jnp.maximum(m_prev, jnp.max(z, axis=1, keepdims=True))
    p = jnp.exp2(z - pltpu.repeat(m_next, tk // LANES, 1))
    alpha = jnp.exp2(m_prev - m_next)
    l_ref[...] = alpha * l_ref[...] + jnp.sum(p, axis=1, keepdims=True)
    acc_ref[...] = alpha * acc_ref[...] + jnp.dot(p.astype(BF16), v, preferred_element_type=F32)
    m_ref[...] = m_next


def _fox_prompt_kernel(q_ref, k_ref, v_ref, c_ref, o_ref, m_ref, l_ref, acc_ref, *, tq, scale):
    qi = pl.program_id(2)
    group, hd = q_ref.shape[1], q_ref.shape[3]
    m_ref[...] = jnp.full(m_ref.shape, MASKED_LOGIT, F32)
    l_ref[...] = jnp.zeros(l_ref.shape, F32)
    acc_ref[...] = jnp.zeros(acc_ref.shape, F32)

    def logits(e, k0):
        k = k_ref[0, e, pl.ds(k0, tq), :]
        s = lax.dot_general(q_ref[0, e], k, NT_DIMS, preferred_element_type=F32)
        return s * (scale * LOG2E) - c_ref[0, e, :, pl.ds(k0, tq)] * LOG2E

    def body(j, carry):
        k0 = pl.multiple_of(j * tq, tq)
        for e in range(group):
            _softmax_step(logits(e, k0), v_ref[0, e, pl.ds(k0, tq), :],
                          m_ref.at[e], l_ref.at[e], acc_ref.at[e])
        return carry

    lax.fori_loop(0, qi, body, 0)

    k0 = pl.multiple_of(qi * tq, tq)
    row = lax.broadcasted_iota(jnp.int32, (tq, tq), 0)
    col = lax.broadcasted_iota(jnp.int32, (tq, tq), 1)
    for e in range(group):
        z = jnp.where(col <= row, logits(e, k0), MASKED_LOGIT)
        _softmax_step(z, v_ref[0, e, pl.ds(k0, tq), :], m_ref.at[e], l_ref.at[e], acc_ref.at[e])
        o_ref[:, e * hd:(e + 1) * hd] = (acc_ref[e] / l_ref[e]).astype(o_ref.dtype)


def _fox_prompt(q, k, v, c_row):
    batch, n_heads, seq, hd = q.shape
    tq = min(512, seq)
    nq = seq // tq
    group = FOX_HEADS_PER_STEP
    kv_spec = pl.BlockSpec((1, group, seq, hd), lambda b, h, i: (b, h, 0, 0))
    return pl.pallas_call(
        functools.partial(_fox_prompt_kernel, tq=tq, scale=hd ** -0.5),
        grid=(batch, n_heads // group, nq),
        in_specs=[pl.BlockSpec((1, group, tq, hd), lambda b, h, i: (b, h, i, 0)), kv_spec, kv_spec,
                  pl.BlockSpec((1, group, 1, seq), lambda b, h, i: (b, h, 0, 0))],
        out_specs=pl.BlockSpec((tq, group * hd), lambda b, h, i: (b * nq + i, h)),
        out_shape=jax.ShapeDtypeStruct((batch * seq, n_heads * hd), BF16),
        scratch_shapes=[pltpu.VMEM((group, tq, LANES), F32), pltpu.VMEM((group, tq, LANES), F32),
                        pltpu.VMEM((group, tq, hd), F32)],
        compiler_params=_params(3),
    )(q, k, v, c_row)


def _fox_sample_kernel(q_ref, kn_ref, vn_ref, kc_ref, vc_ref, c_ref, o_ref, *, past, scale):
    n_heads, new, hd = q_ref.shape[1:]
    row = lax.broadcasted_iota(jnp.int32, (new, new), 0)
    col = lax.broadcasted_iota(jnp.int32, (new, new), 1)
    for h in range(n_heads):
        q = q_ref[0, h]
        kc = kc_ref[0, pl.ds(h, past, stride=n_heads), :].astype(BF16)
        vc = vc_ref[0, pl.ds(h, past, stride=n_heads), :].astype(BF16)
        z_c = (lax.dot_general(q, kc, NT_DIMS, preferred_element_type=F32) * scale
               - c_ref[0, h:h + 1, 0:past])
        z_n = (lax.dot_general(q, kn_ref[0, h], NT_DIMS, preferred_element_type=F32) * scale
               - c_ref[0, h:h + 1, past:past + new])
        z_n = jnp.where(col <= row, z_n, MASKED_LOGIT)
        m = jnp.maximum(jnp.max(z_c, axis=1, keepdims=True), jnp.max(z_n, axis=1, keepdims=True))
        p_c = jnp.exp(z_c - m)
        p_n = jnp.exp(z_n - m)
        l = jnp.sum(p_c, axis=1, keepdims=True) + jnp.sum(p_n, axis=1, keepdims=True)
        acc = (jnp.dot(p_c.astype(BF16), vc, preferred_element_type=F32)
               + jnp.dot(p_n.astype(BF16), vn_ref[0, h], preferred_element_type=F32))
        o_ref[:, h * hd:(h + 1) * hd] = (acc / l).astype(o_ref.dtype)


def _fox_sample(q, k_new, v_new, cache_k, cache_v, c_row, *, layer):
    batch, n_heads, new, hd = q.shape
    past = cache_k.shape[1] // n_heads
    new_spec = pl.BlockSpec((1, n_heads, new, hd), lambda b: (b, 0, 0, 0))
    cache_spec = pl.BlockSpec((1, past * n_heads, hd), lambda b: (layer * batch + b, 0, 0))
    return pl.pallas_call(
        functools.partial(_fox_sample_kernel, past=past, scale=hd ** -0.5),
        grid=(batch,),
        in_specs=[new_spec, new_spec, new_spec, cache_spec, cache_spec,
                  pl.BlockSpec((1, n_heads, c_row.shape[2]), lambda b: (b, 0, 0))],
        out_specs=pl.BlockSpec((new, n_heads * hd), lambda b: (b, 0)),
        out_shape=jax.ShapeDtypeStruct((batch * new, n_heads * hd), BF16),
        compiler_params=_params(1),
    )(q, k_new, v_new, cache_k, cache_v, c_row)


def _retention_kernel(q_ref, k_ref, v_ref, g_ref, s0_ref, gn_ref, o_ref, s_out_ref,
                      s_ref, decay_ref, inter_ref, upd_ref, *, log_gammas):
    b = pl.program_id(0)
    c = pl.program_id(1)
    n_heads, chunk, hd = q_ref.shape[1:]

    @pl.when((b == 0) & (c == 0))
    def _():
        i = lax.broadcasted_iota(jnp.int32, (chunk, chunk), 0)
        jj = lax.broadcasted_iota(jnp.int32, (chunk, chunk), 1)
        diff = (i - jj).astype(F32)
        pos = lax.broadcasted_iota(jnp.int32, (chunk, hd), 0).astype(F32)
        for h, lg in enumerate(log_gammas):
            decay_ref[h] = jnp.where(diff >= 0.0, jnp.exp(jnp.maximum(diff, 0.0) * lg), 0.0)
            inter_ref[h] = jnp.exp((pos + 1.0) * lg)
            upd_ref[h] = jnp.exp((chunk - 1.0 - pos) * lg)

    @pl.when(c == 0)
    def _():
        s_ref[...] = s0_ref[0]

    for h, lg in enumerate(log_gammas):
        q = q_ref[0, h]
        k = k_ref[0, h]
        v = v_ref[0, h]
        s_prev = s_ref[h]
        scores = lax.dot_general(q, k, NT_DIMS, preferred_element_type=F32) * decay_ref[h]
        o = jnp.dot(scores.astype(BF16), v, preferred_element_type=F32)
        o = o + jnp.dot(q, s_prev.astype(BF16), preferred_element_type=F32) * inter_ref[h]
        kd = (k.astype(F32) * upd_ref[h]).astype(BF16)
        s_ref[h] = math.exp(chunk * lg) * s_prev + lax.dot_general(kd, v, TN_DIMS, preferred_element_type=F32)
        g = g_ref[0, h].astype(F32)
        y = _rms(o) * gn_ref[h:h + 1, :]
        o_ref[:, h * hd:(h + 1) * hd] = (y * (g * _sigmoid(g))).astype(o_ref.dtype)

    @pl.when(c == pl.num_programs(1) - 1)
    def _():
        s_out_ref[0] = s_ref[...]


def _retention(q, k, v, g, s0, gn, *, s0_block0):
    batch, n_heads, seq, hd = q.shape
    chunk = min(256, seq)
    nc = seq // chunk
    log_gammas = tuple(math.log(1.0 - 2.0 ** (-5.0 - h)) for h in range(n_heads))
    qkv_spec = pl.BlockSpec((1, n_heads, chunk, hd), lambda b, c: (b, 0, c, 0))
    return pl.pallas_call(
        functools.partial(_retention_kernel, log_gammas=log_gammas),
        grid=(batch, nc),
        in_specs=[qkv_spec] * 4 + [pl.BlockSpec((1, n_heads, hd, hd), lambda b, c: (s0_block0 + b, 0, 0, 0)),
                                   _resident((n_heads, hd), lambda b, c: (0, 0))],
        out_specs=[pl.BlockSpec((chunk, n_heads * hd), lambda b, c: (b * nc + c, 0)),
                   pl.BlockSpec((1, n_heads, hd, hd), lambda b, c: (b, 0, 0, 0))],
        out_shape=[jax.ShapeDtypeStruct((batch * seq, n_heads * hd), BF16),
                   jax.ShapeDtypeStruct((batch, n_heads, hd, hd), F32)],
        scratch_shapes=[pltpu.VMEM((n_heads, hd, hd), F32), pltpu.VMEM((n_heads, chunk, chunk), F32),
                        pltpu.VMEM((n_heads, chunk, hd), F32), pltpu.VMEM((n_heads, chunk, hd), F32)],
        compiler_params=_params(2),
    )(q, k, v, g, s0, gn)


def _outproj_kernel(of_ref, or_ref, x_ref, w_ref, ln2_ref, xm_ref, h2_ref, wb_ref):
    @pl.when(pl.program_id(0) == 0)
    def _():
        wb_ref[...] = w_ref[...].astype(BF16)

    half = of_ref.shape[1]
    y = jnp.dot(of_ref[...], wb_ref[0:half, :], preferred_element_type=F32)
    y = y + jnp.dot(or_ref[...], wb_ref[half:2 * half, :], preferred_element_type=F32)
    xm = x_ref[...] + y
    xm_ref[...] = xm
    h2_ref[...] = (_rms(xm) * ln2_ref[...]).astype(BF16)


def _outproj(o_fox, o_ret, x, w_out, ln2):
    tokens, d = x.shape
    half = o_fox.shape[1]
    tm = min(512, tokens)
    row = lambda i: (i, 0)
    return pl.pallas_call(
        _outproj_kernel,
        grid=(tokens // tm,),
        in_specs=[pl.BlockSpec((tm, half), row), pl.BlockSpec((tm, half), row),
                  pl.BlockSpec((tm, d), row),
                  _resident((2 * half, d), lambda i: (0, 0)), _resident((1, d), lambda i: (0, 0))],
        out_specs=[pl.BlockSpec((tm, d), row), pl.BlockSpec((tm, d), row)],
        out_shape=[jax.ShapeDtypeStruct((tokens, d), F32), jax.ShapeDtypeStruct((tokens, d), BF16)],
        scratch_shapes=[pltpu.VMEM((2 * half, d), BF16)],
        compiler_params=_params(1),
    )(o_fox, o_ret, x, w_out, ln2)


ROW_BLOCK = 8
DOWN_CHUNK = 512
SUB_ROWS = 512
FFN_TOKEN_TILE = 1024
FFN_WIDTH_TILE = 512


def _convffn_kernel(*refs, n_seg, seg, tps, carried):
    if carried:
        (h2_ref, xm_ref, wu_ref, cwa_ref, cwg_ref, cba_ref, cbg_ref, wd_ref,
         out_ref, ta_ref, tg_ref, z_ref, ca_ref, cg_ref) = refs
        hists = (None, None)
        carries = (ca_ref, cg_ref)
    else:
        (h2_ref, xm_ref, wu_ref, cwa_ref, cwg_ref, cba_ref, cbg_ref, wd_ref, ha_ref, hg_ref,
         out_ref, ta_ref, tg_ref, z_ref) = refs
        hists = (ha_ref, hg_ref)
        carries = (None, None)
    i = pl.program_id(0)
    j = pl.program_id(1)
    tf = wu_ref.shape[2] // 2
    d = out_ref.shape[1]
    tm = n_seg * seg
    col0 = pl.multiple_of(j * tf, tf)

    @pl.when(j == 0)
    def _():
        out_ref[...] = xm_ref[...]

    if carried:
        @pl.when(i % tps == 0)
        def _():
            for carry_ref in carries:
                carry_ref[0:2, pl.ds(col0, tf)] = jnp.zeros((2, tf), F32)

    sub = min(tm, SUB_ROWS)
    halves = ((0, cwa_ref, cba_ref, ta_ref, carries[0], hists[0]),
              (tf, cwg_ref, cbg_ref, tg_ref, carries[1], hists[1]))
    ups = [[jnp.dot(h2_ref[r:r + sub, :], wu_ref[0, :, c:c + tf], preferred_element_type=F32)
            for c, *_ in halves] for r in range(0, tm, sub)]

    row = lax.broadcasted_iota(jnp.int32, (ROW_BLOCK, tf), 0)
    taps = [[jnp.broadcast_to(cw_ref[t:t + 1, :], (ROW_BLOCK, tf)) for t in range(3)]
            for _, cw_ref, *_ in halves]
    bias = [jnp.broadcast_to(cb_ref[...], (ROW_BLOCK, tf)) for _, _, cb_ref, *_ in halves]
    prev = [None, None]

    def conv_block(x, r0):
        _, _, _, tail_ref, carry_ref, hist_ref = halves[x]
        s, off = divmod(r0, seg)
        if off == 0:
            if carried:
                h0 = carry_ref[0:1, pl.ds(col0, tf)]
                h1 = carry_ref[1:2, pl.ds(col0, tf)]
            else:
                h0 = hist_ref[s, 0:1, :]
                h1 = hist_ref[s, 1:2, :]
            prev[x] = (jnp.broadcast_to(h1, (ROW_BLOCK, tf)), jnp.where(row == 0, h0, h1))
        u = ups[r0 // sub][x]
        cur = u[r0 % sub:r0 % sub + ROW_BLOCK]
        rot1 = pltpu.roll(cur, 1, 0)
        rot2 = pltpu.roll(cur, 2, 0)
        um1 = jnp.where(row == 0, prev[x][0], rot1)
        um2 = jnp.where(row <= 1, prev[x][1], rot2)
        prev[x] = (rot1, rot2)
        if off == seg - ROW_BLOCK:
            tail_ref[s] = cur[ROW_BLOCK - 2:ROW_BLOCK, :]
        if carried and r0 == tm - ROW_BLOCK:
            carry_ref[0:2, pl.ds(col0, tf)] = cur[ROW_BLOCK - 2:ROW_BLOCK, :]
        return bias[x] + (um2 * taps[x][0] + um1 * taps[x][1] + cur * taps[x][2])

    pack = 2 * ROW_BLOCK
    for r_sub in range(0, tm, sub):
        for r0 in range(r_sub, r_sub + sub, pack):
            a = jnp.concatenate([conv_block(0, r0), conv_block(0, r0 + ROW_BLOCK)], axis=0)
            g = jnp.concatenate([conv_block(1, r0), conv_block(1, r0 + ROW_BLOCK)], axis=0)
            z_ref[r0:r0 + pack, :] = ((g * _sigmoid(g)) * a).astype(BF16)
        z = z_ref[r_sub:r_sub + sub, :]
        for c0 in range(0, d, DOWN_CHUNK):
            out_ref[r_sub:r_sub + sub, c0:c0 + DOWN_CHUNK] += jnp.dot(
                z, wd_ref[:, c0:c0 + DOWN_CHUNK], preferred_element_type=F32)


def _convffn(h2, xm, w_up, conv_w, conv_b, w_down, hist, *, batch, seq):
    tokens, d = xm.shape
    f = w_down.shape[0]
    nf = w_up.shape[0]
    tf = f // nf
    tm = min(FFN_TOKEN_TILE, tokens)
    seg = min(seq, tm)
    n_seg = tm // seg
    tps = seq // seg
    carried = hist is None
    assert not carried or n_seg == 1, "a carried conv state needs one sequence per token tile"
    row = lambda i, j: (i, 0)
    a_col = lambda i, j: (0, j)
    g_col = lambda i, j: (0, nf + j)
    in_specs = [pl.BlockSpec((tm, d), row), _resident((tm, d), row),
                pl.BlockSpec((1, d, 2 * tf), lambda i, j: (j, 0, 0)),
                pl.BlockSpec((3, tf), a_col), pl.BlockSpec((3, tf), g_col),
                pl.BlockSpec((1, tf), a_col), pl.BlockSpec((1, tf), g_col),
                pl.BlockSpec((tf, d), lambda i, j: (j, 0))]
    args = [h2, xm, w_up, conv_w, conv_w, conv_b, conv_b, w_down]
    scratch = [pltpu.VMEM((tm, tf), BF16)]
    if carried:
        scratch += [pltpu.VMEM((8, f), F32), pltpu.VMEM((8, f), F32)]
    else:
        in_specs += [pl.BlockSpec((n_seg, 2, tf), lambda i, j: (i, 0, j)),
                     pl.BlockSpec((n_seg, 2, tf), lambda i, j: (i, 0, nf + j))]
        args += [hist, hist]
    tail_spec = pl.BlockSpec((n_seg, 2, tf), lambda i, j: (i, 0, j))
    tail_shape = jax.ShapeDtypeStruct((batch * tps, 2, f), F32)
    out, tails_a, tails_g = pl.pallas_call(
        functools.partial(_convffn_kernel, n_seg=n_seg, seg=seg, tps=tps, carried=carried),
        grid=(tokens // tm, nf),
        in_specs=in_specs,
        out_specs=[pl.BlockSpec((tm, d), row), tail_spec, tail_spec],
        out_shape=[jax.ShapeDtypeStruct((tokens, d), F32), tail_shape, tail_shape],
        scratch_shapes=scratch,
        compiler_params=_params(2),
    )(*args)
    last = lambda t: t.reshape(batch, tps, 2, f)[:, tps - 1]
    return out, jnp.concatenate([last(tails_a), last(tails_g)], axis=-1)


def _rope_tables(pos, rows):
    half = LANES // 2
    inv = jnp.power(ROPE_BASE, -jnp.arange(half, dtype=F32) / half)
    ang = pos.astype(F32)[:, None] * inv[None, :]
    cos = jnp.cos(ang)
    sin = jnp.sin(ang)
    cos_t = jnp.concatenate([cos, cos], axis=1)
    sin_t = jnp.concatenate([-sin, sin], axis=1)
    reps = rows // pos.shape[0]
    return jnp.tile(cos_t, (reps, 1)), jnp.tile(sin_t, (reps, 1))


def kernel(x_prompt, x_sample, cache_fox_k, cache_fox_v, cache_fox_logf, state_ret, state_conv,
           ln1, w_in, b_f, fox_qn, fox_kn, ret_gn, w_out, ln2, w_up, conv_w, conv_b, w_down):
    depth = ln1.shape[0]
    bp, tp, d = x_prompt.shape
    bs, ts, _ = x_sample.shape
    past = cache_fox_k.shape[2]
    n_heads, hd = cache_fox_k.shape[3], cache_fox_k.shape[4]
    d_head = n_heads * hd
    assert hd == LANES and n_heads % HEADS_PER_STEP == 0 and ret_gn.shape[1:] == (n_heads, hd)

    cos_p, sin_p = _rope_tables(jnp.arange(tp), max(tp, min(512, bp * tp)))
    cos_s, sin_s = _rope_tables(past + jnp.arange(ts), max(ts, min(512, bs * ts)))
    cache_k = cache_fox_k.reshape(depth * bs, past * n_heads, hd)
    cache_v = cache_fox_v.reshape(depth * bs, past * n_heads, hd)
    state_r = state_ret.reshape(depth * bs, n_heads, hd, hd)
    zero_state = jnp.zeros((bp, n_heads, hd, hd), F32)

    xp = x_prompt.reshape(bp * tp, d)
    xs = x_sample.reshape(bs * ts, d)
    st_p, st_s = [], []
    for l in range(depth):
        r0 = N_FOX_KINDS * d_head + n_heads
        n_groups = n_heads // HEADS_PER_STEP
        w_ret = w_in[l][:, r0:]
        w_tiles = _tile_cols([(w_in[l], k * n_groups) for k in range(N_FOX_KINDS)]
                             + [(w_ret, k * n_groups) for k in range(N_RET_KINDS)],
                             HEADS_PER_STEP * hd, n_groups)
        wf = jnp.pad(w_in[l][:, N_FOX_KINDS * d_head:r0], ((0, 0), (0, hd - n_heads))).astype(BF16)
        bfp = jnp.pad(b_f[l], (0, hd - n_heads)).reshape(1, hd)
        qn = fox_qn[l].reshape(1, hd)
        kn = fox_kn[l].reshape(1, hd)
        f = w_down.shape[1]
        nf = f // min(FFN_WIDTH_TILE, f)
        wu = _tile_cols([(w_up[l], 0), (w_up[l], nf)], f // nf, nf)
        wd = w_down[l].astype(BF16)
        cw = conv_w[l]
        cb = conv_b[l].reshape(1, -1)
        ln1_l = ln1[l].reshape(1, d)
        ln2_l = ln2[l].reshape(1, d)

        def group(x, cos_t, sin_t, batch, seq):
            return _inproj(x, ln1_l, w_tiles, wf, bfp, qn, kn, cos_t, sin_t,
                           batch=batch, seq=seq, n_heads=n_heads)

        fq, fkb, fvb, rq, rk, rv, rg, fk, fv, lf = group(xp, cos_p, sin_p, bp, tp)
        c_row = _cumsum(lf.reshape(bp, tp, n_heads).transpose(0, 2, 1))
        o_fox = _fox_prompt(fq, fkb, fvb, c_row.reshape(bp, n_heads, 1, tp))
        o_ret, s_ret_p = _retention(rq, rk, rv, rg, zero_state, ret_gn[l], s0_block0=0)
        xm, h2 = _outproj(o_fox, o_ret, xp, w_out[l], ln2_l)
        xp, conv_p = _convffn(h2, xm, wu, cw, cb, wd, None, batch=bp, seq=tp)
        st_p.append((fk.reshape(bp, tp, n_heads, hd), fv.reshape(bp, tp, n_heads, hd),
                     lf.reshape(bp, tp, n_heads), s_ret_p, conv_p))

        fq, fkb, fvb, rq, rk, rv, rg, fk, fv, lf = group(xs, cos_s, sin_s, bs, ts)
        lf_all = jnp.concatenate([cache_fox_logf[l], lf.reshape(bs, ts, n_heads)], axis=1)
        pad = (-lf_all.shape[1]) % LANES
        lf_all = jnp.pad(lf_all, ((0, 0), (0, pad), (0, 0)))
        c_row = _cumsum(lf_all.transpose(0, 2, 1))
        o_fox = _fox_sample(fq, fkb, fvb, cache_k, cache_v, c_row, layer=l)
        o_ret, s_ret_s = _retention(rq, rk, rv, rg, state_r, ret_gn[l], s0_block0=l * bs)
        xm, h2 = _outproj(o_fox, o_ret, xs, w_out[l], ln2_l)
        xs, conv_s = _convffn(h2, xm, wu, cw, cb, wd, state_conv[l], batch=bs, seq=ts)
        st_s.append((fk.reshape(bs, ts, n_heads, hd), fv.reshape(bs, ts, n_heads, hd),
                     lf.reshape(bs, ts, n_heads), s_ret_s, conv_s))

    stack = lambda st, k: jnp.stack([s[k] for s in st])
    return (xp.reshape(bp, tp, d), xs.reshape(bs, ts, d),
            stack(st_p, 0), stack(st_p, 1), stack(st_p, 2), stack(st_p, 3), stack(st_p, 4),
            stack(st_s, 0), stack(st_s, 1), stack(st_s, 2), stack(st_s, 3), stack(st_s, 4))
```

```python
import functools
import math

import jax
import jax.numpy as jnp
from jax import lax
from jax.experimental import pallas as pl
from jax.experimental.pallas import tpu as pltpu

EPS = 1e-6
ROPE_BASE = 10000.0
MASKED_LOGIT = -1e30
LANES = 128
VMEM_LIMIT_BYTES = 56 * 1024 * 1024
LOG2E = math.log2(math.e)
F32_EXP2_UNDERFLOW = 152.0

F32 = jnp.float32
BF16 = jnp.bfloat16

N_FOX_KINDS = 3
N_RET_KINDS = 4
HEADS_PER_STEP = 2
FOX_HEADS_PER_STEP = 2

NT_DIMS = (((1,), (1,)), ((), ()))
TN_DIMS = (((0,), (0,)), ((), ()))


def _params(n_axes):
    return pltpu.CompilerParams(dimension_semantics=("arbitrary",) * n_axes,
                                vmem_limit_bytes=VMEM_LIMIT_BYTES)


def _resident(block_shape, index_map):
    return pl.BlockSpec(block_shape, index_map, pipeline_mode=pl.Buffered(1))


def _rms(x):
    return x * lax.rsqrt(jnp.mean(x * x, axis=-1, keepdims=True) + EPS)


def _sigmoid(x):
    return 1.0 / (1.0 + jnp.exp(-x))


def _tile_cols_kernel(*refs):
    *src_refs, out_ref = refs
    c0 = 0
    for src in src_refs:
        width = src.shape[1]
        out_ref[0, :, c0:c0 + width] = src[...].astype(BF16)
        c0 += width


def _tile_cols(sources, width, n_tiles):
    d = sources[0][0].shape[0]
    specs = [pl.BlockSpec((d, width), functools.partial(lambda first, t: (0, first + t), first))
             for _, first in sources]
    return pl.pallas_call(
        _tile_cols_kernel,
        grid=(n_tiles,),
        in_specs=specs,
        out_specs=pl.BlockSpec((1, d, width * len(sources)), lambda t: (t, 0, 0)),
        out_shape=jax.ShapeDtypeStruct((n_tiles, d, width * len(sources)), BF16),
        compiler_params=_params(1),
    )(*[w for w, _ in sources])


def _inproj_kernel(x_ref, ln1_ref, w_ref, wf_ref, bf_ref, qn_ref, kn_ref, cos_ref, sin_ref,
                   fq_ref, fkb_ref, fvb_ref, rq_ref, rk_ref, rv_ref, rg_ref, fk_ref, fv_ref, logf_ref,
                   h_ref, *, nb, seg, n_fox):
    j = pl.program_id(1)

    @pl.when(j == 0)
    def _():
        h = (_rms(x_ref[...]) * ln1_ref[...]).astype(BF16)
        h_ref[...] = h
        z = jnp.dot(h, wf_ref[...], preferred_element_type=F32) + bf_ref[...]
        logf = jnp.minimum(z, 0.0) - jnp.log1p(jnp.exp(-jnp.abs(z)))
        logf_ref[...] = logf[:, :n_fox]

    cos = cos_ref[...]
    sin = sin_ref[...]
    hd = LANES

    def project(kind):
        c0 = kind * HEADS_PER_STEP * hd
        p = jnp.dot(h_ref[...], w_ref[0, :, c0:c0 + HEADS_PER_STEP * hd],
                    preferred_element_type=F32)
        return [p[:, e * hd:(e + 1) * hd] for e in range(HEADS_PER_STEP)]

    def head_major(a):
        return a.reshape(nb, seg, hd).astype(BF16)

    def rope(a):
        return a * cos + pltpu.roll(a, hd // 2, 1) * sin

    for e, a in enumerate(project(0)):
        fq_ref[:, e] = head_major(_rms(a) * qn_ref[...])
    for e, a in enumerate(project(1)):
        fk = _rms(a) * kn_ref[...]
        fk_ref[:, e * hd:(e + 1) * hd] = fk
        fkb_ref[:, e] = head_major(fk)
    for e, a in enumerate(project(2)):
        fv_ref[:, e * hd:(e + 1) * hd] = a
        fvb_ref[:, e] = head_major(a)
    for e, a in enumerate(project(3)):
        rq_ref[:, e] = head_major(rope(a))
    for e, a in enumerate(project(4)):
        rk_ref[:, e] = head_major(rope(a) * (hd ** -0.5))
    for e, a in enumerate(project(5)):
        rv_ref[:, e] = head_major(a)
    for e, a in enumerate(project(6)):
        rg_ref[:, e] = head_major(a)


def _inproj(x, ln1, w_tiles, wf, bfp, qn, kn, cos_t, sin_t, *, batch, seq, n_heads):
    tokens, d = x.shape
    hd = LANES
    tm = min(512, tokens)
    seg = min(seq, tm)
    nb = tm // seg
    tps = seq // seg
    n_groups = n_heads // HEADS_PER_STEP
    gcols = HEADS_PER_STEP * hd
    grid = (tokens // tm, n_groups)

    hm_shape = jax.ShapeDtypeStruct((batch, n_heads, seq, hd), BF16)
    hm_spec = pl.BlockSpec((nb, HEADS_PER_STEP, seg, hd), lambda i, j: (i // tps, j, i % tps, 0))
    tok_shape = jax.ShapeDtypeStruct((tokens, n_heads * hd), F32)
    tok_spec = pl.BlockSpec((tm, gcols), lambda i, j: (i, j))

    return pl.pallas_call(
        functools.partial(_inproj_kernel, nb=nb, seg=seg, n_fox=n_heads),
        grid=grid,
        in_specs=[pl.BlockSpec((tm, d), lambda i, j: (i, 0)),
                  _resident((1, d), lambda i, j: (0, 0))]
                 + [pl.BlockSpec((1, d, w_tiles.shape[2]), lambda i, j: (j, 0, 0)),
                    _resident((d, hd), lambda i, j: (0, 0)),
                    _resident((1, hd), lambda i, j: (0, 0)),
                    _resident((1, hd), lambda i, j: (0, 0)),
                    _resident((1, hd), lambda i, j: (0, 0)),
                    pl.BlockSpec((tm, hd), lambda i, j: (i % tps, 0)),
                    pl.BlockSpec((tm, hd), lambda i, j: (i % tps, 0))],
        out_specs=[hm_spec] * 7 + [tok_spec, tok_spec,
                                   pl.BlockSpec((tm, n_heads), lambda i, j: (i, 0))],
        out_shape=[hm_shape] * 7 + [tok_shape, tok_shape,
                                    jax.ShapeDtypeStruct((tokens, n_heads), F32)],
        scratch_shapes=[pltpu.VMEM((tm, d), BF16)],
        compiler_params=_params(2),
    )(x, ln1, w_tiles, wf, bfp, qn, kn, cos_t, sin_t)


def _cumsum_kernel(lf_ref, c_ref, *, ch):
    total = lf_ref.shape[2]
    r = lax.broadcasted_iota(jnp.int32, (ch, ch), 0)
    c = lax.broadcasted_iota(jnp.int32, (ch, ch), 1)
    upper = (r <= c).astype(F32)
    carry = jnp.zeros((lf_ref.shape[1], 1), F32)
    for k in range(total // ch):
        x = lf_ref[0, :, k * ch:(k + 1) * ch]
        cs = jnp.dot(x, upper, precision=lax.Precision.HIGHEST, preferred_element_type=F32) + carry
        c_ref[0, :, k * ch:(k + 1) * ch] = cs
        carry = cs[:, ch - 1:ch]


def _cumsum(lf_t):
    batch, n_heads, total = lf_t.shape
    ch = 256 if total % 256 == 0 else LANES
    spec = pl.BlockSpec((1, n_heads, total), lambda b: (b, 0, 0))
    return pl.pallas_call(
        functools.partial(_cumsum_kernel, ch=ch),
        grid=(batch,), in_specs=[spec], out_specs=spec,
        out_shape=jax.ShapeDtypeStruct(lf_t.shape, F32),
        compiler_params=_params(1),
    )(lf_t)


def _softmax_step(z, v, m_ref, l_ref, acc_ref):
    tk = z.shape[1]
    m_prev = m_ref[...]
    m_next = jnp.maximum(m_prev, jnp.max(z, axis=1, keepdims=True))
    p = jnp.exp2(z - pltpu.repeat(m_next, tk // LANES, 1))
    alpha = jnp.exp2(m_prev - m_next)
    l_ref[...] = alpha * l_ref[...] + jnp.sum(p, axis=1, keepdims=True)
    acc_ref[...] = alpha * acc_ref[...] + jnp.dot(p.astype(BF16), v, preferred_element_type=F32)
    m_ref[...] = m_next


def _fox_prompt_kernel(first_ref, q_ref, k_ref, v_ref, c_ref, o_ref, m_ref, l_ref, acc_ref, *, tq, scale):
    b, hg, qi = pl.program_id(0), pl.program_id(1), pl.program_id(2)
    group, hd = q_ref.shape[1], q_ref.shape[3]
    m_ref[...] = jnp.full(m_ref.shape, MASKED_LOGIT, F32)
    l_ref[...] = jnp.zeros(l_ref.shape, F32)
    acc_ref[...] = jnp.zeros(acc_ref.shape, F32)

    def logits(e, k0):
        k = k_ref[0, e, pl.ds(k0, tq), :]
        s = lax.dot_general(q_ref[0, e], k, NT_DIMS, preferred_element_type=F32)
        return s * (scale * LOG2E) - c_ref[0, e, :, pl.ds(k0, tq)] * LOG2E

    def body(j, carry):
        k0 = pl.multiple_of(j * tq, tq)
        for e in range(group):
            _softmax_step(logits(e, k0), v_ref[0, e, pl.ds(k0, tq), :],
                          m_ref.at[e], l_ref.at[e], acc_ref.at[e])
        return carry

    first = first_ref[(b * pl.num_programs(1) + hg) * pl.num_programs(2) + qi]
    lax.fori_loop(first, qi, body, 0)

    k0 = pl.multiple_of(qi * tq, tq)
    row = lax.broadcasted_iota(jnp.int32, (tq, tq), 0)
    col = lax.broadcasted_iota(jnp.int32, (tq, tq), 1)
    for e in range(group):
        z = jnp.where(col <= row, logits(e, k0), MASKED_LOGIT)
        _softmax_step(z, v_ref[0, e, pl.ds(k0, tq), :], m_ref.at[e], l_ref.at[e], acc_ref.at[e])
        o_ref[:, e * hd:(e + 1) * hd] = (acc_ref[e] / l_ref[e]).astype(o_ref.dtype)


def _first_key_tiles(c_row, qn, kn, *, tq, group, scale):
    batch, n_heads, seq = c_row.shape
    hd = qn.shape[-1]
    bf16_slack = (1.0 + 2.0 ** -8) ** 2
    bound = (hd * scale * LOG2E * bf16_slack) * jnp.max(jnp.abs(qn)) * jnp.max(jnp.abs(kn))
    c_first_row = c_row[:, :, 0::tq]
    c_last_key = c_row[:, :, tq - 1::tq]
    gap = (c_last_key[:, :, None, :] - c_first_row[:, :, :, None]) * LOG2E
    nq = seq // tq
    earlier = jnp.arange(nq)[None, :] < jnp.arange(nq)[:, None]
    dead = (gap > 2.0 * bound + F32_EXP2_UNDERFLOW) & earlier
    first = jnp.sum(dead, axis=-1).astype(jnp.int32)
    return jnp.min(first.reshape(batch, n_heads // group, group, nq), axis=2).reshape(-1)


def _fox_prompt(q, k, v, c_row, qn, kn):
    batch, n_heads, seq, hd = q.shape
    tq = min(512, seq)
    nq = seq // tq
    group = FOX_HEADS_PER_STEP
    scale = hd ** -0.5
    first = _first_key_tiles(c_row, qn, kn, tq=tq, group=group, scale=scale)
    kv_spec = pl.BlockSpec((1, group, seq, hd), lambda b, h, i, first: (b, h, 0, 0))
    return pl.pallas_call(
        functools.partial(_fox_prompt_kernel, tq=tq, scale=scale),
        grid_spec=pltpu.PrefetchScalarGridSpec(
            num_scalar_prefetch=1,
            grid=(batch, n_heads // group, nq),
            in_specs=[pl.BlockSpec((1, group, tq, hd), lambda b, h, i, first: (b, h, i, 0)), kv_spec, kv_spec,
                      pl.BlockSpec((1, group, 1, seq), lambda b, h, i, first: (b, h, 0, 0))],
            out_specs=pl.BlockSpec((tq, group * hd), lambda b, h, i, first: (b * nq + i, h)),
            scratch_shapes=[pltpu.VMEM((group, tq, LANES), F32), pltpu.VMEM((group, tq, LANES), F32),
                            pltpu.VMEM((group, tq, hd), F32)]),
        out_shape=jax.ShapeDtypeStruct((batch * seq, n_heads * hd), BF16),
        compiler_params=_params(3),
    )(first, q, k, v, c_row.reshape(batch, n_heads, 1, seq))


def _fox_sample_kernel(q_ref, kn_ref, vn_ref, kc_ref, vc_ref, c_ref, o_ref, *, past, scale):
    n_heads, new, hd = q_ref.shape[1:]
    row = lax.broadcasted_iota(jnp.int32, (new, new), 0)
    col = lax.broadcasted_iota(jnp.int32, (new, new), 1)
    for h in range(n_heads):
        q = q_ref[0, h]
        kc = kc_ref[0, pl.ds(h, past, stride=n_heads), :].astype(BF16)
        vc = vc_ref[0, pl.ds(h, past, stride=n_heads), :].astype(BF16)
        z_c = (lax.dot_general(q, kc, NT_DIMS, preferred_element_type=F32) * scale
               - c_ref[0, h:h + 1, 0:past])
        z_n = (lax.dot_general(q, kn_ref[0, h], NT_DIMS, preferred_element_type=F32) * scale
               - c_ref[0, h:h + 1, past:past + new])
        z_n = jnp.where(col <= row, z_n, MASKED_LOGIT)
        m = jnp.maximum(jnp.max(z_c, axis=1, keepdims=True), jnp.max(z_n, axis=1, keepdims=True))
        p_c = jnp.exp(z_c - m)
        p_n = jnp.exp(z_n - m)
        l = jnp.sum(p_c, axis=1, keepdims=True) + jnp.sum(p_n, axis=1, keepdims=True)
        acc = (jnp.dot(p_c.astype(BF16), vc, preferred_element_type=F32)
               + jnp.dot(p_n.astype(BF16), vn_ref[0, h], preferred_element_type=F32))
        o_ref[:, h * hd:(h + 1) * hd] = (acc / l).astype(o_ref.dtype)


def _fox_sample(q, k_new, v_new, cache_k, cache_v, c_row, *, layer):
    batch, n_heads, new, hd = q.shape
    past = cache_k.shape[1] // n_heads
    new_spec = pl.BlockSpec((1, n_heads, new, hd), lambda b: (b, 0, 0, 0))
    cache_spec = pl.BlockSpec((1, past * n_heads, hd), lambda b: (layer * batch + b, 0, 0))
    return pl.pallas_call(
        functools.partial(_fox_sample_kernel, past=past, scale=hd ** -0.5),
        grid=(batch,),
        in_specs=[new_spec, new_spec, new_spec, cache_spec, cache_spec,
                  pl.BlockSpec((1, n_heads, c_row.shape[2]), lambda b: (b, 0, 0))],
        out_specs=pl.BlockSpec((new, n_heads * hd), lambda b: (b, 0)),
        out_shape=jax.ShapeDtypeStruct((batch * new, n_heads * hd), BF16),
        compiler_params=_params(1),
    )(q, k_new, v_new, cache_k, cache_v, c_row)


def _retention_kernel(q_ref, k_ref, v_ref, g_ref, s0_ref, gn_ref, o_ref, s_out_ref,
                      s_ref, decay_ref, inter_ref, upd_ref, *, log_gammas):
    b = pl.program_id(0)
    c = pl.program_id(1)
    n_heads, chunk, hd = q_ref.shape[1:]

    @pl.when((b == 0) & (c == 0))
    def _():
        i = lax.broadcasted_iota(jnp.int32, (chunk, chunk), 0)
        jj = lax.broadcasted_iota(jnp.int32, (chunk, chunk), 1)
        diff = (i - jj).astype(F32)
        pos = lax.broadcasted_iota(jnp.int32, (chunk, hd), 0).astype(F32)
        for h, lg in enumerate(log_gammas):
            decay_ref[h] = jnp.where(diff >= 0.0, jnp.exp(jnp.maximum(diff, 0.0) * lg), 0.0)
            inter_ref[h] = jnp.exp((pos + 1.0) * lg)
            upd_ref[h] = jnp.exp((chunk - 1.0 - pos) * lg)

    @pl.when(c == 0)
    def _():
        s_ref[...] = s0_ref[0]

    for h, lg in enumerate(log_gammas):
        q = q_ref[0, h]
        k = k_ref[0, h]
        v = v_ref[0, h]
        s_prev = s_ref[h]
        scores = lax.dot_general(q, k, NT_DIMS, preferred_element_type=F32) * decay_ref[h]
        o = jnp.dot(scores.astype(BF16), v, preferred_element_type=F32)
        o = o + jnp.dot(q, s_prev.astype(BF16), preferred_element_type=F32) * inter_ref[h]
        kd = (k.astype(F32) * upd_ref[h]).astype(BF16)
        s_ref[h] = math.exp(chunk * lg) * s_prev + lax.dot_general(kd, v, TN_DIMS, preferred_element_type=F32)
        g = g_ref[0, h].astype(F32)
        y = _rms(o) * gn_ref[h:h + 1, :]
        o_ref[:, h * hd:(h + 1) * hd] = (y * (g * _sigmoid(g))).astype(o_ref.dtype)

    @pl.when(c == pl.num_programs(1) - 1)
    def _():
        s_out_ref[0] = s_ref[...]


def _retention(q, k, v, g, s0, gn, *, s0_block0):
    batch, n_heads, seq, hd = q.shape
    chunk = min(256, seq)
    nc = seq // chunk
    log_gammas = tuple(math.log(1.0 - 2.0 ** (-5.0 - h)) for h in range(n_heads))
    qkv_spec = pl.BlockSpec((1, n_heads, chunk, hd), lambda b, c: (b, 0, c, 0))
    return pl.pallas_call(
        functools.partial(_retention_kernel, log_gammas=log_gammas),
        grid=(batch, nc),
        in_specs=[qkv_spec] * 4 + [pl.BlockSpec((1, n_heads, hd, hd), lambda b, c: (s0_block0 + b, 0, 0, 0)),
                                   _resident((n_heads, hd), lambda b, c: (0, 0))],
        out_specs=[pl.BlockSpec((chunk, n_heads * hd), lambda b, c: (b * nc + c, 0)),
                   pl.BlockSpec((1, n_heads, hd, hd), lambda b, c: (b, 0, 0, 0))],
        out_shape=[jax.ShapeDtypeStruct((batch * seq, n_heads * hd), BF16),
                   jax.ShapeDtypeStruct((batch, n_heads, hd, hd), F32)],
        scratch_shapes=[pltpu.VMEM((n_heads, hd, hd), F32), pltpu.VMEM((n_heads, chunk, chunk), F32),
                        pltpu.VMEM((n_heads, chunk, hd), F32), pltpu.VMEM((n_heads, chunk, hd), F32)],
        compiler_params=_params(2),
    )(q, k, v, g, s0, gn)


def _outproj_kernel(of_ref, or_ref, x_ref, w_ref, ln2_ref, xm_ref, h2_ref, wb_ref):
    @pl.when(pl.program_id(0) == 0)
    def _():
        wb_ref[...] = w_ref[...].astype(BF16)

    half = of_ref.shape[1]
    y = jnp.dot(of_ref[...], wb_ref[0:half, :], preferred_element_type=F32)
    y = y + jnp.dot(or_ref[...], wb_ref[half:2 * half, :], preferred_element_type=F32)
    xm = x_ref[...] + y
    xm_ref[...] = xm
    h2_ref[...] = (_rms(xm) * ln2_ref[...]).astype(BF16)


def _outproj(o_fox, o_ret, x, w_out, ln2):
    tokens, d = x.shape
    half = o_fox.shape[1]
    tm = min(512, tokens)
    row = lambda i: (i, 0)
    return pl.pallas_call(
        _outproj_kernel,
        grid=(tokens // tm,),
        in_specs=[pl.BlockSpec((tm, half), row), pl.BlockSpec((tm, half), row),
                  pl.BlockSpec((tm, d), row),
                  _resident((2 * half, d), lambda i: (0, 0)), _resident((1, d), lambda i: (0, 0))],
        out_specs=[pl.BlockSpec((tm, d), row), pl.BlockSpec((tm, d), row)],
        out_shape=[jax.ShapeDtypeStruct((tokens, d), F32), jax.ShapeDtypeStruct((tokens, d), BF16)],
        scratch_shapes=[pltpu.VMEM((2 * half, d), BF16)],
        compiler_params=_params(1),
    )(o_fox, o_ret, x, w_out, ln2)


ROW_BLOCK = 8
DOWN_CHUNK = 512
SUB_ROWS = 512
FFN_TOKEN_TILE = 1024
FFN_WIDTH_TILE = 512


def _convffn_kernel(*refs, n_seg, seg, tps, carried):
    if carried:
        (h2_ref, xm_ref, wu_ref, cwa_ref, cwg_ref, cba_ref, cbg_ref, wd_ref,
         out_ref, ta_ref, tg_ref, z_ref, ca_ref, cg_ref) = refs
        hists = (None, None)
        carries = (ca_ref, cg_ref)
    else:
        (h2_ref, xm_ref, wu_ref, cwa_ref, cwg_ref, cba_ref, cbg_ref, wd_ref, ha_ref, hg_ref,
         out_ref, ta_ref, tg_ref, z_ref) = refs
        hists = (ha_ref, hg_ref)
        carries = (None, None)
    i = pl.program_id(0)
    j = pl.program_id(1)
    tf = wu_ref.shape[2] // 2
    d = out_ref.shape[1]
    tm = n_seg * seg
    col0 = pl.multiple_of(j * tf, tf)

    @pl.when(j == 0)
    def _():
        out_ref[...] = xm_ref[...]

    if carried:
        @pl.when(i % tps == 0)
        def _():
            for carry_ref in carries:
                carry_ref[0:2, pl.ds(col0, tf)] = jnp.zeros((2, tf), F32)

    sub = min(tm, SUB_ROWS)
    halves = ((0, cwa_ref, cba_ref, ta_ref, carries[0], hists[0]),
              (tf, cwg_ref, cbg_ref, tg_ref, carries[1], hists[1]))
    ups = [[jnp.dot(h2_ref[r:r + sub, :], wu_ref[0, :, c:c + tf], preferred_element_type=F32)
            for c, *_ in halves] for r in range(0, tm, sub)]

    row = lax.broadcasted_iota(jnp.int32, (ROW_BLOCK, tf), 0)
    taps = [[jnp.broadcast_to(cw_ref[t:t + 1, :], (ROW_BLOCK, tf)) for t in range(3)]
            for _, cw_ref, *_ in halves]
    bias = [jnp.broadcast_to(cb_ref[...], (ROW_BLOCK, tf)) for _, _, cb_ref, *_ in halves]
    prev = [None, None]

    def conv_block(x, r0):
        _, _, _, tail_ref, carry_ref, hist_ref = halves[x]
        s, off = divmod(r0, seg)
        if off == 0:
            if carried:
                h0 = carry_ref[0:1, pl.ds(col0, tf)]
                h1 = carry_ref[1:2, pl.ds(col0, tf)]
            else:
                h0 = hist_ref[s, 0:1, :]
                h1 = hist_ref[s, 1:2, :]
            prev[x] = (jnp.broadcast_to(h1, (ROW_BLOCK, tf)), jnp.where(row == 0, h0, h1))
        u = ups[r0 // sub][x]
        cur = u[r0 % sub:r0 % sub + ROW_BLOCK]
        rot1 = pltpu.roll(cur, 1, 0)
        rot2 = pltpu.roll(cur, 2, 0)
        um1 = jnp.where(row == 0, prev[x][0], rot1)
        um2 = jnp.where(row <= 1, prev[x][1], rot2)
        prev[x] = (rot1, rot2)
        if off == seg - ROW_BLOCK:
            tail_ref[s] = cur[ROW_BLOCK - 2:ROW_BLOCK, :]
        if carried and r0 == tm - ROW_BLOCK:
            carry_ref[0:2, pl.ds(col0, tf)] = cur[ROW_BLOCK - 2:ROW_BLOCK, :]
        return bias[x] + (um2 * taps[x][0] + um1 * taps[x][1] + cur * taps[x][2])

    pack = 2 * ROW_BLOCK
    for r_sub in range(0, tm, sub):
        for r0 in range(r_sub, r_sub + sub, pack):
            a = jnp.concatenate([conv_block(0, r0), conv_block(0, r0 + ROW_BLOCK)], axis=0)
            g = jnp.concatenate([conv_block(1, r0), conv_block(1, r0 + ROW_BLOCK)], axis=0)
            z_ref[r0:r0 + pack, :] = ((g * _sigmoid(g)) * a).astype(BF16)
        z = z_ref[r_sub:r_sub + sub, :]
        for c0 in range(0, d, DOWN_CHUNK):
            out_ref[r_sub:r_sub + sub, c0:c0 + DOWN_CHUNK] += jnp.dot(
                z, wd_ref[:, c0:c0 + DOWN_CHUNK], preferred_element_type=F32)


def _convffn(h2, xm, w_up, conv_w, conv_b, w_down, hist, *, batch, seq):
    tokens, d = xm.shape
    f = w_down.shape[0]
    nf = w_up.shape[0]
    tf = f // nf
    tm = min(FFN_TOKEN_TILE, tokens)
    seg = min(seq, tm)
    n_seg = tm // seg
    tps = seq // seg
    carried = hist is None
    assert not carried or n_seg == 1, "a carried conv state needs one sequence per token tile"
    row = lambda i, j: (i, 0)
    a_col = lambda i, j: (0, j)
    g_col = lambda i, j: (0, nf + j)
    in_specs = [pl.BlockSpec((tm, d), row), _resident((tm, d), row),
                pl.BlockSpec((1, d, 2 * tf), lambda i, j: (j, 0, 0)),
                pl.BlockSpec((3, tf), a_col), pl.BlockSpec((3, tf), g_col),
                pl.BlockSpec((1, tf), a_col), pl.BlockSpec((1, tf), g_col),
                pl.BlockSpec((tf, d), lambda i, j: (j, 0))]
    args = [h2, xm, w_up, conv_w, conv_w, conv_b, conv_b, w_down]
    scratch = [pltpu.VMEM((tm, tf), BF16)]
    if carried:
        scratch += [pltpu.VMEM((8, f), F32), pltpu.VMEM((8, f), F32)]
    else:
        in_specs += [pl.BlockSpec((n_seg, 2, tf), lambda i, j: (i, 0, j)),
                     pl.BlockSpec((n_seg, 2, tf), lambda i, j: (i, 0, nf + j))]
        args += [hist, hist]
    tail_spec = pl.BlockSpec((n_seg, 2, tf), lambda i, j: (i, 0, j))
    tail_shape = jax.ShapeDtypeStruct((batch * tps, 2, f), F32)
    out, tails_a, tails_g = pl.pallas_call(
        functools.partial(_convffn_kernel, n_seg=n_seg, seg=seg, tps=tps, carried=carried),
        grid=(tokens // tm, nf),
        in_specs=in_specs,
        out_specs=[pl.BlockSpec((tm, d), row), tail_spec, tail_spec],
        out_shape=[jax.ShapeDtypeStruct((tokens, d), F32), tail_shape, tail_shape],
        scratch_shapes=scratch,
        compiler_params=_params(2),
    )(*args)
    last = lambda t: t.reshape(batch, tps, 2, f)[:, tps - 1]
    return out, jnp.concatenate([last(tails_a), last(tails_g)], axis=-1)


def _rope_tables(pos, rows):
    half = LANES // 2
    inv = jnp.power(ROPE_BASE, -jnp.arange(half, dtype=F32) / half)
    ang = pos.astype(F32)[:, None] * inv[None, :]
    cos = jnp.cos(ang)
    sin = jnp.sin(ang)
    cos_t = jnp.concatenate([cos, cos], axis=1)
    sin_t = jnp.concatenate([-sin, sin], axis=1)
    reps = rows // pos.shape[0]
    return jnp.tile(cos_t, (reps, 1)), jnp.tile(sin_t, (reps, 1))


def kernel(x_prompt, x_sample, cache_fox_k, cache_fox_v, cache_fox_logf, state_ret, state_conv,
           ln1, w_in, b_f, fox_qn, fox_kn, ret_gn, w_out, ln2, w_up, conv_w, conv_b, w_down):
    depth = ln1.shape[0]
    bp, tp, d = x_prompt.shape
    bs, ts, _ = x_sample.shape
    past = cache_fox_k.shape[2]
    n_heads, hd = cache_fox_k.shape[3], cache_fox_k.shape[4]
    d_head = n_heads * hd
    assert hd == LANES and n_heads % HEADS_PER_STEP == 0 and ret_gn.shape[1:] == (n_heads, hd)

    cos_p, sin_p = _rope_tables(jnp.arange(tp), max(tp, min(512, bp * tp)))
    cos_s, sin_s = _rope_tables(past + jnp.arange(ts), max(ts, min(512, bs * ts)))
    cache_k = cache_fox_k.reshape(depth * bs, past * n_heads, hd)
    cache_v = cache_fox_v.reshape(depth * bs, past * n_heads, hd)
    state_r = state_ret.reshape(depth * bs, n_heads, hd, hd)
    zero_state = jnp.zeros((bp, n_heads, hd, hd), F32)

    xp = x_prompt.reshape(bp * tp, d)
    xs = x_sample.reshape(bs * ts, d)
    st_p, st_s = [], []
    for l in range(depth):
        r0 = N_FOX_KINDS * d_head + n_heads
        n_groups = n_heads // HEADS_PER_STEP
        w_ret = w_in[l][:, r0:]
        w_tiles = _tile_cols([(w_in[l], k * n_groups) for k in range(N_FOX_KINDS)]
                             + [(w_ret, k * n_groups) for k in range(N_RET_KINDS)],
                             HEADS_PER_STEP * hd, n_groups)
        wf = jnp.pad(w_in[l][:, N_FOX_KINDS * d_head:r0], ((0, 0), (0, hd - n_heads))).astype(BF16)
        bfp = jnp.pad(b_f[l], (0, hd - n_heads)).reshape(1, hd)
        qn = fox_qn[l].reshape(1, hd)
        kn = fox_kn[l].reshape(1, hd)
        f = w_down.shape[1]
        nf = f // min(FFN_WIDTH_TILE, f)
        wu = _tile_cols([(w_up[l], 0), (w_up[l], nf)], f // nf, nf)
        wd = w_down[l].astype(BF16)
        cw = conv_w[l]
        cb = conv_b[l].reshape(1, -1)
        ln1_l = ln1[l].reshape(1, d)
        ln2_l = ln2[l].reshape(1, d)

        def group(x, cos_t, sin_t, batch, seq):
            return _inproj(x, ln1_l, w_tiles, wf, bfp, qn, kn, cos_t, sin_t,
                           batch=batch, seq=seq, n_heads=n_heads)

        fq, fkb, fvb, rq, rk, rv, rg, fk, fv, lf = group(xp, cos_p, sin_p, bp, tp)
        c_row = _cumsum(lf.reshape(bp, tp, n_heads).transpose(0, 2, 1))
        o_fox = _fox_prompt(fq, fkb, fvb, c_row, qn, kn)
        o_ret, s_ret_p = _retention(rq, rk, rv, rg, zero_state, ret_gn[l], s0_block0=0)
        xm, h2 = _outproj(o_fox, o_ret, xp, w_out[l], ln2_l)
        xp, conv_p = _convffn(h2, xm, wu, cw, cb, wd, None, batch=bp, seq=tp)
        st_p.append((fk.reshape(bp, tp, n_heads, hd), fv.reshape(bp, tp, n_heads, hd),
                     lf.reshape(bp, tp, n_heads), s_ret_p, conv_p))

        fq, fkb, fvb, rq, rk, rv, rg, fk, fv, lf = group(xs, cos_s, sin_s, bs, ts)
        lf_all = jnp.concatenate([cache_fox_logf[l], lf.reshape(bs, ts, n_heads)], axis=1)
        pad = (-lf_all.shape[1]) % LANES
        lf_all = jnp.pad(lf_all, ((0, 0), (0, pad), (0, 0)))
        c_row = _cumsum(lf_all.transpose(0, 2, 1))
        o_fox = _fox_sample(fq, fkb, fvb, cache_k, cache_v, c_row, layer=l)
        o_ret, s_ret_s = _retention(rq, rk, rv, rg, state_r, ret_gn[l], s0_block0=l * bs)
        xm, h2 = _outproj(o_fox, o_ret, xs, w_out[l], ln2_l)
        xs, conv_s = _convffn(h2, xm, wu, cw, cb, wd, state_conv[l], batch=bs, seq=ts)
        st_s.append((fk.reshape(bs, ts, n_heads, hd), fv.reshape(bs, ts, n_heads, hd),
                     lf.reshape(bs, ts, n_heads), s_ret_s, conv_s))

    stack = lambda st, k: jnp.stack([s[k] for s in st])
    return (xp.reshape(bp, tp, d), xs.reshape(bs, ts, d),
            stack(st_p, 0), stack(st_p, 1), stack(st_p, 2), stack(st_p, 3), stack(st_p, 4),
            stack(st_s, 0), stack(st_s, 1), stack(st_s, 2), stack(st_s, 3), stack(st_s, 4))
```

```python
import functools
import math

import jax
import jax.numpy as jnp
from jax import lax
from jax.experimental import pallas as pl
from jax.experimental.pallas import tpu as pltpu

EPS = 1e-6
ROPE_BASE = 10000.0
MASKED_LOGIT = -1e30
LANES = 128
VMEM_LIMIT_BYTES = 56 * 1024 * 1024
LOG2E = math.log2(math.e)
FOX_BOUNDED_LIMIT = 100.0
F32_EXP2_UNDERFLOW = 152.0

F32 = jnp.float32
BF16 = jnp.bfloat16

N_FOX_KINDS = 3
N_RET_KINDS = 4
HEADS_PER_STEP = 2
FOX_HEADS_PER_STEP = 2

NT_DIMS = (((1,), (1,)), ((), ()))
TN_DIMS = (((0,), (0,)), ((), ()))


def _params(n_axes):
    return pltpu.CompilerParams(dimension_semantics=("arbitrary",) * n_axes,
                                vmem_limit_bytes=VMEM_LIMIT_BYTES)


def _resident(block_shape, index_map):
    return pl.BlockSpec(block_shape, index_map, pipeline_mode=pl.Buffered(1))


def _rms(x):
    return x * lax.rsqrt(jnp.mean(x * x, axis=-1, keepdims=True) + EPS)


def _sigmoid(x):
    return 1.0 / (1.0 + jnp.exp(-x))


def _tile_cols_kernel(*refs):
    *src_refs, out_ref = refs
    c0 = 0
    for src in src_refs:
        width = src.shape[1]
        out_ref[0, :, c0:c0 + width] = src[...].astype(BF16)
        c0 += width


def _tile_cols(sources, width, n_tiles):
    d = sources[0][0].shape[0]
    specs = [pl.BlockSpec((d, width), functools.partial(lambda first, t: (0, first + t), first))
             for _, first in sources]
    return pl.pallas_call(
        _tile_cols_kernel,
        grid=(n_tiles,),
        in_specs=specs,
        out_specs=pl.BlockSpec((1, d, width * len(sources)), lambda t: (t, 0, 0)),
        out_shape=jax.ShapeDtypeStruct((n_tiles, d, width * len(sources)), BF16),
        compiler_params=_params(1),
    )(*[w for w, _ in sources])


def _inproj_kernel(x_ref, ln1_ref, w_ref, wf_ref, bf_ref, qn_ref, kn_ref, cos_ref, sin_ref,
                   fq_ref, fkb_ref, fvb_ref, rq_ref, rk_ref, rv_ref, rg_ref, fk_ref, fv_ref, logf_ref,
                   h_ref, *, nb, seg, n_fox):
    j = pl.program_id(1)

    @pl.when(j == 0)
    def _():
        h = (_rms(x_ref[...]) * ln1_ref[...]).astype(BF16)
        h_ref[...] = h
        z = jnp.dot(h, wf_ref[...], preferred_element_type=F32) + bf_ref[...]
        logf = jnp.minimum(z, 0.0) - jnp.log1p(jnp.exp(-jnp.abs(z)))
        logf_ref[...] = logf[:, :n_fox]

    cos = cos_ref[...]
    sin = sin_ref[...]
    hd = LANES

    def project(kind):
        c0 = kind * HEADS_PER_STEP * hd
        p = jnp.dot(h_ref[...], w_ref[0, :, c0:c0 + HEADS_PER_STEP * hd],
                    preferred_element_type=F32)
        return [p[:, e * hd:(e + 1) * hd] for e in range(HEADS_PER_STEP)]

    def head_major(a):
        return a.reshape(nb, seg, hd).astype(BF16)

    def rope(a):
        return a * cos + pltpu.roll(a, hd // 2, 1) * sin

    for e, a in enumerate(project(0)):
        fq_ref[:, e] = head_major(_rms(a) * qn_ref[...])
    for e, a in enumerate(project(1)):
        fk = _rms(a) * kn_ref[...]
        fk_ref[:, e * hd:(e + 1) * hd] = fk
        fkb_ref[:, e] = head_major(fk)
    for e, a in enumerate(project(2)):
        fv_ref[:, e * hd:(e + 1) * hd] = a
        fvb_ref[:, e] = head_major(a)
    for e, a in enumerate(project(3)):
        rq_ref[:, e] = head_major(rope(a))
    for e, a in enumerate(project(4)):
        rk_ref[:, e] = head_major(rope(a) * (hd ** -0.5))
    for e, a in enumerate(project(5)):
        rv_ref[:, e] = head_major(a)
    for e, a in enumerate(project(6)):
        rg_ref[:, e] = head_major(a)


def _inproj(x, ln1, w_tiles, wf, bfp, qn, kn, cos_t, sin_t, *, batch, seq, n_heads):
    tokens, d = x.shape
    hd = LANES
    tm = min(512, tokens)
    seg = min(seq, tm)
    nb = tm // seg
    tps = seq // seg
    n_groups = n_heads // HEADS_PER_STEP
    gcols = HEADS_PER_STEP * hd
    grid = (tokens // tm, n_groups)

    hm_shape = jax.ShapeDtypeStruct((batch, n_heads, seq, hd), BF16)
    hm_spec = pl.BlockSpec((nb, HEADS_PER_STEP, seg, hd), lambda i, j: (i // tps, j, i % tps, 0))
    tok_shape = jax.ShapeDtypeStruct((tokens, n_heads * hd), F32)
    tok_spec = pl.BlockSpec((tm, gcols), lambda i, j: (i, j))

    return pl.pallas_call(
        functools.partial(_inproj_kernel, nb=nb, seg=seg, n_fox=n_heads),
        grid=grid,
        in_specs=[pl.BlockSpec((tm, d), lambda i, j: (i, 0)),
                  _resident((1, d), lambda i, j: (0, 0))]
                 + [pl.BlockSpec((1, d, w_tiles.shape[2]), lambda i, j: (j, 0, 0)),
                    _resident((d, hd), lambda i, j: (0, 0)),
                    _resident((1, hd), lambda i, j: (0, 0)),
                    _resident((1, hd), lambda i, j: (0, 0)),
                    _resident((1, hd), lambda i, j: (0, 0)),
                    pl.BlockSpec((tm, hd), lambda i, j: (i % tps, 0)),
                    pl.BlockSpec((tm, hd), lambda i, j: (i % tps, 0))],
        out_specs=[hm_spec] * 7 + [tok_spec, tok_spec,
                                   pl.BlockSpec((tm, n_heads), lambda i, j: (i, 0))],
        out_shape=[hm_shape] * 7 + [tok_shape, tok_shape,
                                    jax.ShapeDtypeStruct((tokens, n_heads), F32)],
        scratch_shapes=[pltpu.VMEM((tm, d), BF16)],
        compiler_params=_params(2),
    )(x, ln1, w_tiles, wf, bfp, qn, kn, cos_t, sin_t)


def _cumsum_kernel(lf_ref, c_ref, *, ch):
    total = lf_ref.shape[2]
    r = lax.broadcasted_iota(jnp.int32, (ch, ch), 0)
    c = lax.broadcasted_iota(jnp.int32, (ch, ch), 1)
    upper = (r <= c).astype(F32)
    carry = jnp.zeros((lf_ref.shape[1], 1), F32)
    for k in range(total // ch):
        x = lf_ref[0, :, k * ch:(k + 1) * ch]
        cs = jnp.dot(x, upper, precision=lax.Precision.HIGHEST, preferred_element_type=F32) + carry
        c_ref[0, :, k * ch:(k + 1) * ch] = cs
        carry = cs[:, ch - 1:ch]


def _cumsum(lf_t):
    batch, n_heads, total = lf_t.shape
    ch = 256 if total % 256 == 0 else LANES
    spec = pl.BlockSpec((1, n_heads, total), lambda b: (b, 0, 0))
    return pl.pallas_call(
        functools.partial(_cumsum_kernel, ch=ch),
        grid=(batch,), in_specs=[spec], out_specs=spec,
        out_shape=jax.ShapeDtypeStruct(lf_t.shape, F32),
        compiler_params=_params(1),
    )(lf_t)


def _softmax_step(z, v, m_ref, l_ref, acc_ref):
    tk = z.shape[1]
    m_prev = m_ref[...]
    m_next = jnp.maximum(m_prev, jnp.max(z, axis=1, keepdims=True))
    p = jnp.exp2(z - pltpu.repeat(m_next, tk // LANES, 1))
    alpha = jnp.exp2(m_prev - m_next)
    l_ref[...] = alpha * l_ref[...] + jnp.sum(p, axis=1, keepdims=True)
    acc_ref[...] = alpha * acc_ref[...] + jnp.dot(p.astype(BF16), v, preferred_element_type=F32)
    m_ref[...] = m_next


def _fox_prompt_kernel(first_ref, q_ref, k_ref, v_ref, c_ref, o_ref, m_ref, l_ref, acc_ref, *, tq, scale):
    b, hg, qi = pl.program_id(0), pl.program_id(1), pl.program_id(2)
    group, hd = q_ref.shape[1], q_ref.shape[3]
    m_ref[...] = jnp.full(m_ref.shape, MASKED_LOGIT, F32)
    l_ref[...] = jnp.zeros(l_ref.shape, F32)
    acc_ref[...] = jnp.zeros(acc_ref.shape, F32)

    def logits(e, k0):
        k = k_ref[0, e, pl.ds(k0, tq), :]
        s = lax.dot_general(q_ref[0, e], k, NT_DIMS, preferred_element_type=F32)
        return s * (scale * LOG2E) - c_ref[0, e, :, pl.ds(k0, tq)] * LOG2E

    def body(j, carry):
        k0 = pl.multiple_of(j * tq, tq)
        for e in range(group):
            _softmax_step(logits(e, k0), v_ref[0, e, pl.ds(k0, tq), :],
                          m_ref.at[e], l_ref.at[e], acc_ref.at[e])
        return carry

    first = first_ref[(b * pl.num_programs(1) + hg) * pl.num_programs(2) + qi]
    lax.fori_loop(first, qi, body, 0)

    k0 = pl.multiple_of(qi * tq, tq)
    row = lax.broadcasted_iota(jnp.int32, (tq, tq), 0)
    col = lax.broadcasted_iota(jnp.int32, (tq, tq), 1)
    for e in range(group):
        z = jnp.where(col <= row, logits(e, k0), MASKED_LOGIT)
        _softmax_step(z, v_ref[0, e, pl.ds(k0, tq), :], m_ref.at[e], l_ref.at[e], acc_ref.at[e])
        o_ref[:, e * hd:(e + 1) * hd] = (acc_ref[e] / l_ref[e]).astype(o_ref.dtype)


def _fox_prompt_bounded_kernel(first_ref, q_ref, k_ref, v_ref, c2_ref, r2_ref, o_ref, l_ref, acc_ref, *, tq, scale):
    b, hg, qi = pl.program_id(0), pl.program_id(1), pl.program_id(2)
    group, hd = q_ref.shape[1], q_ref.shape[3]
    l_ref[...] = jnp.zeros(l_ref.shape, F32)
    acc_ref[...] = jnp.zeros(acc_ref.shape, F32)
    q0 = pl.multiple_of(qi * tq, tq)
    row = lax.broadcasted_iota(jnp.int32, (tq, tq), 0)
    col = lax.broadcasted_iota(jnp.int32, (tq, tq), 1)
    row_terms = [jnp.sum(jnp.where(row == col, r2_ref[0, e, :, pl.ds(q0, tq)], 0.0), axis=1, keepdims=True)
                 for e in range(group)]

    def step(e, k0, masked):
        k = k_ref[0, e, pl.ds(k0, tq), :]
        s = lax.dot_general(q_ref[0, e], k, NT_DIMS, preferred_element_type=F32)
        z = (s * (scale * LOG2E) - c2_ref[0, e, :, pl.ds(k0, tq)]) + row_terms[e]
        if masked:
            z = jnp.where(col <= row, z, MASKED_LOGIT)
        p = jnp.exp2(z)
        part = p[:, 0:LANES]
        for c0 in range(LANES, tq, LANES):
            part = part + p[:, c0:c0 + LANES]
        l_ref[e] += part
        acc_ref[e] += jnp.dot(p.astype(BF16), v_ref[0, e, pl.ds(k0, tq), :], preferred_element_type=F32)

    def body(j, carry):
        k0 = pl.multiple_of(j * tq, tq)
        for e in range(group):
            step(e, k0, False)
        return carry

    first = first_ref[(b * pl.num_programs(1) + hg) * pl.num_programs(2) + qi]
    lax.fori_loop(first, qi, body, 0)
    for e in range(group):
        step(e, q0, True)
        l = jnp.sum(l_ref[e], axis=1, keepdims=True)
        o_ref[:, e * hd:(e + 1) * hd] = (acc_ref[e] / l).astype(o_ref.dtype)


def _qk_logit_bound(qn, kn, hd, scale):
    bf16_slack = (1.0 + 2.0 ** -8) ** 2
    return (hd * scale * LOG2E * bf16_slack) * jnp.max(jnp.abs(qn)) * jnp.max(jnp.abs(kn))


def _first_key_tiles(c_row, bound, *, tq, group):
    batch, n_heads, seq = c_row.shape
    c_first_row = c_row[:, :, 0::tq]
    c_last_key = c_row[:, :, tq - 1::tq]
    gap = (c_last_key[:, :, None, :] - c_first_row[:, :, :, None]) * LOG2E
    nq = seq // tq
    earlier = jnp.arange(nq)[None, :] < jnp.arange(nq)[:, None]
    dead = (gap > 2.0 * bound + F32_EXP2_UNDERFLOW) & earlier
    first = jnp.sum(dead, axis=-1).astype(jnp.int32)
    return jnp.min(first.reshape(batch, n_heads // group, group, nq), axis=2).reshape(-1)


def _fox_prompt(q, k, v, c_row, qn, kn):
    batch, n_heads, seq, hd = q.shape
    tq = min(512, seq)
    nq = seq // tq
    group = FOX_HEADS_PER_STEP
    scale = hd ** -0.5
    bound = _qk_logit_bound(qn, kn, hd, scale)
    first = _first_key_tiles(c_row, bound, tq=tq, group=group)
    kv_spec = pl.BlockSpec((1, group, seq, hd), lambda b, h, i, first: (b, h, 0, 0))
    row_spec = pl.BlockSpec((1, group, 1, seq), lambda b, h, i, first: (b, h, 0, 0))
    q_spec = pl.BlockSpec((1, group, tq, hd), lambda b, h, i, first: (b, h, i, 0))
    out_spec = pl.BlockSpec((tq, group * hd), lambda b, h, i, first: (b * nq + i, h))
    out_shape = jax.ShapeDtypeStruct((batch * seq, n_heads * hd), BF16)
    stats = pltpu.VMEM((group, tq, LANES), F32)
    acc = pltpu.VMEM((group, tq, hd), F32)

    def running_max(c4):
        return pl.pallas_call(
            functools.partial(_fox_prompt_kernel, tq=tq, scale=scale),
            grid_spec=pltpu.PrefetchScalarGridSpec(
                num_scalar_prefetch=1, grid=(batch, n_heads // group, nq),
                in_specs=[q_spec, kv_spec, kv_spec, row_spec], out_specs=out_spec,
                scratch_shapes=[stats, stats, acc]),
            out_shape=out_shape, compiler_params=_params(3),
        )(first, q, k, v, c4)

    def bounded(c4):
        return pl.pallas_call(
            functools.partial(_fox_prompt_bounded_kernel, tq=tq, scale=scale),
            grid_spec=pltpu.PrefetchScalarGridSpec(
                num_scalar_prefetch=1, grid=(batch, n_heads // group, nq),
                in_specs=[q_spec, kv_spec, kv_spec, row_spec, row_spec], out_specs=out_spec,
                scratch_shapes=[stats, acc]),
            out_shape=out_shape, compiler_params=_params(3),
        )(first, q, k, v, c4 * LOG2E, c4 * LOG2E - bound)

    return lax.cond(2.0 * bound <= FOX_BOUNDED_LIMIT, bounded, running_max,
                    c_row.reshape(batch, n_heads, 1, seq))


def _fox_sample_kernel(q_ref, kn_ref, vn_ref, kc_ref, vc_ref, c_ref, o_ref, *, past, scale):
    n_heads, new, hd = q_ref.shape[1:]
    row = lax.broadcasted_iota(jnp.int32, (new, new), 0)
    col = lax.broadcasted_iota(jnp.int32, (new, new), 1)
    for h in range(n_heads):
        q = q_ref[0, h]
        kc = kc_ref[0, pl.ds(h, past, stride=n_heads), :].astype(BF16)
        vc = vc_ref[0, pl.ds(h, past, stride=n_heads), :].astype(BF16)
        z_c = (lax.dot_general(q, kc, NT_DIMS, preferred_element_type=F32) * scale
               - c_ref[0, h:h + 1, 0:past])
        z_n = (lax.dot_general(q, kn_ref[0, h], NT_DIMS, preferred_element_type=F32) * scale
               - c_ref[0, h:h + 1, past:past + new])
        z_n = jnp.where(col <= row, z_n, MASKED_LOGIT)
        m = jnp.maximum(jnp.max(z_c, axis=1, keepdims=True), jnp.max(z_n, axis=1, keepdims=True))
        p_c = jnp.exp(z_c - m)
        p_n = jnp.exp(z_n - m)
        l = jnp.sum(p_c, axis=1, keepdims=True) + jnp.sum(p_n, axis=1, keepdims=True)
        acc = (jnp.dot(p_c.astype(BF16), vc, preferred_element_type=F32)
               + jnp.dot(p_n.astype(BF16), vn_ref[0, h], preferred_element_type=F32))
        o_ref[:, h * hd:(h + 1) * hd] = (acc / l).astype(o_ref.dtype)


def _fox_sample(q, k_new, v_new, cache_k, cache_v, c_row, *, layer):
    batch, n_heads, new, hd = q.shape
    past = cache_k.shape[1] // n_heads
    new_spec = pl.BlockSpec((1, n_heads, new, hd), lambda b: (b, 0, 0, 0))
    cache_spec = pl.BlockSpec((1, past * n_heads, hd), lambda b: (layer * batch + b, 0, 0))
    return pl.pallas_call(
        functools.partial(_fox_sample_kernel, past=past, scale=hd ** -0.5),
        grid=(batch,),
        in_specs=[new_spec, new_spec, new_spec, cache_spec, cache_spec,
                  pl.BlockSpec((1, n_heads, c_row.shape[2]), lambda b: (b, 0, 0))],
        out_specs=pl.BlockSpec((new, n_heads * hd), lambda b: (b, 0)),
        out_shape=jax.ShapeDtypeStruct((batch * new, n_heads * hd), BF16),
        compiler_params=_params(1),
    )(q, k_new, v_new, cache_k, cache_v, c_row)


def _retention_kernel(q_ref, k_ref, v_ref, g_ref, s0_ref, gn_ref, o_ref, s_out_ref,
                      s_ref, decay_ref, inter_ref, upd_ref, *, log_gammas):
    b = pl.program_id(0)
    c = pl.program_id(1)
    n_heads, chunk, hd = q_ref.shape[1:]

    @pl.when((b == 0) & (c == 0))
    def _():
        i = lax.broadcasted_iota(jnp.int32, (chunk, chunk), 0)
        jj = lax.broadcasted_iota(jnp.int32, (chunk, chunk), 1)
        diff = (i - jj).astype(F32)
        pos = lax.broadcasted_iota(jnp.int32, (chunk, hd), 0).astype(F32)
        for h, lg in enumerate(log_gammas):
            decay_ref[h] = jnp.where(diff >= 0.0, jnp.exp(jnp.maximum(diff, 0.0) * lg), 0.0)
            inter_ref[h] = jnp.exp((pos + 1.0) * lg)
            upd_ref[h] = jnp.exp((chunk - 1.0 - pos) * lg)

    @pl.when(c == 0)
    def _():
        s_ref[...] = s0_ref[0]

    for h, lg in enumerate(log_gammas):
        q = q_ref[0, h]
        k = k_ref[0, h]
        v = v_ref[0, h]
        s_prev = s_ref[h]
        scores = lax.dot_general(q, k, NT_DIMS, preferred_element_type=F32) * decay_ref[h]
        o = jnp.dot(scores.astype(BF16), v, preferred_element_type=F32)
        o = o + jnp.dot(q, s_prev.astype(BF16), preferred_element_type=F32) * inter_ref[h]
        kd = (k.astype(F32) * upd_ref[h]).astype(BF16)
        s_ref[h] = math.exp(chunk * lg) * s_prev + lax.dot_general(kd, v, TN_DIMS, preferred_element_type=F32)
        g = g_ref[0, h].astype(F32)
        y = _rms(o) * gn_ref[h:h + 1, :]
        o_ref[:, h * hd:(h + 1) * hd] = (y * (g * _sigmoid(g))).astype(o_ref.dtype)

    @pl.when(c == pl.num_programs(1) - 1)
    def _():
        s_out_ref[0] = s_ref[...]


def _retention(q, k, v, g, s0, gn, *, s0_block0):
    batch, n_heads, seq, hd = q.shape
    chunk = min(256, seq)
    nc = seq // chunk
    log_gammas = tuple(math.log(1.0 - 2.0 ** (-5.0 - h)) for h in range(n_heads))
    qkv_spec = pl.BlockSpec((1, n_heads, chunk, hd), lambda b, c: (b, 0, c, 0))
    return pl.pallas_call(
        functools.partial(_retention_kernel, log_gammas=log_gammas),
        grid=(batch, nc),
        in_specs=[qkv_spec] * 4 + [pl.BlockSpec((1, n_heads, hd, hd), lambda b, c: (s0_block0 + b, 0, 0, 0)),
                                   _resident((n_heads, hd), lambda b, c: (0, 0))],
        out_specs=[pl.BlockSpec((chunk, n_heads * hd), lambda b, c: (b * nc + c, 0)),
                   pl.BlockSpec((1, n_heads, hd, hd), lambda b, c: (b, 0, 0, 0))],
        out_shape=[jax.ShapeDtypeStruct((batch * seq, n_heads * hd), BF16),
                   jax.ShapeDtypeStruct((batch, n_heads, hd, hd), F32)],
        scratch_shapes=[pltpu.VMEM((n_heads, hd, hd), F32), pltpu.VMEM((n_heads, chunk, chunk), F32),
                        pltpu.VMEM((n_heads, chunk, hd), F32), pltpu.VMEM((n_heads, chunk, hd), F32)],
        compiler_params=_params(2),
    )(q, k, v, g, s0, gn)


def _outproj_kernel(of_ref, or_ref, x_ref, w_ref, ln2_ref, xm_ref, h2_ref, wb_ref):
    @pl.when(pl.program_id(0) == 0)
    def _():
        wb_ref[...] = w_ref[...].astype(BF16)

    half = of_ref.shape[1]
    y = jnp.dot(of_ref[...], wb_ref[0:half, :], preferred_element_type=F32)
    y = y + jnp.dot(or_ref[...], wb_ref[half:2 * half, :], preferred_element_type=F32)
    xm = x_ref[...] + y
    xm_ref[...] = xm
    h2_ref[...] = (_rms(xm) * ln2_ref[...]).astype(BF16)


def _outproj(o_fox, o_ret, x, w_out, ln2):
    tokens, d = x.shape
    half = o_fox.shape[1]
    tm = min(512, tokens)
    row = lambda i: (i, 0)
    return pl.pallas_call(
        _outproj_kernel,
        grid=(tokens // tm,),
        in_specs=[pl.BlockSpec((tm, half), row), pl.BlockSpec((tm, half), row),
                  pl.BlockSpec((tm, d), row),
                  _resident((2 * half, d), lambda i: (0, 0)), _resident((1, d), lambda i: (0, 0))],
        out_specs=[pl.BlockSpec((tm, d), row), pl.BlockSpec((tm, d), row)],
        out_shape=[jax.ShapeDtypeStruct((tokens, d), F32), jax.ShapeDtypeStruct((tokens, d), BF16)],
        scratch_shapes=[pltpu.VMEM((2 * half, d), BF16)],
        compiler_params=_params(1),
    )(o_fox, o_ret, x, w_out, ln2)


ROW_BLOCK = 8
DOWN_CHUNK = 512
SUB_ROWS = 512
FFN_TOKEN_TILE = 1024
FFN_WIDTH_TILE = 512


def _convffn_kernel(*refs, n_seg, seg, tps, carried):
    if carried:
        (h2_ref, xm_ref, wu_ref, cwa_ref, cwg_ref, cba_ref, cbg_ref, wd_ref,
         out_ref, ta_ref, tg_ref, z_ref, ca_ref, cg_ref) = refs
        hists = (None, None)
        carries = (ca_ref, cg_ref)
    else:
        (h2_ref, xm_ref, wu_ref, cwa_ref, cwg_ref, cba_ref, cbg_ref, wd_ref, ha_ref, hg_ref,
         out_ref, ta_ref, tg_ref, z_ref) = refs
        hists = (ha_ref, hg_ref)
        carries = (None, None)
    i = pl.program_id(0)
    j = pl.program_id(1)
    tf = wu_ref.shape[2] // 2
    d = out_ref.shape[1]
    tm = n_seg * seg
    col0 = pl.multiple_of(j * tf, tf)

    @pl.when(j == 0)
    def _():
        out_ref[...] = xm_ref[...]

    if carried:
        @pl.when(i % tps == 0)
        def _():
            for carry_ref in carries:
                carry_ref[0:2, pl.ds(col0, tf)] = jnp.zeros((2, tf), F32)

    sub = min(tm, SUB_ROWS)
    halves = ((0, cwa_ref, cba_ref, ta_ref, carries[0], hists[0]),
              (tf, cwg_ref, cbg_ref, tg_ref, carries[1], hists[1]))
    ups = [[jnp.dot(h2_ref[r:r + sub, :], wu_ref[0, :, c:c + tf], preferred_element_type=F32)
            for c, *_ in halves] for r in range(0, tm, sub)]

    row = lax.broadcasted_iota(jnp.int32, (ROW_BLOCK, tf), 0)
    taps = [[jnp.broadcast_to(cw_ref[t:t + 1, :], (ROW_BLOCK, tf)) for t in range(3)]
            for _, cw_ref, *_ in halves]
    bias = [jnp.broadcast_to(cb_ref[...], (ROW_BLOCK, tf)) for _, _, cb_ref, *_ in halves]
    prev = [None, None]

    def conv_block(x, r0):
        _, _, _, tail_ref, carry_ref, hist_ref = halves[x]
        s, off = divmod(r0, seg)
        if off == 0:
            if carried:
                h0 = carry_ref[0:1, pl.ds(col0, tf)]
                h1 = carry_ref[1:2, pl.ds(col0, tf)]
            else:
                h0 = hist_ref[s, 0:1, :]
                h1 = hist_ref[s, 1:2, :]
            prev[x] = (jnp.broadcast_to(h1, (ROW_BLOCK, tf)), jnp.where(row == 0, h0, h1))
        u = ups[r0 // sub][x]
        cur = u[r0 % sub:r0 % sub + ROW_BLOCK]
        rot1 = pltpu.roll(cur, 1, 0)
        rot2 = pltpu.roll(cur, 2, 0)
        um1 = jnp.where(row == 0, prev[x][0], rot1)
        um2 = jnp.where(row <= 1, prev[x][1], rot2)
        prev[x] = (rot1, rot2)
        if off == seg - ROW_BLOCK:
            tail_ref[s] = cur[ROW_BLOCK - 2:ROW_BLOCK, :]
        if carried and r0 == tm - ROW_BLOCK:
            carry_ref[0:2, pl.ds(col0, tf)] = cur[ROW_BLOCK - 2:ROW_BLOCK, :]
        return bias[x] + (um2 * taps[x][0] + um1 * taps[x][1] + cur * taps[x][2])

    pack = 2 * ROW_BLOCK
    for r_sub in range(0, tm, sub):
        for r0 in range(r_sub, r_sub + sub, pack):
            a = jnp.concatenate([conv_block(0, r0), conv_block(0, r0 + ROW_BLOCK)], axis=0)
            g = jnp.concatenate([conv_block(1, r0), conv_block(1, r0 + ROW_BLOCK)], axis=0)
            z_ref[r0:r0 + pack, :] = ((g * _sigmoid(g)) * a).astype(BF16)
        z = z_ref[r_sub:r_sub + sub, :]
        for c0 in range(0, d, DOWN_CHUNK):
            out_ref[r_sub:r_sub + sub, c0:c0 + DOWN_CHUNK] += jnp.dot(
                z, wd_ref[:, c0:c0 + DOWN_CHUNK], preferred_element_type=F32)


def _convffn(h2, xm, w_up, conv_w, conv_b, w_down, hist, *, batch, seq):
    tokens, d = xm.shape
    f = w_down.shape[0]
    nf = w_up.shape[0]
    tf = f // nf
    tm = min(FFN_TOKEN_TILE, tokens)
    seg = min(seq, tm)
    n_seg = tm // seg
    tps = seq // seg
    carried = hist is None
    assert not carried or n_seg == 1, "a carried conv state needs one sequence per token tile"
    row = lambda i, j: (i, 0)
    a_col = lambda i, j: (0, j)
    g_col = lambda i, j: (0, nf + j)
    in_specs = [pl.BlockSpec((tm, d), row), _resident((tm, d), row),
                pl.BlockSpec((1, d, 2 * tf), lambda i, j: (j, 0, 0)),
                pl.BlockSpec((3, tf), a_col), pl.BlockSpec((3, tf), g_col),
                pl.BlockSpec((1, tf), a_col), pl.BlockSpec((1, tf), g_col),
                pl.BlockSpec((tf, d), lambda i, j: (j, 0))]
    args = [h2, xm, w_up, conv_w, conv_w, conv_b, conv_b, w_down]
    scratch = [pltpu.VMEM((tm, tf), BF16)]
    if carried:
        scratch += [pltpu.VMEM((8, f), F32), pltpu.VMEM((8, f), F32)]
    else:
        in_specs += [pl.BlockSpec((n_seg, 2, tf), lambda i, j: (i, 0, j)),
                     pl.BlockSpec((n_seg, 2, tf), lambda i, j: (i, 0, nf + j))]
        args += [hist, hist]
    tail_spec = pl.BlockSpec((n_seg, 2, tf), lambda i, j: (i, 0, j))
    tail_shape = jax.ShapeDtypeStruct((batch * tps, 2, f), F32)
    out, tails_a, tails_g = pl.pallas_call(
        functools.partial(_convffn_kernel, n_seg=n_seg, seg=seg, tps=tps, carried=carried),
        grid=(tokens // tm, nf),
        in_specs=in_specs,
        out_specs=[pl.BlockSpec((tm, d), row), tail_spec, tail_spec],
        out_shape=[jax.ShapeDtypeStruct((tokens, d), F32), tail_shape, tail_shape],
        scratch_shapes=scratch,
        compiler_params=_params(2),
    )(*args)
    last = lambda t: t.reshape(batch, tps, 2, f)[:, tps - 1]
    return out, jnp.concatenate([last(tails_a), last(tails_g)], axis=-1)


def _rope_tables(pos, rows):
    half = LANES // 2
    inv = jnp.power(ROPE_BASE, -jnp.arange(half, dtype=F32) / half)
    ang = pos.astype(F32)[:, None] * inv[None, :]
    cos = jnp.cos(ang)
    sin = jnp.sin(ang)
    cos_t = jnp.concatenate([cos, cos], axis=1)
    sin_t = jnp.concatenate([-sin, sin], axis=1)
    reps = rows // pos.shape[0]
    return jnp.tile(cos_t, (reps, 1)), jnp.tile(sin_t, (reps, 1))


def kernel(x_prompt, x_sample, cache_fox_k, cache_fox_v, cache_fox_logf, state_ret, state_conv,
           ln1, w_in, b_f, fox_qn, fox_kn, ret_gn, w_out, ln2, w_up, conv_w, conv_b, w_down):
    depth = ln1.shape[0]
    bp, tp, d = x_prompt.shape
    bs, ts, _ = x_sample.shape
    past = cache_fox_k.shape[2]
    n_heads, hd = cache_fox_k.shape[3], cache_fox_k.shape[4]
    d_head = n_heads * hd
    assert hd == LANES and n_heads % HEADS_PER_STEP == 0 and ret_gn.shape[1:] == (n_heads, hd)

    cos_p, sin_p = _rope_tables(jnp.arange(tp), max(tp, min(512, bp * tp)))
    cos_s, sin_s = _rope_tables(past + jnp.arange(ts), max(ts, min(512, bs * ts)))
    cache_k = cache_fox_k.reshape(depth * bs, past * n_heads, hd)
    cache_v = cache_fox_v.reshape(depth * bs, past * n_heads, hd)
    state_r = state_ret.reshape(depth * bs, n_heads, hd, hd)
    zero_state = jnp.zeros((bp, n_heads, hd, hd), F32)

    xp = x_prompt.reshape(bp * tp, d)
    xs = x_sample.reshape(bs * ts, d)
    st_p, st_s = [], []
    for l in range(depth):
        r0 = N_FOX_KINDS * d_head + n_heads
        n_groups = n_heads // HEADS_PER_STEP
        w_ret = w_in[l][:, r0:]
        w_tiles = _tile_cols([(w_in[l], k * n_groups) for k in range(N_FOX_KINDS)]
                             + [(w_ret, k * n_groups) for k in range(N_RET_KINDS)],
                             HEADS_PER_STEP * hd, n_groups)
        wf = jnp.pad(w_in[l][:, N_FOX_KINDS * d_head:r0], ((0, 0), (0, hd - n_heads))).astype(BF16)
        bfp = jnp.pad(b_f[l], (0, hd - n_heads)).reshape(1, hd)
        qn = fox_qn[l].reshape(1, hd)
        kn = fox_kn[l].reshape(1, hd)
        f = w_down.shape[1]
        nf = f // min(FFN_WIDTH_TILE, f)
        wu = _tile_cols([(w_up[l], 0), (w_up[l], nf)], f // nf, nf)
        wd = w_down[l].astype(BF16)
        cw = conv_w[l]
        cb = conv_b[l].reshape(1, -1)
        ln1_l = ln1[l].reshape(1, d)
        ln2_l = ln2[l].reshape(1, d)

        def group(x, cos_t, sin_t, batch, seq):
            return _inproj(x, ln1_l, w_tiles, wf, bfp, qn, kn, cos_t, sin_t,
                           batch=batch, seq=seq, n_heads=n_heads)

        fq, fkb, fvb, rq, rk, rv, rg, fk, fv, lf = group(xp, cos_p, sin_p, bp, tp)
        c_row = _cumsum(lf.reshape(bp, tp, n_heads).transpose(0, 2, 1))
        o_fox = _fox_prompt(fq, fkb, fvb, c_row, qn, kn)
        o_ret, s_ret_p = _retention(rq, rk, rv, rg, zero_state, ret_gn[l], s0_block0=0)
        xm, h2 = _outproj(o_fox, o_ret, xp, w_out[l], ln2_l)
        xp, conv_p = _convffn(h2, xm, wu, cw, cb, wd, None, batch=bp, seq=tp)
        st_p.append((fk.reshape(bp, tp, n_heads, hd), fv.reshape(bp, tp, n_heads, hd),
                     lf.reshape(bp, tp, n_heads), s_ret_p, conv_p))

        fq, fkb, fvb, rq, rk, rv, rg, fk, fv, lf = group(xs, cos_s, sin_s, bs, ts)
        lf_all = jnp.concatenate([cache_fox_logf[l], lf.reshape(bs, ts, n_heads)], axis=1)
        pad = (-lf_all.shape[1]) % LANES
        lf_all = jnp.pad(lf_all, ((0, 0), (0, pad), (0, 0)))
        c_row = _cumsum(lf_all.transpose(0, 2, 1))
        o_fox = _fox_sample(fq, fkb, fvb, cache_k, cache_v, c_row, layer=l)
        o_ret, s_ret_s = _retention(rq, rk, rv, rg, state_r, ret_gn[l], s0_block0=l * bs)
        xm, h2 = _outproj(o_fox, o_ret, xs, w_out[l], ln2_l)
        xs, conv_s = _convffn(h2, xm, wu, cw, cb, wd, state_conv[l], batch=bs, seq=ts)
        st_s.append((fk.reshape(bs, ts, n_heads, hd), fv.reshape(bs, ts, n_heads, hd),
                     lf.reshape(bs, ts, n_heads), s_ret_s, conv_s))

    stack = lambda st, k: jnp.stack([s[k] for s in st])
    return (xp.reshape(bp, tp, d), xs.reshape(bs, ts, d),
            stack(st_p, 0), stack(st_p, 1), stack(st_p, 2), stack(st_p, 3), stack(st_p, 4),
            stack(st_s, 0), stack(st_s, 1), stack(st_s, 2), stack(st_s, 3), stack(st_s, 4))
```

```python
import functools
import math

import jax
import jax.numpy as jnp
from jax import lax
from jax.experimental import pallas as pl
from jax.experimental.pallas import tpu as pltpu

EPS = 1e-6
ROPE_BASE = 10000.0
MASKED_LOGIT = -1e30
LANES = 128
VMEM_LIMIT_BYTES = 56 * 1024 * 1024
WEIGHT_BAND_ROWS = 256
LOG2E = math.log2(math.e)
FOX_BOUNDED_LIMIT = 100.0
F32_EXP2_UNDERFLOW = 152.0

F32 = jnp.float32
BF16 = jnp.bfloat16

N_FOX_KINDS = 3
N_RET_KINDS = 4
HEADS_PER_STEP = 2
FOX_HEADS_PER_STEP = 2

NT_DIMS = (((1,), (1,)), ((), ()))
TN_DIMS = (((0,), (0,)), ((), ()))


def _params(n_axes):
    return pltpu.CompilerParams(dimension_semantics=("arbitrary",) * n_axes,
                                vmem_limit_bytes=VMEM_LIMIT_BYTES)


def _resident(block_shape, index_map):
    return pl.BlockSpec(block_shape, index_map, pipeline_mode=pl.Buffered(1))


def _rms(x):
    return x * lax.rsqrt(jnp.mean(x * x, axis=-1, keepdims=True) + EPS)


def _sigmoid(x):
    return 1.0 / (1.0 + jnp.exp(-x))


def _tile_cols_kernel(src_ref, out_ref, *, tiles):
    for t, ranges in enumerate(tiles):
        c0 = 0
        for start, width in ranges:
            out_ref[t, :, c0:c0 + width] = src_ref[0, :, start:start + width].astype(BF16)
            c0 += width


def _tile_cols(w, layer, tiles):
    _, d, cols = w.shape
    width = sum(n for _, n in tiles[0])
    assert all(sum(n for _, n in ranges) == width for ranges in tiles)
    rows = min(WEIGHT_BAND_ROWS, d)
    return pl.pallas_call(
        functools.partial(_tile_cols_kernel, tiles=tiles),
        grid=(d // rows,),
        in_specs=[pl.BlockSpec((1, rows, cols), lambda r: (layer, r, 0))],
        out_specs=pl.BlockSpec((len(tiles), rows, width), lambda r: (0, r, 0)),
        out_shape=jax.ShapeDtypeStruct((len(tiles), d, width), BF16),
        compiler_params=_params(1),
    )(w)


def _inproj_kernel(x_ref, ln1_ref, w_ref, wf_ref, bf_ref, qn_ref, kn_ref, cos_ref, sin_ref,
                   fq_ref, fkb_ref, fvb_ref, rq_ref, rk_ref, rv_ref, rg_ref, fk_ref, fv_ref, logf_ref,
                   h_ref, *, nb, seg, n_fox):
    j = pl.program_id(1)

    @pl.when(j == 0)
    def _():
        h = (_rms(x_ref[...]) * ln1_ref[...]).astype(BF16)
        h_ref[...] = h
        z = jnp.dot(h, wf_ref[...], preferred_element_type=F32) + bf_ref[...]
        logf = jnp.minimum(z, 0.0) - jnp.log1p(jnp.exp(-jnp.abs(z)))
        logf_ref[...] = logf[:, :n_fox]

    cos = cos_ref[...]
    sin = sin_ref[...]
    hd = LANES

    def project(kind):
        c0 = kind * HEADS_PER_STEP * hd
        p = jnp.dot(h_ref[...], w_ref[0, :, c0:c0 + HEADS_PER_STEP * hd],
                    preferred_element_type=F32)
        return [p[:, e * hd:(e + 1) * hd] for e in range(HEADS_PER_STEP)]

    def head_major(a):
        return a.reshape(nb, seg, hd).astype(BF16)

    def rope(a):
        return a * cos + pltpu.roll(a, hd // 2, 1) * sin

    for e, a in enumerate(project(0)):
        fq_ref[:, e] = head_major(_rms(a) * qn_ref[...])
    for e, a in enumerate(project(1)):
        fk = _rms(a) * kn_ref[...]
        fk_ref[:, e * hd:(e + 1) * hd] = fk
        fkb_ref[:, e] = head_major(fk)
    for e, a in enumerate(project(2)):
        fv_ref[:, e * hd:(e + 1) * hd] = a
        fvb_ref[:, e] = head_major(a)
    for e, a in enumerate(project(3)):
        rq_ref[:, e] = head_major(rope(a))
    for e, a in enumerate(project(4)):
        rk_ref[:, e] = head_major(rope(a) * (hd ** -0.5))
    for e, a in enumerate(project(5)):
        rv_ref[:, e] = head_major(a)
    for e, a in enumerate(project(6)):
        rg_ref[:, e] = head_major(a)


def _inproj(x, ln1, w_tiles, wf, bfp, qn, kn, cos_t, sin_t, *, batch, seq, n_heads):
    tokens, d = x.shape
    hd = LANES
    tm = min(512, tokens)
    seg = min(seq, tm)
    nb = tm // seg
    tps = seq // seg
    n_groups = n_heads // HEADS_PER_STEP
    gcols = HEADS_PER_STEP * hd
    grid = (tokens // tm, n_groups)

    hm_shape = jax.ShapeDtypeStruct((batch, n_heads, seq, hd), BF16)
    hm_spec = pl.BlockSpec((nb, HEADS_PER_STEP, seg, hd), lambda i, j: (i // tps, j, i % tps, 0))
    tok_shape = jax.ShapeDtypeStruct((tokens, n_heads * hd), F32)
    tok_spec = pl.BlockSpec((tm, gcols), lambda i, j: (i, j))

    return pl.pallas_call(
        functools.partial(_inproj_kernel, nb=nb, seg=seg, n_fox=n_heads),
        grid=grid,
        in_specs=[pl.BlockSpec((tm, d), lambda i, j: (i, 0)),
                  _resident((1, d), lambda i, j: (0, 0))]
                 + [pl.BlockSpec((1, d, w_tiles.shape[2]), lambda i, j: (j, 0, 0)),
                    _resident((d, hd), lambda i, j: (0, 0)),
                    _resident((1, hd), lambda i, j: (0, 0)),
                    _resident((1, hd), lambda i, j: (0, 0)),
                    _resident((1, hd), lambda i, j: (0, 0)),
                    pl.BlockSpec((tm, hd), lambda i, j: (i % tps, 0)),
                    pl.BlockSpec((tm, hd), lambda i, j: (i % tps, 0))],
        out_specs=[hm_spec] * 7 + [tok_spec, tok_spec,
                                   pl.BlockSpec((tm, n_heads), lambda i, j: (i, 0))],
        out_shape=[hm_shape] * 7 + [tok_shape, tok_shape,
                                    jax.ShapeDtypeStruct((tokens, n_heads), F32)],
        scratch_shapes=[pltpu.VMEM((tm, d), BF16)],
        compiler_params=_params(2),
    )(x, ln1, w_tiles, wf, bfp, qn, kn, cos_t, sin_t)


def _cumsum_kernel(lf_ref, c_ref, *, ch):
    total = lf_ref.shape[2]
    r = lax.broadcasted_iota(jnp.int32, (ch, ch), 0)
    c = lax.broadcasted_iota(jnp.int32, (ch, ch), 1)
    upper = (r <= c).astype(F32)
    carry = jnp.zeros((lf_ref.shape[1], 1), F32)
    for k in range(total // ch):
        x = lf_ref[0, :, k * ch:(k + 1) * ch]
        cs = jnp.dot(x, upper, precision=lax.Precision.HIGHEST, preferred_element_type=F32) + carry
        c_ref[0, :, k * ch:(k + 1) * ch] = cs
        carry = cs[:, ch - 1:ch]


def _cumsum(lf_t):
    batch, n_heads, total = lf_t.shape
    ch = 256 if total % 256 == 0 else LANES
    spec = pl.BlockSpec((1, n_heads, total), lambda b: (b, 0, 0))
    return pl.pallas_call(
        functools.partial(_cumsum_kernel, ch=ch),
        grid=(batch,), in_specs=[spec], out_specs=spec,
        out_shape=jax.ShapeDtypeStruct(lf_t.shape, F32),
        compiler_params=_params(1),
    )(lf_t)


def _softmax_step(z, v, m_ref, l_ref, acc_ref):
    tk = z.shape[1]
    m_prev = m_ref[...]
    m_next = jnp.maximum(m_prev, jnp.max(z, axis=1, keepdims=True))
    p = jnp.exp2(z - pltpu.repeat(m_next, tk // LANES, 1))
    alpha = jnp.exp2(m_prev - m_next)
    l_ref[...] = alpha * l_ref[...] + jnp.sum(p, axis=1, keepdims=True)
    acc_ref[...] = alpha * acc_ref[...] + jnp.dot(p.astype(BF16), v, preferred_element_type=F32)
    m_ref[...] = m_next


def _fox_prompt_kernel(first_ref, q_ref, k_ref, v_ref, c_ref, o_ref, m_ref, l_ref, acc_ref, *, tq, scale):
    b, hg, qi = pl.program_id(0), pl.program_id(1), pl.program_id(2)
    group, hd = q_ref.shape[1], q_ref.shape[3]
    m_ref[...] = jnp.full(m_ref.shape, MASKED_LOGIT, F32)
    l_ref[...] = jnp.zeros(l_ref.shape, F32)
    acc_ref[...] = jnp.zeros(acc_ref.shape, F32)

    def logits(e, k0):
        k = k_ref[0, e, pl.ds(k0, tq), :]
        s = lax.dot_general(q_ref[0, e], k, NT_DIMS, preferred_element_type=F32)
        return s * (scale * LOG2E) - c_ref[0, e, :, pl.ds(k0, tq)] * LOG2E

    def body(j, carry):
        k0 = pl.multiple_of(j * tq, tq)
        for e in range(group):
            _softmax_step(logits(e, k0), v_ref[0, e, pl.ds(k0, tq), :],
                          m_ref.at[e], l_ref.at[e], acc_ref.at[e])
        return carry

    first = first_ref[(b * pl.num_programs(1) + hg) * pl.num_programs(2) + qi]
    lax.fori_loop(first, qi, body, 0)

    k0 = pl.multiple_of(qi * tq, tq)
    row = lax.broadcasted_iota(jnp.int32, (tq, tq), 0)
    col = lax.broadcasted_iota(jnp.int32, (tq, tq), 1)
    for e in range(group):
        z = jnp.where(col <= row, logits(e, k0), MASKED_LOGIT)
        _softmax_step(z, v_ref[0, e, pl.ds(k0, tq), :], m_ref.at[e], l_ref.at[e], acc_ref.at[e])
        o_ref[:, e * hd:(e + 1) * hd] = (acc_ref[e] / l_ref[e]).astype(o_ref.dtype)


def _fox_prompt_bounded_kernel(first_ref, q_ref, k_ref, v_ref, c2_ref, r2_ref, o_ref, l_ref, acc_ref, *, tq, scale):
    b, hg, qi = pl.program_id(0), pl.program_id(1), pl.program_id(2)
    group, hd = q_ref.shape[1], q_ref.shape[3]
    l_ref[...] = jnp.zeros(l_ref.shape, F32)
    acc_ref[...] = jnp.zeros(acc_ref.shape, F32)
    q0 = pl.multiple_of(qi * tq, tq)
    row = lax.broadcasted_iota(jnp.int32, (tq, tq), 0)
    col = lax.broadcasted_iota(jnp.int32, (tq, tq), 1)
    row_terms = [jnp.sum(jnp.where(row == col, r2_ref[0, e, :, pl.ds(q0, tq)], 0.0), axis=1, keepdims=True)
                 for e in range(group)]

    def step(e, k0, masked):
        k = k_ref[0, e, pl.ds(k0, tq), :]
        s = lax.dot_general(q_ref[0, e], k, NT_DIMS, preferred_element_type=F32)
        z = (s * (scale * LOG2E) - c2_ref[0, e, :, pl.ds(k0, tq)]) + row_terms[e]
        if masked:
            z = jnp.where(col <= row, z, MASKED_LOGIT)
        p = jnp.exp2(z)
        part = p[:, 0:LANES]
        for c0 in range(LANES, tq, LANES):
            part = part + p[:, c0:c0 + LANES]
        l_ref[e] += part
        acc_ref[e] += jnp.dot(p.astype(BF16), v_ref[0, e, pl.ds(k0, tq), :], preferred_element_type=F32)

    def body(j, carry):
        k0 = pl.multiple_of(j * tq, tq)
        for e in range(group):
            step(e, k0, False)
        return carry

    first = first_ref[(b * pl.num_programs(1) + hg) * pl.num_programs(2) + qi]
    lax.fori_loop(first, qi, body, 0)
    for e in range(group):
        step(e, q0, True)
        l = jnp.sum(l_ref[e], axis=1, keepdims=True)
        o_ref[:, e * hd:(e + 1) * hd] = (acc_ref[e] / l).astype(o_ref.dtype)


def _qk_logit_bound(qn, kn, hd, scale):
    bf16_slack = (1.0 + 2.0 ** -8) ** 2
    return (hd * scale * LOG2E * bf16_slack) * jnp.max(jnp.abs(qn)) * jnp.max(jnp.abs(kn))


def _first_key_tiles(c_row, bound, *, tq, group):
    batch, n_heads, seq = c_row.shape
    c_first_row = c_row[:, :, 0::tq]
    c_last_key = c_row[:, :, tq - 1::tq]
    gap = (c_last_key[:, :, None, :] - c_first_row[:, :, :, None]) * LOG2E
    nq = seq // tq
    earlier = jnp.arange(nq)[None, :] < jnp.arange(nq)[:, None]
    dead = (gap > 2.0 * bound + F32_EXP2_UNDERFLOW) & earlier
    first = jnp.sum(dead, axis=-1).astype(jnp.int32)
    return jnp.min(first.reshape(batch, n_heads // group, group, nq), axis=2).reshape(-1)


def _fox_prompt(q, k, v, c_row, qn, kn):
    batch, n_heads, seq, hd = q.shape
    tq = min(512, seq)
    nq = seq // tq
    group = FOX_HEADS_PER_STEP
    scale = hd ** -0.5
    bound = _qk_logit_bound(qn, kn, hd, scale)
    first = _first_key_tiles(c_row, bound, tq=tq, group=group)
    kv_spec = pl.BlockSpec((1, group, seq, hd), lambda b, h, i, first: (b, h, 0, 0))
    row_spec = pl.BlockSpec((1, group, 1, seq), lambda b, h, i, first: (b, h, 0, 0))
    q_spec = pl.BlockSpec((1, group, tq, hd), lambda b, h, i, first: (b, h, i, 0))
    out_spec = pl.BlockSpec((tq, group * hd), lambda b, h, i, first: (b * nq + i, h))
    out_shape = jax.ShapeDtypeStruct((batch * seq, n_heads * hd), BF16)
    stats = pltpu.VMEM((group, tq, LANES), F32)
    acc = pltpu.VMEM((group, tq, hd), F32)

    def running_max(c4):
        return pl.pallas_call(
            functools.partial(_fox_prompt_kernel, tq=tq, scale=scale),
            grid_spec=pltpu.PrefetchScalarGridSpec(
                num_scalar_prefetch=1, grid=(batch, n_heads // group, nq),
                in_specs=[q_spec, kv_spec, kv_spec, row_spec], out_specs=out_spec,
                scratch_shapes=[stats, stats, acc]),
            out_shape=out_shape, compiler_params=_params(3),
        )(first, q, k, v, c4)

    def bounded(c4):
        return pl.pallas_call(
            functools.partial(_fox_prompt_bounded_kernel, tq=tq, scale=scale),
            grid_spec=pltpu.PrefetchScalarGridSpec(
                num_scalar_prefetch=1, grid=(batch, n_heads // group, nq),
                in_specs=[q_spec, kv_spec, kv_spec, row_spec, row_spec], out_specs=out_spec,
                scratch_shapes=[stats, acc]),
            out_shape=out_shape, compiler_params=_params(3),
        )(first, q, k, v, c4 * LOG2E, c4 * LOG2E - bound)

    return lax.cond(2.0 * bound <= FOX_BOUNDED_LIMIT, bounded, running_max,
                    c_row.reshape(batch, n_heads, 1, seq))


def _fox_sample_kernel(q_ref, kn_ref, vn_ref, kc_ref, vc_ref, c_ref, o_ref, *, past, scale):
    n_heads, new, hd = q_ref.shape[1:]
    row = lax.broadcasted_iota(jnp.int32, (new, new), 0)
    col = lax.broadcasted_iota(jnp.int32, (new, new), 1)
    for h in range(n_heads):
        q = q_ref[0, h]
        kc = kc_ref[0, pl.ds(h, past, stride=n_heads), :].astype(BF16)
        vc = vc_ref[0, pl.ds(h, past, stride=n_heads), :].astype(BF16)
        z_c = (lax.dot_general(q, kc, NT_DIMS, preferred_element_type=F32) * scale
               - c_ref[0, h:h + 1, 0:past])
        z_n = (lax.dot_general(q, kn_ref[0, h], NT_DIMS, preferred_element_type=F32) * scale
               - c_ref[0, h:h + 1, past:past + new])
        z_n = jnp.where(col <= row, z_n, MASKED_LOGIT)
        m = jnp.maximum(jnp.max(z_c, axis=1, keepdims=True), jnp.max(z_n, axis=1, keepdims=True))
        p_c = jnp.exp(z_c - m)
        p_n = jnp.exp(z_n - m)
        l = jnp.sum(p_c, axis=1, keepdims=True) + jnp.sum(p_n, axis=1, keepdims=True)
        acc = (jnp.dot(p_c.astype(BF16), vc, preferred_element_type=F32)
               + jnp.dot(p_n.astype(BF16), vn_ref[0, h], preferred_element_type=F32))
        o_ref[:, h * hd:(h + 1) * hd] = (acc / l).astype(o_ref.dtype)


def _fox_sample(q, k_new, v_new, cache_k, cache_v, c_row, *, layer):
    batch, n_heads, new, hd = q.shape
    past = cache_k.shape[1] // n_heads
    new_spec = pl.BlockSpec((1, n_heads, new, hd), lambda b: (b, 0, 0, 0))
    cache_spec = pl.BlockSpec((1, past * n_heads, hd), lambda b: (layer * batch + b, 0, 0))
    return pl.pallas_call(
        functools.partial(_fox_sample_kernel, past=past, scale=hd ** -0.5),
        grid=(batch,),
        in_specs=[new_spec, new_spec, new_spec, cache_spec, cache_spec,
                  pl.BlockSpec((1, n_heads, c_row.shape[2]), lambda b: (b, 0, 0))],
        out_specs=pl.BlockSpec((new, n_heads * hd), lambda b: (b, 0)),
        out_shape=jax.ShapeDtypeStruct((batch * new, n_heads * hd), BF16),
        compiler_params=_params(1),
    )(q, k_new, v_new, cache_k, cache_v, c_row)


def _retention_kernel(q_ref, k_ref, v_ref, g_ref, s0_ref, gn_ref, o_ref, s_out_ref,
                      s_ref, decay_ref, inter_ref, upd_ref, *, log_gammas):
    b = pl.program_id(0)
    c = pl.program_id(1)
    n_heads, chunk, hd = q_ref.shape[1:]

    @pl.when((b == 0) & (c == 0))
    def _():
        i = lax.broadcasted_iota(jnp.int32, (chunk, chunk), 0)
        jj = lax.broadcasted_iota(jnp.int32, (chunk, chunk), 1)
        diff = (i - jj).astype(F32)
        pos = lax.broadcasted_iota(jnp.int32, (chunk, hd), 0).astype(F32)
        for h, lg in enumerate(log_gammas):
            decay_ref[h] = jnp.where(diff >= 0.0, jnp.exp(jnp.maximum(diff, 0.0) * lg), 0.0)
            inter_ref[h] = jnp.exp((pos + 1.0) * lg)
            upd_ref[h] = jnp.exp((chunk - 1.0 - pos) * lg)

    @pl.when(c == 0)
    def _():
        s_ref[...] = s0_ref[0]

    for h, lg in enumerate(log_gammas):
        q = q_ref[0, h]
        k = k_ref[0, h]
        v = v_ref[0, h]
        s_prev = s_ref[h]
        scores = lax.dot_general(q, k, NT_DIMS, preferred_element_type=F32) * decay_ref[h]
        o = jnp.dot(scores.astype(BF16), v, preferred_element_type=F32)
        o = o + jnp.dot(q, s_prev.astype(BF16), preferred_element_type=F32) * inter_ref[h]
        kd = (k.astype(F32) * upd_ref[h]).astype(BF16)
        s_ref[h] = math.exp(chunk * lg) * s_prev + lax.dot_general(kd, v, TN_DIMS, preferred_element_type=F32)
        g = g_ref[0, h].astype(F32)
        y = _rms(o) * gn_ref[h:h + 1, :]
        o_ref[:, h * hd:(h + 1) * hd] = (y * (g * _sigmoid(g))).astype(o_ref.dtype)

    @pl.when(c == pl.num_programs(1) - 1)
    def _():
        s_out_ref[0] = s_ref[...]


def _retention(q, k, v, g, s0, gn, *, s0_block0):
    batch, n_heads, seq, hd = q.shape
    chunk = min(256, seq)
    nc = seq // chunk
    log_gammas = tuple(math.log(1.0 - 2.0 ** (-5.0 - h)) for h in range(n_heads))
    qkv_spec = pl.BlockSpec((1, n_heads, chunk, hd), lambda b, c: (b, 0, c, 0))
    return pl.pallas_call(
        functools.partial(_retention_kernel, log_gammas=log_gammas),
        grid=(batch, nc),
        in_specs=[qkv_spec] * 4 + [pl.BlockSpec((1, n_heads, hd, hd), lambda b, c: (s0_block0 + b, 0, 0, 0)),
                                   _resident((n_heads, hd), lambda b, c: (0, 0))],
        out_specs=[pl.BlockSpec((chunk, n_heads * hd), lambda b, c: (b * nc + c, 0)),
                   pl.BlockSpec((1, n_heads, hd, hd), lambda b, c: (b, 0, 0, 0))],
        out_shape=[jax.ShapeDtypeStruct((batch * seq, n_heads * hd), BF16),
                   jax.ShapeDtypeStruct((batch, n_heads, hd, hd), F32)],
        scratch_shapes=[pltpu.VMEM((n_heads, hd, hd), F32), pltpu.VMEM((n_heads, chunk, chunk), F32),
                        pltpu.VMEM((n_heads, chunk, hd), F32), pltpu.VMEM((n_heads, chunk, hd), F32)],
        compiler_params=_params(2),
    )(q, k, v, g, s0, gn)


def _outproj_kernel(of_ref, or_ref, x_ref, w_ref, ln2_ref, xm_ref, h2_ref, wb_ref):
    @pl.when(pl.program_id(0) == 0)
    def _():
        wb_ref[...] = w_ref[...].astype(BF16)

    half = of_ref.shape[1]
    y = jnp.dot(of_ref[...], wb_ref[0:half, :], preferred_element_type=F32)
    y = y + jnp.dot(or_ref[...], wb_ref[half:2 * half, :], preferred_element_type=F32)
    xm = x_ref[...] + y
    xm_ref[...] = xm
    h2_ref[...] = (_rms(xm) * ln2_ref[...]).astype(BF16)


def _outproj(o_fox, o_ret, x, w_out, ln2):
    tokens, d = x.shape
    half = o_fox.shape[1]
    tm = min(512, tokens)
    row = lambda i: (i, 0)
    return pl.pallas_call(
        _outproj_kernel,
        grid=(tokens // tm,),
        in_specs=[pl.BlockSpec((tm, half), row), pl.BlockSpec((tm, half), row),
                  pl.BlockSpec((tm, d), row),
                  _resident((2 * half, d), lambda i: (0, 0)), _resident((1, d), lambda i: (0, 0))],
        out_specs=[pl.BlockSpec((tm, d), row), pl.BlockSpec((tm, d), row)],
        out_shape=[jax.ShapeDtypeStruct((tokens, d), F32), jax.ShapeDtypeStruct((tokens, d), BF16)],
        scratch_shapes=[pltpu.VMEM((2 * half, d), BF16)],
        compiler_params=_params(1),
    )(o_fox, o_ret, x, w_out, ln2)


ROW_BLOCK = 8
DOWN_CHUNK = 512
SUB_SPLIT = (1, 1)
FFN_TOKEN_TILE = 1024
FFN_WIDTH_TILE = 512


def _convffn_kernel(*refs, n_seg, seg, tps, carried):
    if carried:
        (h2_ref, xm_ref, wu_ref, cwa_ref, cwg_ref, cba_ref, cbg_ref, wd_ref,
         out_ref, ta_ref, tg_ref, z_ref, ca_ref, cg_ref) = refs
        hists = (None, None)
        carries = (ca_ref, cg_ref)
    else:
        (h2_ref, xm_ref, wu_ref, cwa_ref, cwg_ref, cba_ref, cbg_ref, wd_ref, ha_ref, hg_ref,
         out_ref, ta_ref, tg_ref, z_ref) = refs
        hists = (ha_ref, hg_ref)
        carries = (None, None)
    i = pl.program_id(0)
    j = pl.program_id(1)
    tf = wu_ref.shape[2] // 2
    d = out_ref.shape[1]
    tm = n_seg * seg
    col0 = pl.multiple_of(j * tf, tf)

    @pl.when(j == 0)
    def _():
        out_ref[...] = xm_ref[...]

    if carried:
        @pl.when(i % tps == 0)
        def _():
            for carry_ref in carries:
                carry_ref[0:2, pl.ds(col0, tf)] = jnp.zeros((2, tf), F32)

    unit = tm // sum(SUB_SPLIT)
    starts = [unit * sum(SUB_SPLIT[:n]) for n in range(len(SUB_SPLIT) + 1)]
    halves = ((0, cwa_ref, cba_ref, ta_ref, carries[0], hists[0]),
              (tf, cwg_ref, cbg_ref, tg_ref, carries[1], hists[1]))
    ups = [[jnp.dot(h2_ref[lo:hi, :], wu_ref[0, :, c:c + tf], preferred_element_type=F32)
            for c, *_ in halves] for lo, hi in zip(starts[:-1], starts[1:])]

    row = lax.broadcasted_iota(jnp.int32, (ROW_BLOCK, tf), 0)
    taps = [[jnp.broadcast_to(cw_ref[t:t + 1, :], (ROW_BLOCK, tf)) for t in range(3)]
            for _, cw_ref, *_ in halves]
    bias = [jnp.broadcast_to(cb_ref[...], (ROW_BLOCK, tf)) for _, _, cb_ref, *_ in halves]
    prev = [None, None]

    def conv_block(x, r0):
        _, _, _, tail_ref, carry_ref, hist_ref = halves[x]
        s, off = divmod(r0, seg)
        if off == 0:
            if carried:
                h0 = carry_ref[0:1, pl.ds(col0, tf)]
                h1 = carry_ref[1:2, pl.ds(col0, tf)]
            else:
                h0 = hist_ref[s, 0:1, :]
                h1 = hist_ref[s, 1:2, :]
            prev[x] = (jnp.broadcast_to(h1, (ROW_BLOCK, tf)), jnp.where(row == 0, h0, h1))
        si = max(n for n, lo in enumerate(starts[:-1]) if lo <= r0)
        cur = ups[si][x][r0 - starts[si]:r0 - starts[si] + ROW_BLOCK]
        rot1 = pltpu.roll(cur, 1, 0)
        rot2 = pltpu.roll(cur, 2, 0)
        um1 = jnp.where(row == 0, prev[x][0], rot1)
        um2 = jnp.where(row <= 1, prev[x][1], rot2)
        prev[x] = (rot1, rot2)
        if off == seg - ROW_BLOCK:
            tail_ref[s] = cur[ROW_BLOCK - 2:ROW_BLOCK, :]
        if carried and r0 == tm - ROW_BLOCK:
            carry_ref[0:2, pl.ds(col0, tf)] = cur[ROW_BLOCK - 2:ROW_BLOCK, :]
        return bias[x] + (um2 * taps[x][0] + um1 * taps[x][1] + cur * taps[x][2])

    pack = 2 * ROW_BLOCK
    for lo, hi in zip(starts[:-1], starts[1:]):
        for r0 in range(lo, hi, pack):
            a = jnp.concatenate([conv_block(0, r0), conv_block(0, r0 + ROW_BLOCK)], axis=0)
            g = jnp.concatenate([conv_block(1, r0), conv_block(1, r0 + ROW_BLOCK)], axis=0)
            z_ref[r0:r0 + pack, :] = ((g * _sigmoid(g)) * a).astype(BF16)
        z = z_ref[lo:hi, :]
        for c0 in range(0, d, DOWN_CHUNK):
            out_ref[lo:hi, c0:c0 + DOWN_CHUNK] += jnp.dot(
                z, wd_ref[:, c0:c0 + DOWN_CHUNK], preferred_element_type=F32)


def _convffn(h2, xm, w_up, conv_w, conv_b, w_down, hist, *, batch, seq):
    tokens, d = xm.shape
    f = w_down.shape[0]
    nf = w_up.shape[0]
    tf = f // nf
    tm = min(FFN_TOKEN_TILE, tokens)
    seg = min(seq, tm)
    n_seg = tm // seg
    tps = seq // seg
    carried = hist is None
    assert not carried or n_seg == 1, "a carried conv state needs one sequence per token tile"
    row = lambda i, j: (i, 0)
    a_col = lambda i, j: (0, j)
    g_col = lambda i, j: (0, nf + j)
    in_specs = [pl.BlockSpec((tm, d), row), _resident((tm, d), row),
                pl.BlockSpec((1, d, 2 * tf), lambda i, j: (j, 0, 0)),
                pl.BlockSpec((3, tf), a_col), pl.BlockSpec((3, tf), g_col),
                pl.BlockSpec((1, tf), a_col), pl.BlockSpec((1, tf), g_col),
                pl.BlockSpec((tf, d), lambda i, j: (j, 0))]
    args = [h2, xm, w_up, conv_w, conv_w, conv_b, conv_b, w_down]
    scratch = [pltpu.VMEM((tm, tf), BF16)]
    if carried:
        scratch += [pltpu.VMEM((8, f), F32), pltpu.VMEM((8, f), F32)]
    else:
        in_specs += [pl.BlockSpec((n_seg, 2, tf), lambda i, j: (i, 0, j)),
                     pl.BlockSpec((n_seg, 2, tf), lambda i, j: (i, 0, nf + j))]
        args += [hist, hist]
    tail_spec = pl.BlockSpec((n_seg, 2, tf), lambda i, j: (i, 0, j))
    tail_shape = jax.ShapeDtypeStruct((batch * tps, 2, f), F32)
    out, tails_a, tails_g = pl.pallas_call(
        functools.partial(_convffn_kernel, n_seg=n_seg, seg=seg, tps=tps, carried=carried),
        grid=(tokens // tm, nf),
        in_specs=in_specs,
        out_specs=[pl.BlockSpec((tm, d), row), tail_spec, tail_spec],
        out_shape=[jax.ShapeDtypeStruct((tokens, d), F32), tail_shape, tail_shape],
        scratch_shapes=scratch,
        compiler_params=_params(2),
    )(*args)
    last = lambda t: t.reshape(batch, tps, 2, f)[:, tps - 1]
    return out, jnp.concatenate([last(tails_a), last(tails_g)], axis=-1)


def _rope_tables(pos, rows):
    half = LANES // 2
    inv = jnp.power(ROPE_BASE, -jnp.arange(half, dtype=F32) / half)
    ang = pos.astype(F32)[:, None] * inv[None, :]
    cos = jnp.cos(ang)
    sin = jnp.sin(ang)
    cos_t = jnp.concatenate([cos, cos], axis=1)
    sin_t = jnp.concatenate([-sin, sin], axis=1)
    reps = rows // pos.shape[0]
    return jnp.tile(cos_t, (reps, 1)), jnp.tile(sin_t, (reps, 1))


def kernel(x_prompt, x_sample, cache_fox_k, cache_fox_v, cache_fox_logf, state_ret, state_conv,
           ln1, w_in, b_f, fox_qn, fox_kn, ret_gn, w_out, ln2, w_up, conv_w, conv_b, w_down):
    depth = ln1.shape[0]
    bp, tp, d = x_prompt.shape
    bs, ts, _ = x_sample.shape
    past = cache_fox_k.shape[2]
    n_heads, hd = cache_fox_k.shape[3], cache_fox_k.shape[4]
    d_head = n_heads * hd
    assert hd == LANES and n_heads % HEADS_PER_STEP == 0 and ret_gn.shape[1:] == (n_heads, hd)

    cos_p, sin_p = _rope_tables(jnp.arange(tp), max(tp, min(512, bp * tp)))
    cos_s, sin_s = _rope_tables(past + jnp.arange(ts), max(ts, min(512, bs * ts)))
    cache_k = cache_fox_k.reshape(depth * bs, past * n_heads, hd)
    cache_v = cache_fox_v.reshape(depth * bs, past * n_heads, hd)
    state_r = state_ret.reshape(depth * bs, n_heads, hd, hd)
    zero_state = jnp.zeros((bp, n_heads, hd, hd), F32)

    xp = x_prompt.reshape(bp * tp, d)
    xs = x_sample.reshape(bs * ts, d)
    st_p, st_s = [], []
    for l in range(depth):
        r0 = N_FOX_KINDS * d_head + n_heads
        n_groups = n_heads // HEADS_PER_STEP
        gcols = HEADS_PER_STEP * hd
        w_tiles = _tile_cols(w_in, l, [[(k * d_head + j * gcols, gcols) for k in range(N_FOX_KINDS)]
                                       + [(r0 + k * d_head + j * gcols, gcols) for k in range(N_RET_KINDS)]
                                       for j in range(n_groups)])
        wf = jnp.pad(w_in[l][:, N_FOX_KINDS * d_head:r0], ((0, 0), (0, hd - n_heads))).astype(BF16)
        bfp = jnp.pad(b_f[l], (0, hd - n_heads)).reshape(1, hd)
        qn = fox_qn[l].reshape(1, hd)
        kn = fox_kn[l].reshape(1, hd)
        f = w_down.shape[1]
        nf = f // min(FFN_WIDTH_TILE, f)
        tf = f // nf
        wu = _tile_cols(w_up, l, [[(j * tf, tf), (f + j * tf, tf)] for j in range(nf)])
        wd = w_down[l].astype(BF16)
        cw = conv_w[l]
        cb = conv_b[l].reshape(1, -1)
        ln1_l = ln1[l].reshape(1, d)
        ln2_l = ln2[l].reshape(1, d)

        def group(x, cos_t, sin_t, batch, seq):
            return _inproj(x, ln1_l, w_tiles, wf, bfp, qn, kn, cos_t, sin_t,
                           batch=batch, seq=seq, n_heads=n_heads)

        fq, fkb, fvb, rq, rk, rv, rg, fk, fv, lf = group(xp, cos_p, sin_p, bp, tp)
        c_row = _cumsum(lf.reshape(bp, tp, n_heads).transpose(0, 2, 1))
        o_fox = _fox_prompt(fq, fkb, fvb, c_row, qn, kn)
        o_ret, s_ret_p = _retention(rq, rk, rv, rg, zero_state, ret_gn[l], s0_block0=0)
        xm, h2 = _outproj(o_fox, o_ret, xp, w_out[l], ln2_l)
        xp, conv_p = _convffn(h2, xm, wu, cw, cb, wd, None, batch=bp, seq=tp)
        st_p.append((fk.reshape(bp, tp, n_heads, hd), fv.reshape(bp, tp, n_heads, hd),
                     lf.reshape(bp, tp, n_heads), s_ret_p, conv_p))

        fq, fkb, fvb, rq, rk, rv, rg, fk, fv, lf = group(xs, cos_s, sin_s, bs, ts)
        lf_all = jnp.concatenate([cache_fox_logf[l], lf.reshape(bs, ts, n_heads)], axis=1)
        pad = (-lf_all.shape[1]) % LANES
        lf_all = jnp.pad(lf_all, ((0, 0), (0, pad), (0, 0)))
        c_row = _cumsum(lf_all.transpose(0, 2, 1))
        o_fox = _fox_sample(fq, fkb, fvb, cache_k, cache_v, c_row, layer=l)
        o_ret, s_ret_s = _retention(rq, rk, rv, rg, state_r, ret_gn[l], s0_block0=l * bs)
        xm, h2 = _outproj(o_fox, o_ret, xs, w_out[l], ln2_l)
        xs, conv_s = _convffn(h2, xm, wu, cw, cb, wd, state_conv[l], batch=bs, seq=ts)
        st_s.append((fk.reshape(bs, ts, n_heads, hd), fv.reshape(bs, ts, n_heads, hd),
                     lf.reshape(bs, ts, n_heads), s_ret_s, conv_s))

    stack = lambda st, k: jnp.stack([s[k] for s in st])
    return (xp.reshape(bp, tp, d), xs.reshape(bs, ts, d),
            stack(st_p, 0), stack(st_p, 1), stack(st_p, 2), stack(st_p, 3), stack(st_p, 4),
            stack(st_s, 0), stack(st_s, 1), stack(st_s, 2), stack(st_s, 3), stack(st_s, 4))
```

```python
import functools
import math

import jax
import jax.numpy as jnp
from jax import lax
from jax.experimental import pallas as pl
from jax.experimental.pallas import tpu as pltpu

EPS = 1e-6
ROPE_BASE = 10000.0
MASKED_LOGIT = -1e30
LANES = 128
VMEM_LIMIT_BYTES = 56 * 1024 * 1024
LOG2E = math.log2(math.e)
FOX_BOUNDED_LIMIT = 100.0
F32_EXP2_UNDERFLOW = 152.0

F32 = jnp.float32
BF16 = jnp.bfloat16

N_FOX_KINDS = 3
N_RET_KINDS = 4
HEADS_PER_STEP = 2
FOX_HEADS_PER_STEP = 4

NT_DIMS = (((1,), (1,)), ((), ()))
TN_DIMS = (((0,), (0,)), ((), ()))


def _params(n_axes):
    return pltpu.CompilerParams(dimension_semantics=("arbitrary",) * n_axes,
                                vmem_limit_bytes=VMEM_LIMIT_BYTES)


def _resident(block_shape, index_map):
    return pl.BlockSpec(block_shape, index_map, pipeline_mode=pl.Buffered(1))


def _rms(x):
    return x * lax.rsqrt(jnp.mean(x * x, axis=-1, keepdims=True) + EPS)


def _sigmoid(x):
    return 1.0 / (1.0 + jnp.exp(-x))


def _inproj_kernel(x_ref, ln1_ref, wfq_ref, wfk_ref, wfv_ref, wrq_ref, wrk_ref, wrv_ref, wrg_ref,
                   wf_ref, bf_ref, qn_ref, kn_ref, cos_ref, sin_ref,
                   fq_ref, fkb_ref, fvb_ref, rq_ref, rk_ref, rv_ref, rg_ref, fk_ref, fv_ref, logf_ref,
                   h_ref, *, nb, seg, n_fox):
    j = pl.program_id(1)

    @pl.when(j == 0)
    def _():
        h = (_rms(x_ref[...]) * ln1_ref[...]).astype(BF16)
        h_ref[...] = h
        z = jnp.dot(h, wf_ref[...], preferred_element_type=F32) + bf_ref[...]
        logf = jnp.minimum(z, 0.0) - jnp.log1p(jnp.exp(-jnp.abs(z)))
        logf_ref[...] = logf[:, :n_fox]

    cos = cos_ref[...]
    sin = sin_ref[...]
    hd = LANES

    def project(w_ref):
        p = jnp.dot(h_ref[...], w_ref[...], preferred_element_type=F32)
        return [p[:, e * hd:(e + 1) * hd] for e in range(HEADS_PER_STEP)]

    def head_major(a):
        return a.reshape(nb, seg, hd).astype(BF16)

    def rope(a):
        return a * cos + pltpu.roll(a, hd // 2, 1) * sin

    for e, a in enumerate(project(wfq_ref)):
        fq_ref[:, e] = head_major(_rms(a) * qn_ref[...])
    for e, a in enumerate(project(wfk_ref)):
        fk = _rms(a) * kn_ref[...]
        fk_ref[:, e * hd:(e + 1) * hd] = fk
        fkb_ref[:, e] = head_major(fk)
    for e, a in enumerate(project(wfv_ref)):
        fv_ref[:, e * hd:(e + 1) * hd] = a
        fvb_ref[:, e] = head_major(a)
    for e, a in enumerate(project(wrq_ref)):
        rq_ref[:, e] = head_major(rope(a))
    for e, a in enumerate(project(wrk_ref)):
        rk_ref[:, e] = head_major(rope(a) * (hd ** -0.5))
    for e, a in enumerate(project(wrv_ref)):
        rv_ref[:, e] = head_major(a)
    for e, a in enumerate(project(wrg_ref)):
        rg_ref[:, e] = head_major(a)


def _inproj(x, ln1, w_fox, w_ret, wf, bfp, qn, kn, cos_t, sin_t, *, batch, seq, n_heads):
    tokens, d = x.shape
    hd = LANES
    tm = min(512, tokens)
    seg = min(seq, tm)
    nb = tm // seg
    tps = seq // seg
    n_groups = n_heads // HEADS_PER_STEP
    gcols = HEADS_PER_STEP * hd
    grid = (tokens // tm, n_groups)

    def w_spec(kind):
        return pl.BlockSpec((d, gcols), lambda i, j: (0, kind * n_groups + j))

    hm_shape = jax.ShapeDtypeStruct((batch, n_heads, seq, hd), BF16)
    hm_spec = pl.BlockSpec((nb, HEADS_PER_STEP, seg, hd), lambda i, j: (i // tps, j, i % tps, 0))
    tok_shape = jax.ShapeDtypeStruct((tokens, n_heads * hd), F32)
    tok_spec = pl.BlockSpec((tm, gcols), lambda i, j: (i, j))

    return pl.pallas_call(
        functools.partial(_inproj_kernel, nb=nb, seg=seg, n_fox=n_heads),
        grid=grid,
        in_specs=[pl.BlockSpec((tm, d), lambda i, j: (i, 0)),
                  _resident((1, d), lambda i, j: (0, 0))]
                 + [w_spec(k) for k in range(N_FOX_KINDS)] + [w_spec(k) for k in range(N_RET_KINDS)]
                 + [_resident((d, hd), lambda i, j: (0, 0)),
                    _resident((1, hd), lambda i, j: (0, 0)),
                    _resident((1, hd), lambda i, j: (0, 0)),
                    _resident((1, hd), lambda i, j: (0, 0)),
                    pl.BlockSpec((tm, hd), lambda i, j: (i % tps, 0)),
                    pl.BlockSpec((tm, hd), lambda i, j: (i % tps, 0))],
        out_specs=[hm_spec] * 7 + [tok_spec, tok_spec,
                                   pl.BlockSpec((tm, n_heads), lambda i, j: (i, 0))],
        out_shape=[hm_shape] * 7 + [tok_shape, tok_shape,
                                    jax.ShapeDtypeStruct((tokens, n_heads), F32)],
        scratch_shapes=[pltpu.VMEM((tm, d), BF16)],
        compiler_params=_params(2),
    )(x, ln1, w_fox, w_fox, w_fox, w_ret, w_ret, w_ret, w_ret, wf, bfp, qn, kn, cos_t, sin_t)


def _cumsum_kernel(lf_ref, c_ref, *, ch):
    total = lf_ref.shape[2]
    r = lax.broadcasted_iota(jnp.int32, (ch, ch), 0)
    c = lax.broadcasted_iota(jnp.int32, (ch, ch), 1)
    upper = (r <= c).astype(F32)
    carry = jnp.zeros((lf_ref.shape[1], 1), F32)
    for k in range(total // ch):
        x = lf_ref[0, :, k * ch:(k + 1) * ch]
        cs = jnp.dot(x, upper, precision=lax.Precision.HIGHEST, preferred_element_type=F32) + carry
        c_ref[0, :, k * ch:(k + 1) * ch] = cs
        carry = cs[:, ch - 1:ch]


def _cumsum(lf_t):
    batch, n_heads, total = lf_t.shape
    ch = 256 if total % 256 == 0 else LANES
    spec = pl.BlockSpec((1, n_heads, total), lambda b: (b, 0, 0))
    return pl.pallas_call(
        functools.partial(_cumsum_kernel, ch=ch),
        grid=(batch,), in_specs=[spec], out_specs=spec,
        out_shape=jax.ShapeDtypeStruct(lf_t.shape, F32),
        compiler_params=_params(1),
    )(lf_t)


def _softmax_step(z, v, m_ref, l_ref, acc_ref):
    tk = z.shape[1]
    m_prev = m_ref[...]
    m_next = jnp.maximum(m_prev, jnp.max(z, axis=1, keepdims=True))
    p = jnp.exp2(z - pltpu.repeat(m_next, tk // LANES, 1))
    alpha = jnp.exp2(m_prev - m_next)
    l_ref[...] = alpha * l_ref[...] + jnp.sum(p, axis=1, keepdims=True)
    acc_ref[...] = alpha * acc_ref[...] + jnp.dot(p.astype(BF16), v, preferred_element_type=F32)
    m_ref[...] = m_next


def _fox_prompt_kernel(first_ref, q_ref, k_ref, v_ref, c_ref, o_ref, m_ref, l_ref, acc_ref, *, tq, scale):
    b, hg, qi = pl.program_id(0), pl.program_id(1), pl.program_id(2)
    group, hd = q_ref.shape[1], q_ref.shape[3]
    m_ref[...] = jnp.full(m_ref.shape, MASKED_LOGIT, F32)
    l_ref[...] = jnp.zeros(l_ref.shape, F32)
    acc_ref[...] = jnp.zeros(acc_ref.shape, F32)

    def logits(e, k0):
        k = k_ref[0, e, pl.ds(k0, tq), :]
        s = lax.dot_general(q_ref[0, e], k, NT_DIMS, preferred_element_type=F32)
        return s * (scale * LOG2E) - c_ref[0, e, :, pl.ds(k0, tq)] * LOG2E

    def body(j, carry):
        k0 = pl.multiple_of(j * tq, tq)
        for e in range(group):
            _softmax_step(logits(e, k0), v_ref[0, e, pl.ds(k0, tq), :],
                          m_ref.at[e], l_ref.at[e], acc_ref.at[e])
        return carry

    first = first_ref[(b * pl.num_programs(1) + hg) * pl.num_programs(2) + qi]
    lax.fori_loop(first, qi, body, 0)

    k0 = pl.multiple_of(qi * tq, tq)
    row = lax.broadcasted_iota(jnp.int32, (tq, tq), 0)
    col = lax.broadcasted_iota(jnp.int32, (tq, tq), 1)
    for e in range(group):
        z = jnp.where(col <= row, logits(e, k0), MASKED_LOGIT)
        _softmax_step(z, v_ref[0, e, pl.ds(k0, tq), :], m_ref.at[e], l_ref.at[e], acc_ref.at[e])
        o_ref[:, e * hd:(e + 1) * hd] = (acc_ref[e] / l_ref[e]).astype(o_ref.dtype)


def _fox_prompt_bounded_kernel(first_ref, q_ref, k_ref, v_ref, c2_ref, r2_ref, o_ref, l_ref, acc_ref, *, tq, scale):
    b, hg, qi = pl.program_id(0), pl.program_id(1), pl.program_id(2)
    group, hd = q_ref.shape[1], q_ref.shape[3]
    l_ref[...] = jnp.zeros(l_ref.shape, F32)
    acc_ref[...] = jnp.zeros(acc_ref.shape, F32)
    q0 = pl.multiple_of(qi * tq, tq)
    row = lax.broadcasted_iota(jnp.int32, (tq, tq), 0)
    col = lax.broadcasted_iota(jnp.int32, (tq, tq), 1)
    row_terms = [jnp.sum(jnp.where(row == col, r2_ref[0, e, :, pl.ds(q0, tq)], 0.0), axis=1, keepdims=True)
                 for e in range(group)]

    def step(e, k0, masked):
        k = k_ref[0, e, pl.ds(k0, tq), :]
        s = lax.dot_general(q_ref[0, e], k, NT_DIMS, preferred_element_type=F32)
        z = (s * (scale * LOG2E) - c2_ref[0, e, :, pl.ds(k0, tq)]) + row_terms[e]
        if masked:
            z = jnp.where(col <= row, z, MASKED_LOGIT)
        p = jnp.exp2(z)
        part = p[:, 0:LANES]
        for c0 in range(LANES, tq, LANES):
            part = part + p[:, c0:c0 + LANES]
        l_ref[e] += part
        acc_ref[e] += jnp.dot(p.astype(BF16), v_ref[0, e, pl.ds(k0, tq), :], preferred_element_type=F32)

    def body(j, carry):
        k0 = pl.multiple_of(j * tq, tq)
        for e in range(group):
            step(e, k0, False)
        return carry

    first = first_ref[(b * pl.num_programs(1) + hg) * pl.num_programs(2) + qi]
    lax.fori_loop(first, qi, body, 0)
    for e in range(group):
        step(e, q0, True)
        l = jnp.sum(l_ref[e], axis=1, keepdims=True)
        o_ref[:, e * hd:(e + 1) * hd] = (acc_ref[e] / l).astype(o_ref.dtype)


def _qk_logit_bound(qn, kn, hd, scale):
    bf16_slack = (1.0 + 2.0 ** -8) ** 2
    return (hd * scale * LOG2E * bf16_slack) * jnp.max(jnp.abs(qn)) * jnp.max(jnp.abs(kn))


def _first_key_tiles(c_row, bound, *, tq, group):
    batch, n_heads, seq = c_row.shape
    c_first_row = c_row[:, :, 0::tq]
    c_last_key = c_row[:, :, tq - 1::tq]
    gap = (c_last_key[:, :, None, :] - c_first_row[:, :, :, None]) * LOG2E
    nq = seq // tq
    earlier = jnp.arange(nq)[None, :] < jnp.arange(nq)[:, None]
    dead = (gap > 2.0 * bound + F32_EXP2_UNDERFLOW) & earlier
    first = jnp.sum(dead, axis=-1).astype(jnp.int32)
    return jnp.min(first.reshape(batch, n_heads // group, group, nq), axis=2).reshape(-1)


def _fox_prompt(q, k, v, c_row, qn, kn):
    batch, n_heads, seq, hd = q.shape
    tq = min(512, seq)
    nq = seq // tq
    group = FOX_HEADS_PER_STEP
    scale = hd ** -0.5
    bound = _qk_logit_bound(qn, kn, hd, scale)
    first = _first_key_tiles(c_row, bound, tq=tq, group=group)
    kv_spec = pl.BlockSpec((1, group, seq, hd), lambda b, h, i, first: (b, h, 0, 0))
    row_spec = pl.BlockSpec((1, group, 1, seq), lambda b, h, i, first: (b, h, 0, 0))
    q_spec = pl.BlockSpec((1, group, tq, hd), lambda b, h, i, first: (b, h, i, 0))
    out_spec = pl.BlockSpec((tq, group * hd), lambda b, h, i, first: (b * nq + i, h))
    out_shape = jax.ShapeDtypeStruct((batch * seq, n_heads * hd), BF16)
    stats = pltpu.VMEM((group, tq, LANES), F32)
    acc = pltpu.VMEM((group, tq, hd), F32)

    def running_max(c4):
        return pl.pallas_call(
            functools.partial(_fox_prompt_kernel, tq=tq, scale=scale),
            grid_spec=pltpu.PrefetchScalarGridSpec(
                num_scalar_prefetch=1, grid=(batch, n_heads // group, nq),
                in_specs=[q_spec, kv_spec, kv_spec, row_spec], out_specs=out_spec,
                scratch_shapes=[stats, stats, acc]),
            out_shape=out_shape, compiler_params=_params(3),
        )(first, q, k, v, c4)

    def bounded(c4):
        return pl.pallas_call(
            functools.partial(_fox_prompt_bounded_kernel, tq=tq, scale=scale),
            grid_spec=pltpu.PrefetchScalarGridSpec(
                num_scalar_prefetch=1, grid=(batch, n_heads // group, nq),
                in_specs=[q_spec, kv_spec, kv_spec, row_spec, row_spec], out_specs=out_spec,
                scratch_shapes=[stats, acc]),
            out_shape=out_shape, compiler_params=_params(3),
        )(first, q, k, v, c4 * LOG2E, c4 * LOG2E - bound)

    return lax.cond(2.0 * bound <= FOX_BOUNDED_LIMIT, bounded, running_max,
                    c_row.reshape(batch, n_heads, 1, seq))


def _fox_sample_kernel(q_ref, kn_ref, vn_ref, kc_ref, vc_ref, c_ref, o_ref, *, past, scale):
    n_heads, new, hd = q_ref.shape[1:]
    row = lax.broadcasted_iota(jnp.int32, (new, new), 0)
    col = lax.broadcasted_iota(jnp.int32, (new, new), 1)
    for h in range(n_heads):
        q = q_ref[0, h]
        kc = kc_ref[0, pl.ds(h, past, stride=n_heads), :].astype(BF16)
        vc = vc_ref[0, pl.ds(h, past, stride=n_heads), :].astype(BF16)
        z_c = (lax.dot_general(q, kc, NT_DIMS, preferred_element_type=F32) * scale
               - c_ref[0, h:h + 1, 0:past])
        z_n = (lax.dot_general(q, kn_ref[0, h], NT_DIMS, preferred_element_type=F32) * scale
               - c_ref[0, h:h + 1, past:past + new])
        z_n = jnp.where(col <= row, z_n, MASKED_LOGIT)
        m = jnp.maximum(jnp.max(z_c, axis=1, keepdims=True), jnp.max(z_n, axis=1, keepdims=True))
        p_c = jnp.exp(z_c - m)
        p_n = jnp.exp(z_n - m)
        l = jnp.sum(p_c, axis=1, keepdims=True) + jnp.sum(p_n, axis=1, keepdims=True)
        acc = (jnp.dot(p_c.astype(BF16), vc, preferred_element_type=F32)
               + jnp.dot(p_n.astype(BF16), vn_ref[0, h], preferred_element_type=F32))
        o_ref[:, h * hd:(h + 1) * hd] = (acc / l).astype(o_ref.dtype)


def _fox_sample(q, k_new, v_new, cache_k, cache_v, c_row, *, layer):
    batch, n_heads, new, hd = q.shape
    past = cache_k.shape[1] // n_heads
    new_spec = pl.BlockSpec((1, n_heads, new, hd), lambda b: (b, 0, 0, 0))
    cache_spec = pl.BlockSpec((1, past * n_heads, hd), lambda b: (layer * batch + b, 0, 0))
    return pl.pallas_call(
        functools.partial(_fox_sample_kernel, past=past, scale=hd ** -0.5),
        grid=(batch,),
        in_specs=[new_spec, new_spec, new_spec, cache_spec, cache_spec,
                  pl.BlockSpec((1, n_heads, c_row.shape[2]), lambda b: (b, 0, 0))],
        out_specs=pl.BlockSpec((new, n_heads * hd), lambda b: (b, 0)),
        out_shape=jax.ShapeDtypeStruct((batch * new, n_heads * hd), BF16),
        compiler_params=_params(1),
    )(q, k_new, v_new, cache_k, cache_v, c_row)


def _retention_kernel(q_ref, k_ref, v_ref, g_ref, s0_ref, gn_ref, o_ref, s_out_ref,
                      s_ref, decay_ref, inter_ref, upd_ref, *, log_gammas):
    b = pl.program_id(0)
    c = pl.program_id(1)
    n_heads, chunk, hd = q_ref.shape[1:]

    @pl.when((b == 0) & (c == 0))
    def _():
        i = lax.broadcasted_iota(jnp.int32, (chunk, chunk), 0)
        jj = lax.broadcasted_iota(jnp.int32, (chunk, chunk), 1)
        diff = (i - jj).astype(F32)
        pos = lax.broadcasted_iota(jnp.int32, (chunk, hd), 0).astype(F32)
        for h, lg in enumerate(log_gammas):
            decay_ref[h] = jnp.where(diff >= 0.0, jnp.exp(jnp.maximum(diff, 0.0) * lg), 0.0)
            inter_ref[h] = jnp.exp((pos + 1.0) * lg)
            upd_ref[h] = jnp.exp((chunk - 1.0 - pos) * lg)

    @pl.when(c == 0)
    def _():
        s_ref[...] = s0_ref[0]

    for h, lg in enumerate(log_gammas):
        q = q_ref[0, h]
        k = k_ref[0, h]
        v = v_ref[0, h]
        s_prev = s_ref[h]
        scores = lax.dot_general(q, k, NT_DIMS, preferred_element_type=F32) * decay_ref[h]
        o = jnp.dot(scores.astype(BF16), v, preferred_element_type=F32)
        o = o + jnp.dot(q, s_prev.astype(BF16), preferred_element_type=F32) * inter_ref[h]
        kd = (k.astype(F32) * upd_ref[h]).astype(BF16)
        s_ref[h] = math.exp(chunk * lg) * s_prev + lax.dot_general(kd, v, TN_DIMS, preferred_element_type=F32)
        g = g_ref[0, h].astype(F32)
        y = _rms(o) * gn_ref[h:h + 1, :]
        o_ref[:, h * hd:(h + 1) * hd] = (y * (g * _sigmoid(g))).astype(o_ref.dtype)

    @pl.when(c == pl.num_programs(1) - 1)
    def _():
        s_out_ref[0] = s_ref[...]


def _retention(q, k, v, g, s0, gn, *, s0_block0):
    batch, n_heads, seq, hd = q.shape
    chunk = min(256, seq)
    nc = seq // chunk
    log_gammas = tuple(math.log(1.0 - 2.0 ** (-5.0 - h)) for h in range(n_heads))
    qkv_spec = pl.BlockSpec((1, n_heads, chunk, hd), lambda b, c: (b, 0, c, 0))
    return pl.pallas_call(
        functools.partial(_retention_kernel, log_gammas=log_gammas),
        grid=(batch, nc),
        in_specs=[qkv_spec] * 4 + [pl.BlockSpec((1, n_heads, hd, hd), lambda b, c: (s0_block0 + b, 0, 0, 0)),
                                   _resident((n_heads, hd), lambda b, c: (0, 0))],
        out_specs=[pl.BlockSpec((chunk, n_heads * hd), lambda b, c: (b * nc + c, 0)),
                   pl.BlockSpec((1, n_heads, hd, hd), lambda b, c: (b, 0, 0, 0))],
        out_shape=[jax.ShapeDtypeStruct((batch * seq, n_heads * hd), BF16),
                   jax.ShapeDtypeStruct((batch, n_heads, hd, hd), F32)],
        scratch_shapes=[pltpu.VMEM((n_heads, hd, hd), F32), pltpu.VMEM((n_heads, chunk, chunk), F32),
                        pltpu.VMEM((n_heads, chunk, hd), F32), pltpu.VMEM((n_heads, chunk, hd), F32)],
        compiler_params=_params(2),
    )(q, k, v, g, s0, gn)


def _outproj_kernel(of_ref, or_ref, x_ref, w_ref, ln2_ref, xm_ref, h2_ref, wb_ref):
    @pl.when(pl.program_id(0) == 0)
    def _():
        wb_ref[...] = w_ref[...].astype(BF16)

    half = of_ref.shape[1]
    y = jnp.dot(of_ref[...], wb_ref[0:half, :], preferred_element_type=F32)
    y = y + jnp.dot(or_ref[...], wb_ref[half:2 * half, :], preferred_element_type=F32)
    xm = x_ref[...] + y
    xm_ref[...] = xm
    h2_ref[...] = (_rms(xm) * ln2_ref[...]).astype(BF16)


def _outproj(o_fox, o_ret, x, w_out, ln2):
    tokens, d = x.shape
    half = o_fox.shape[1]
    tm = min(512, tokens)
    row = lambda i: (i, 0)
    return pl.pallas_call(
        _outproj_kernel,
        grid=(tokens // tm,),
        in_specs=[pl.BlockSpec((tm, half), row), pl.BlockSpec((tm, half), row),
                  pl.BlockSpec((tm, d), row),
                  _resident((2 * half, d), lambda i: (0, 0)), _resident((1, d), lambda i: (0, 0))],
        out_specs=[pl.BlockSpec((tm, d), row), pl.BlockSpec((tm, d), row)],
        out_shape=[jax.ShapeDtypeStruct((tokens, d), F32), jax.ShapeDtypeStruct((tokens, d), BF16)],
        scratch_shapes=[pltpu.VMEM((2 * half, d), BF16)],
        compiler_params=_params(1),
    )(o_fox, o_ret, x, w_out, ln2)


ROW_BLOCK = 8
DOWN_CHUNK = 512
SUB_SPLIT = (1, 1)
FFN_TOKEN_TILE = 1024
FFN_WIDTH_TILE = 512


def _convffn_kernel(*refs, n_seg, seg, tps, carried):
    if carried:
        (h2_ref, xm_ref, wa_ref, wg_ref, cwa_ref, cwg_ref, cba_ref, cbg_ref, wd_ref,
         out_ref, ta_ref, tg_ref, z_ref, ca_ref, cg_ref) = refs
        hists = (None, None)
        carries = (ca_ref, cg_ref)
    else:
        (h2_ref, xm_ref, wa_ref, wg_ref, cwa_ref, cwg_ref, cba_ref, cbg_ref, wd_ref, ha_ref, hg_ref,
         out_ref, ta_ref, tg_ref, z_ref) = refs
        hists = (ha_ref, hg_ref)
        carries = (None, None)
    i = pl.program_id(0)
    j = pl.program_id(1)
    tf = wa_ref.shape[1]
    d = out_ref.shape[1]
    tm = n_seg * seg
    col0 = pl.multiple_of(j * tf, tf)

    @pl.when(j == 0)
    def _():
        out_ref[...] = xm_ref[...]

    if carried:
        @pl.when(i % tps == 0)
        def _():
            for carry_ref in carries:
                carry_ref[0:2, pl.ds(col0, tf)] = jnp.zeros((2, tf), F32)

    unit = tm // sum(SUB_SPLIT)
    starts = [unit * sum(SUB_SPLIT[:n]) for n in range(len(SUB_SPLIT) + 1)]
    halves = ((wa_ref, cwa_ref, cba_ref, ta_ref, carries[0], hists[0]),
              (wg_ref, cwg_ref, cbg_ref, tg_ref, carries[1], hists[1]))
    ups = [[jnp.dot(h2_ref[lo:hi, :], w_ref[...], preferred_element_type=F32)
            for w_ref, *_ in halves] for lo, hi in zip(starts[:-1], starts[1:])]

    row = lax.broadcasted_iota(jnp.int32, (ROW_BLOCK, tf), 0)
    taps = [[jnp.broadcast_to(cw_ref[t:t + 1, :], (ROW_BLOCK, tf)) for t in range(3)]
            for _, cw_ref, *_ in halves]
    bias = [jnp.broadcast_to(cb_ref[...], (ROW_BLOCK, tf)) for _, _, cb_ref, *_ in halves]
    prev = [None, None]

    def conv_block(x, r0):
        _, _, _, tail_ref, carry_ref, hist_ref = halves[x]
        s, off = divmod(r0, seg)
        if off == 0:
            if carried:
                h0 = carry_ref[0:1, pl.ds(col0, tf)]
                h1 = carry_ref[1:2, pl.ds(col0, tf)]
            else:
                h0 = hist_ref[s, 0:1, :]
                h1 = hist_ref[s, 1:2, :]
            prev[x] = (jnp.broadcast_to(h1, (ROW_BLOCK, tf)), jnp.where(row == 0, h0, h1))
        si = max(n for n, lo in enumerate(starts[:-1]) if lo <= r0)
        cur = ups[si][x][r0 - starts[si]:r0 - starts[si] + ROW_BLOCK]
        rot1 = pltpu.roll(cur, 1, 0)
        rot2 = pltpu.roll(cur, 2, 0)
        um1 = jnp.where(row == 0, prev[x][0], rot1)
        um2 = jnp.where(row <= 1, prev[x][1], rot2)
        prev[x] = (rot1, rot2)
        if off == seg - ROW_BLOCK:
            tail_ref[s] = cur[ROW_BLOCK - 2:ROW_BLOCK, :]
        if carried and r0 == tm - ROW_BLOCK:
            carry_ref[0:2, pl.ds(col0, tf)] = cur[ROW_BLOCK - 2:ROW_BLOCK, :]
        return bias[x] + (um2 * taps[x][0] + um1 * taps[x][1] + cur * taps[x][2])

    pack = 2 * ROW_BLOCK
    for lo, hi in zip(starts[:-1], starts[1:]):
        for r0 in range(lo, hi, pack):
            a = jnp.concatenate([conv_block(0, r0), conv_block(0, r0 + ROW_BLOCK)], axis=0)
            g = jnp.concatenate([conv_block(1, r0), conv_block(1, r0 + ROW_BLOCK)], axis=0)
            z_ref[r0:r0 + pack, :] = ((g * _sigmoid(g)) * a).astype(BF16)
        z = z_ref[lo:hi, :]
        for c0 in range(0, d, DOWN_CHUNK):
            out_ref[lo:hi, c0:c0 + DOWN_CHUNK] += jnp.dot(
                z, wd_ref[:, c0:c0 + DOWN_CHUNK], preferred_element_type=F32)


def _convffn(h2, xm, w_up, conv_w, conv_b, w_down, hist, *, batch, seq):
    tokens, d = xm.shape
    f = w_down.shape[0]
    tf = min(FFN_WIDTH_TILE, f)
    nf = f // tf
    tm = min(FFN_TOKEN_TILE, tokens)
    seg = min(seq, tm)
    n_seg = tm // seg
    tps = seq // seg
    carried = hist is None
    assert not carried or n_seg == 1, "a carried conv state needs one sequence per token tile"
    row = lambda i, j: (i, 0)
    a_col = lambda i, j: (0, j)
    g_col = lambda i, j: (0, nf + j)
    in_specs = [pl.BlockSpec((tm, d), row), _resident((tm, d), row),
                pl.BlockSpec((d, tf), a_col), pl.BlockSpec((d, tf), g_col),
                pl.BlockSpec((3, tf), a_col), pl.BlockSpec((3, tf), g_col),
                pl.BlockSpec((1, tf), a_col), pl.BlockSpec((1, tf), g_col),
                pl.BlockSpec((tf, d), lambda i, j: (j, 0))]
    args = [h2, xm, w_up, w_up, conv_w, conv_w, conv_b, conv_b, w_down]
    scratch = [pltpu.VMEM((tm, tf), BF16)]
    if carried:
        scratch += [pltpu.VMEM((8, f), F32), pltpu.VMEM((8, f), F32)]
    else:
        in_specs += [pl.BlockSpec((n_seg, 2, tf), lambda i, j: (i, 0, j)),
                     pl.BlockSpec((n_seg, 2, tf), lambda i, j: (i, 0, nf + j))]
        args += [hist, hist]
    tail_spec = pl.BlockSpec((n_seg, 2, tf), lambda i, j: (i, 0, j))
    tail_shape = jax.ShapeDtypeStruct((batch * tps, 2, f), F32)
    out, tails_a, tails_g = pl.pallas_call(
        functools.partial(_convffn_kernel, n_seg=n_seg, seg=seg, tps=tps, carried=carried),
        grid=(tokens // tm, nf),
        in_specs=in_specs,
        out_specs=[pl.BlockSpec((tm, d), row), tail_spec, tail_spec],
        out_shape=[jax.ShapeDtypeStruct((tokens, d), F32), tail_shape, tail_shape],
        scratch_shapes=scratch,
        compiler_params=_params(2),
    )(*args)
    last = lambda t: t.reshape(batch, tps, 2, f)[:, tps - 1]
    return out, jnp.concatenate([last(tails_a), last(tails_g)], axis=-1)


def _rope_tables(pos, rows):
    half = LANES // 2
    inv = jnp.power(ROPE_BASE, -jnp.arange(half, dtype=F32) / half)
    ang = pos.astype(F32)[:, None] * inv[None, :]
    cos = jnp.cos(ang)
    sin = jnp.sin(ang)
    cos_t = jnp.concatenate([cos, cos], axis=1)
    sin_t = jnp.concatenate([-sin, sin], axis=1)
    reps = rows // pos.shape[0]
    return jnp.tile(cos_t, (reps, 1)), jnp.tile(sin_t, (reps, 1))


def kernel(x_prompt, x_sample, cache_fox_k, cache_fox_v, cache_fox_logf, state_ret, state_conv,
           ln1, w_in, b_f, fox_qn, fox_kn, ret_gn, w_out, ln2, w_up, conv_w, conv_b, w_down):
    depth = ln1.shape[0]
    bp, tp, d = x_prompt.shape
    bs, ts, _ = x_sample.shape
    past = cache_fox_k.shape[2]
    n_heads, hd = cache_fox_k.shape[3], cache_fox_k.shape[4]
    d_head = n_heads * hd
    assert hd == LANES and n_heads % HEADS_PER_STEP == 0 and ret_gn.shape[1:] == (n_heads, hd)

    cos_p, sin_p = _rope_tables(jnp.arange(tp), max(tp, min(512, bp * tp)))
    cos_s, sin_s = _rope_tables(past + jnp.arange(ts), max(ts, min(512, bs * ts)))
    cache_k = cache_fox_k.reshape(depth * bs, past * n_heads, hd)
    cache_v = cache_fox_v.reshape(depth * bs, past * n_heads, hd)
    state_r = state_ret.reshape(depth * bs, n_heads, hd, hd)
    zero_state = jnp.zeros((bp, n_heads, hd, hd), F32)

    xp = x_prompt.reshape(bp * tp, d)
    xs = x_sample.reshape(bs * ts, d)
    st_p, st_s = [], []
    for l in range(depth):
        r0 = N_FOX_KINDS * d_head + n_heads
        w_fox = w_in[l][:, :N_FOX_KINDS * d_head].astype(BF16)
        w_ret = w_in[l][:, r0:].astype(BF16)
        wf = jnp.pad(w_in[l][:, N_FOX_KINDS * d_head:r0], ((0, 0), (0, hd - n_heads))).astype(BF16)
        bfp = jnp.pad(b_f[l], (0, hd - n_heads)).reshape(1, hd)
        qn = fox_qn[l].reshape(1, hd)
        kn = fox_kn[l].reshape(1, hd)
        wu = w_up[l].astype(BF16)
        wd = w_down[l].astype(BF16)
        cw = conv_w[l]
        cb = conv_b[l].reshape(1, -1)
        ln1_l = ln1[l].reshape(1, d)
        ln2_l = ln2[l].reshape(1, d)

        def group(x, cos_t, sin_t, batch, seq):
            return _inproj(x, ln1_l, w_fox, w_ret, wf, bfp, qn, kn, cos_t, sin_t,
                           batch=batch, seq=seq, n_heads=n_heads)

        fq, fkb, fvb, rq, rk, rv, rg, fk, fv, lf = group(xp, cos_p, sin_p, bp, tp)
        c_row = _cumsum(lf.reshape(bp, tp, n_heads).transpose(0, 2, 1))
        o_fox = _fox_prompt(fq, fkb, fvb, c_row, qn, kn)
        o_ret, s_ret_p = _retention(rq, rk, rv, rg, zero_state, ret_gn[l], s0_block0=0)
        xm, h2 = _outproj(o_fox, o_ret, xp, w_out[l], ln2_l)
        xp, conv_p = _convffn(h2, xm, wu, cw, cb, wd, None, batch=bp, seq=tp)
        st_p.append((fk.reshape(bp, tp, n_heads, hd), fv.reshape(bp, tp, n_heads, hd),
                     lf.reshape(bp, tp, n_heads), s_ret_p, conv_p))

        fq, fkb, fvb, rq, rk, rv, rg, fk, fv, lf = group(xs, cos_s, sin_s, bs, ts)
        lf_all = jnp.concatenate([cache_fox_logf[l], lf.reshape(bs, ts, n_heads)], axis=1)
        pad = (-lf_all.shape[1]) % LANES
        lf_all = jnp.pad(lf_all, ((0, 0), (0, pad), (0, 0)))
        c_row = _cumsum(lf_all.transpose(0, 2, 1))
        o_fox = _fox_sample(fq, fkb, fvb, cache_k, cache_v, c_row, layer=l)
        o_ret, s_ret_s = _retention(rq, rk, rv, rg, state_r, ret_gn[l], s0_block0=l * bs)
        xm, h2 = _outproj(o_fox, o_ret, xs, w_out[l], ln2_l)
        xs, conv_s = _convffn(h2, xm, wu, cw, cb, wd, state_conv[l], batch=bs, seq=ts)
        st_s.append((fk.reshape(bs, ts, n_heads, hd), fv.reshape(bs, ts, n_heads, hd),
                     lf.reshape(bs, ts, n_heads), s_ret_s, conv_s))

    stack = lambda st, k: jnp.stack([s[k] for s in st])
    return (xp.reshape(bp, tp, d), xs.reshape(bs, ts, d),
            stack(st_p, 0), stack(st_p, 1), stack(st_p, 2), stack(st_p, 3), stack(st_p, 4),
            stack(st_s, 0), stack(st_s, 1), stack(st_s, 2), stack(st_s, 3), stack(st_s, 4))
```

```python
import functools
import math

import jax
import jax.numpy as jnp
from jax import lax
from jax.experimental import pallas as pl
from jax.experimental.pallas import tpu as pltpu

EPS = 1e-6
ROPE_BASE = 10000.0
MASKED_LOGIT = -1e30
LANES = 128
VMEM_LIMIT_BYTES = 56 * 1024 * 1024
LOG2E = math.log2(math.e)
FOX_BOUNDED_LIMIT = 100.0
F32_EXP2_UNDERFLOW = 152.0

F32 = jnp.float32
BF16 = jnp.bfloat16

N_FOX_KINDS = 3
N_RET_KINDS = 4
HEADS_PER_STEP = 2
INPROJ_TOKEN_TILE = 1024
FOX_HEADS_PER_STEP = 4

NT_DIMS = (((1,), (1,)), ((), ()))
TN_DIMS = (((0,), (0,)), ((), ()))


def _params(n_axes):
    return pltpu.CompilerParams(dimension_semantics=("arbitrary",) * n_axes,
                                vmem_limit_bytes=VMEM_LIMIT_BYTES)


def _resident(block_shape, index_map):
    return pl.BlockSpec(block_shape, index_map, pipeline_mode=pl.Buffered(1))


def _rms(x):
    return x * lax.rsqrt(jnp.mean(x * x, axis=-1, keepdims=True) + EPS)


def _sigmoid(x):
    return 1.0 / (1.0 + jnp.exp(-x))


def _inproj_kernel(x_ref, ln1_ref, wfq_ref, wfk_ref, wfv_ref, wrq_ref, wrk_ref, wrv_ref, wrg_ref,
                   wf_ref, bf_ref, qn_ref, kn_ref, cos_ref, sin_ref,
                   fq_ref, fkb_ref, fvb_ref, rq_ref, rk_ref, rv_ref, rg_ref, fk_ref, fv_ref, logf_ref,
                   h_ref, *, nb, seg, n_fox):
    j = pl.program_id(1)

    @pl.when(j == 0)
    def _():
        h = (_rms(x_ref[...]) * ln1_ref[...]).astype(BF16)
        h_ref[...] = h
        z = jnp.dot(h, wf_ref[...], preferred_element_type=F32) + bf_ref[...]
        logf = jnp.minimum(z, 0.0) - jnp.log1p(jnp.exp(-jnp.abs(z)))
        logf_ref[...] = logf[:, :n_fox]

    cos = cos_ref[...]
    sin = sin_ref[...]
    hd = LANES

    def project(w_ref):
        p = jnp.dot(h_ref[...], w_ref[...], preferred_element_type=F32)
        return [p[:, e * hd:(e + 1) * hd] for e in range(HEADS_PER_STEP)]

    def head_major(a):
        return a.reshape(nb, seg, hd).astype(BF16)

    def rope(a):
        return a * cos + pltpu.roll(a, hd // 2, 1) * sin

    for e, a in enumerate(project(wfq_ref)):
        fq_ref[:, e] = head_major(_rms(a) * qn_ref[...])
    for e, a in enumerate(project(wfk_ref)):
        fk = _rms(a) * kn_ref[...]
        fk_ref[:, e * hd:(e + 1) * hd] = fk
        fkb_ref[:, e] = head_major(fk)
    for e, a in enumerate(project(wfv_ref)):
        fv_ref[:, e * hd:(e + 1) * hd] = a
        fvb_ref[:, e] = head_major(a)
    for e, a in enumerate(project(wrq_ref)):
        rq_ref[:, e] = head_major(rope(a))
    for e, a in enumerate(project(wrk_ref)):
        rk_ref[:, e] = head_major(rope(a) * (hd ** -0.5))
    for e, a in enumerate(project(wrv_ref)):
        rv_ref[:, e] = head_major(a)
    for e, a in enumerate(project(wrg_ref)):
        rg_ref[:, e] = head_major(a)


def _inproj(x, ln1, w_fox, w_ret, wf, bfp, qn, kn, cos_t, sin_t, *, batch, seq, n_heads):
    tokens, d = x.shape
    hd = LANES
    tm = min(INPROJ_TOKEN_TILE, tokens)
    seg = min(seq, tm)
    nb = tm // seg
    tps = seq // seg
    n_groups = n_heads // HEADS_PER_STEP
    gcols = HEADS_PER_STEP * hd
    grid = (tokens // tm, n_groups)

    def w_spec(kind):
        return pl.BlockSpec((d, gcols), lambda i, j: (0, kind * n_groups + j))

    hm_shape = jax.ShapeDtypeStruct((batch, n_heads, seq, hd), BF16)
    hm_spec = pl.BlockSpec((nb, HEADS_PER_STEP, seg, hd), lambda i, j: (i // tps, j, i % tps, 0))
    tok_shape = jax.ShapeDtypeStruct((tokens, n_heads * hd), F32)
    tok_spec = pl.BlockSpec((tm, gcols), lambda i, j: (i, j))

    return pl.pallas_call(
        functools.partial(_inproj_kernel, nb=nb, seg=seg, n_fox=n_heads),
        grid=grid,
        in_specs=[pl.BlockSpec((tm, d), lambda i, j: (i, 0)),
                  _resident((1, d), lambda i, j: (0, 0))]
                 + [w_spec(k) for k in range(N_FOX_KINDS)] + [w_spec(k) for k in range(N_RET_KINDS)]
                 + [_resident((d, hd), lambda i, j: (0, 0)),
                    _resident((1, hd), lambda i, j: (0, 0)),
                    _resident((1, hd), lambda i, j: (0, 0)),
                    _resident((1, hd), lambda i, j: (0, 0)),
                    pl.BlockSpec((tm, hd), lambda i, j: (i % tps, 0)),
                    pl.BlockSpec((tm, hd), lambda i, j: (i % tps, 0))],
        out_specs=[hm_spec] * 7 + [tok_spec, tok_spec,
                                   pl.BlockSpec((tm, n_heads), lambda i, j: (i, 0))],
        out_shape=[hm_shape] * 7 + [tok_shape, tok_shape,
                                    jax.ShapeDtypeStruct((tokens, n_heads), F32)],
        scratch_shapes=[pltpu.VMEM((tm, d), BF16)],
        compiler_params=_params(2),
    )(x, ln1, w_fox, w_fox, w_fox, w_ret, w_ret, w_ret, w_ret, wf, bfp, qn, kn, cos_t, sin_t)


def _cumsum_kernel(lf_ref, c_ref, *, ch):
    total = lf_ref.shape[2]
    r = lax.broadcasted_iota(jnp.int32, (ch, ch), 0)
    c = lax.broadcasted_iota(jnp.int32, (ch, ch), 1)
    upper = (r <= c).astype(F32)
    carry = jnp.zeros((lf_ref.shape[1], 1), F32)
    for k in range(total // ch):
        x = lf_ref[0, :, k * ch:(k + 1) * ch]
        cs = jnp.dot(x, upper, precision=lax.Precision.HIGHEST, preferred_element_type=F32) + carry
        c_ref[0, :, k * ch:(k + 1) * ch] = cs
        carry = cs[:, ch - 1:ch]


def _cumsum(lf_t):
    batch, n_heads, total = lf_t.shape
    ch = 256 if total % 256 == 0 else LANES
    spec = pl.BlockSpec((1, n_heads, total), lambda b: (b, 0, 0))
    return pl.pallas_call(
        functools.partial(_cumsum_kernel, ch=ch),
        grid=(batch,), in_specs=[spec], out_specs=spec,
        out_shape=jax.ShapeDtypeStruct(lf_t.shape, F32),
        compiler_params=_params(1),
    )(lf_t)


def _softmax_step(z, v, m_ref, l_ref, acc_ref):
    tk = z.shape[1]
    m_prev = m_ref[...]
    m_next = jnp.maximum(m_prev, jnp.max(z, axis=1, keepdims=True))
    p = jnp.exp2(z - pltpu.repeat(m_next, tk // LANES, 1))
    alpha = jnp.exp2(m_prev - m_next)
    l_ref[...] = alpha * l_ref[...] + jnp.sum(p, axis=1, keepdims=True)
    acc_ref[...] = alpha * acc_ref[...] + jnp.dot(p.astype(BF16), v, preferred_element_type=F32)
    m_ref[...] = m_next


def _fox_prompt_kernel(first_ref, q_ref, k_ref, v_ref, c_ref, o_ref, m_ref, l_ref, acc_ref, *, tq, scale):
    b, hg, qi = pl.program_id(0), pl.program_id(1), pl.program_id(2)
    group, hd = q_ref.shape[1], q_ref.shape[3]
    m_ref[...] = jnp.full(m_ref.shape, MASKED_LOGIT, F32)
    l_ref[...] = jnp.zeros(l_ref.shape, F32)
    acc_ref[...] = jnp.zeros(acc_ref.shape, F32)

    def logits(e, k0):
        k = k_ref[0, e, pl.ds(k0, tq), :]
        s = lax.dot_general(q_ref[0, e], k, NT_DIMS, preferred_element_type=F32)
        return s * (scale * LOG2E) - c_ref[0, e, :, pl.ds(k0, tq)] * LOG2E

    def body(j, carry):
        k0 = pl.multiple_of(j * tq, tq)
        for e in range(group):
            _softmax_step(logits(e, k0), v_ref[0, e, pl.ds(k0, tq), :],
                          m_ref.at[e], l_ref.at[e], acc_ref.at[e])
        return carry

    first = first_ref[(b * pl.num_programs(1) + hg) * pl.num_programs(2) + qi]
    lax.fori_loop(first, qi, body, 0)

    k0 = pl.multiple_of(qi * tq, tq)
    row = lax.broadcasted_iota(jnp.int32, (tq, tq), 0)
    col = lax.broadcasted_iota(jnp.int32, (tq, tq), 1)
    for e in range(group):
        z = jnp.where(col <= row, logits(e, k0), MASKED_LOGIT)
        _softmax_step(z, v_ref[0, e, pl.ds(k0, tq), :], m_ref.at[e], l_ref.at[e], acc_ref.at[e])
        o_ref[:, e * hd:(e + 1) * hd] = (acc_ref[e] / l_ref[e]).astype(o_ref.dtype)


def _fox_prompt_bounded_kernel(first_ref, q_ref, k_ref, v_ref, c2_ref, r2_ref, o_ref, l_ref, acc_ref, *, tq, scale):
    b, hg, qi = pl.program_id(0), pl.program_id(1), pl.program_id(2)
    group, hd = q_ref.shape[1], q_ref.shape[3]
    l_ref[...] = jnp.zeros(l_ref.shape, F32)
    acc_ref[...] = jnp.zeros(acc_ref.shape, F32)
    q0 = pl.multiple_of(qi * tq, tq)
    row = lax.broadcasted_iota(jnp.int32, (tq, tq), 0)
    col = lax.broadcasted_iota(jnp.int32, (tq, tq), 1)
    row_terms = [jnp.sum(jnp.where(row == col, r2_ref[0, e, :, pl.ds(q0, tq)], 0.0), axis=1, keepdims=True)
                 for e in range(group)]

    def step(e, k0, masked):
        k = k_ref[0, e, pl.ds(k0, tq), :]
        s = lax.dot_general(q_ref[0, e], k, NT_DIMS, preferred_element_type=F32)
        z = (s * (scale * LOG2E) - c2_ref[0, e, :, pl.ds(k0, tq)]) + row_terms[e]
        if masked:
            z = jnp.where(col <= row, z, MASKED_LOGIT)
        p = jnp.exp2(z)
        part = p[:, 0:LANES]
        for c0 in range(LANES, tq, LANES):
            part = part + p[:, c0:c0 + LANES]
        l_ref[e] += part
        acc_ref[e] += jnp.dot(p.astype(BF16), v_ref[0, e, pl.ds(k0, tq), :], preferred_element_type=F32)

    def body(j, carry):
        k0 = pl.multiple_of(j * tq, tq)
        for e in range(group):
            step(e, k0, False)
        return carry

    first = first_ref[(b * pl.num_programs(1) + hg) * pl.num_programs(2) + qi]
    lax.fori_loop(first, qi, body, 0)
    for e in range(group):
        step(e, q0, True)
        l = jnp.sum(l_ref[e], axis=1, keepdims=True)
        o_ref[:, e * hd:(e + 1) * hd] = (acc_ref[e] / l).astype(o_ref.dtype)


def _qk_logit_bound(qn, kn, hd, scale):
    bf16_slack = (1.0 + 2.0 ** -8) ** 2
    return (hd * scale * LOG2E * bf16_slack) * jnp.max(jnp.abs(qn)) * jnp.max(jnp.abs(kn))


def _first_key_tiles(c_row, bound, *, tq, group):
    batch, n_heads, seq = c_row.shape
    c_first_row = c_row[:, :, 0::tq]
    c_last_key = c_row[:, :, tq - 1::tq]
    gap = (c_last_key[:, :, None, :] - c_first_row[:, :, :, None]) * LOG2E
    nq = seq // tq
    earlier = jnp.arange(nq)[None, :] < jnp.arange(nq)[:, None]
    dead = (gap > 2.0 * bound + F32_EXP2_UNDERFLOW) & earlier
    first = jnp.sum(dead, axis=-1).astype(jnp.int32)
    return jnp.min(first.reshape(batch, n_heads // group, group, nq), axis=2).reshape(-1)


def _fox_prompt(q, k, v, c_row, qn, kn):
    batch, n_heads, seq, hd = q.shape
    tq = min(512, seq)
    nq = seq // tq
    group = FOX_HEADS_PER_STEP
    scale = hd ** -0.5
    bound = _qk_logit_bound(qn, kn, hd, scale)
    first = _first_key_tiles(c_row, bound, tq=tq, group=group)
    kv_spec = pl.BlockSpec((1, group, seq, hd), lambda b, h, i, first: (b, h, 0, 0))
    row_spec = pl.BlockSpec((1, group, 1, seq), lambda b, h, i, first: (b, h, 0, 0))
    q_spec = pl.BlockSpec((1, group, tq, hd), lambda b, h, i, first: (b, h, i, 0))
    out_spec = pl.BlockSpec((tq, group * hd), lambda b, h, i, first: (b * nq + i, h))
    out_shape = jax.ShapeDtypeStruct((batch * seq, n_heads * hd), BF16)
    stats = pltpu.VMEM((group, tq, LANES), F32)
    acc = pltpu.VMEM((group, tq, hd), F32)

    def running_max(c4):
        return pl.pallas_call(
            functools.partial(_fox_prompt_kernel, tq=tq, scale=scale),
            grid_spec=pltpu.PrefetchScalarGridSpec(
                num_scalar_prefetch=1, grid=(batch, n_heads // group, nq),
                in_specs=[q_spec, kv_spec, kv_spec, row_spec], out_specs=out_spec,
                scratch_shapes=[stats, stats, acc]),
            out_shape=out_shape, compiler_params=_params(3),
        )(first, q, k, v, c4)

    def bounded(c4):
        return pl.pallas_call(
            functools.partial(_fox_prompt_bounded_kernel, tq=tq, scale=scale),
            grid_spec=pltpu.PrefetchScalarGridSpec(
                num_scalar_prefetch=1, grid=(batch, n_heads // group, nq),
                in_specs=[q_spec, kv_spec, kv_spec, row_spec, row_spec], out_specs=out_spec,
                scratch_shapes=[stats, acc]),
            out_shape=out_shape, compiler_params=_params(3),
        )(first, q, k, v, c4 * LOG2E, c4 * LOG2E - bound)

    return lax.cond(2.0 * bound <= FOX_BOUNDED_LIMIT, bounded, running_max,
                    c_row.reshape(batch, n_heads, 1, seq))


def _fox_sample_kernel(q_ref, kn_ref, vn_ref, kc_ref, vc_ref, c_ref, o_ref, *, past, scale):
    n_heads, new, hd = q_ref.shape[1:]
    row = lax.broadcasted_iota(jnp.int32, (new, new), 0)
    col = lax.broadcasted_iota(jnp.int32, (new, new), 1)
    for h in range(n_heads):
        q = q_ref[0, h]
        kc = kc_ref[0, pl.ds(h, past, stride=n_heads), :].astype(BF16)
        vc = vc_ref[0, pl.ds(h, past, stride=n_heads), :].astype(BF16)
        z_c = (lax.dot_general(q, kc, NT_DIMS, preferred_element_type=F32) * scale
               - c_ref[0, h:h + 1, 0:past])
        z_n = (lax.dot_general(q, kn_ref[0, h], NT_DIMS, preferred_element_type=F32) * scale
               - c_ref[0, h:h + 1, past:past + new])
        z_n = jnp.where(col <= row, z_n, MASKED_LOGIT)
        m = jnp.maximum(jnp.max(z_c, axis=1, keepdims=True), jnp.max(z_n, axis=1, keepdims=True))
        p_c = jnp.exp(z_c - m)
        p_n = jnp.exp(z_n - m)
        l = jnp.sum(p_c, axis=1, keepdims=True) + jnp.sum(p_n, axis=1, keepdims=True)
        acc = (jnp.dot(p_c.astype(BF16), vc, preferred_element_type=F32)
               + jnp.dot(p_n.astype(BF16), vn_ref[0, h], preferred_element_type=F32))
        o_ref[:, h * hd:(h + 1) * hd] = (acc / l).astype(o_ref.dtype)


def _fox_sample(q, k_new, v_new, cache_k, cache_v, c_row, *, layer):
    batch, n_heads, new, hd = q.shape
    past = cache_k.shape[1] // n_heads
    new_spec = pl.BlockSpec((1, n_heads, new, hd), lambda b: (b, 0, 0, 0))
    cache_spec = pl.BlockSpec((1, past * n_heads, hd), lambda b: (layer * batch + b, 0, 0))
    return pl.pallas_call(
        functools.partial(_fox_sample_kernel, past=past, scale=hd ** -0.5),
        grid=(batch,),
        in_specs=[new_spec, new_spec, new_spec, cache_spec, cache_spec,
                  pl.BlockSpec((1, n_heads, c_row.shape[2]), lambda b: (b, 0, 0))],
        out_specs=pl.BlockSpec((new, n_heads * hd), lambda b: (b, 0)),
        out_shape=jax.ShapeDtypeStruct((batch * new, n_heads * hd), BF16),
        compiler_params=_params(1),
    )(q, k_new, v_new, cache_k, cache_v, c_row)


def _retention_kernel(q_ref, k_ref, v_ref, g_ref, s0_ref, gn_ref, o_ref, s_out_ref,
                      s_ref, decay_ref, inter_ref, upd_ref, *, log_gammas):
    b = pl.program_id(0)
    c = pl.program_id(1)
    n_heads, chunk, hd = q_ref.shape[1:]

    @pl.when((b == 0) & (c == 0))
    def _():
        i = lax.broadcasted_iota(jnp.int32, (chunk, chunk), 0)
        jj = lax.broadcasted_iota(jnp.int32, (chunk, chunk), 1)
        diff = (i - jj).astype(F32)
        pos = lax.broadcasted_iota(jnp.int32, (chunk, hd), 0).astype(F32)
        for h, lg in enumerate(log_gammas):
            decay_ref[h] = jnp.where(diff >= 0.0, jnp.exp(jnp.maximum(diff, 0.0) * lg), 0.0)
            inter_ref[h] = jnp.exp((pos + 1.0) * lg)
            upd_ref[h] = jnp.exp((chunk - 1.0 - pos) * lg)

    @pl.when(c == 0)
    def _():
        s_ref[...] = s0_ref[0]

    for h, lg in enumerate(log_gammas):
        q = q_ref[0, h]
        k = k_ref[0, h]
        v = v_ref[0, h]
        s_prev = s_ref[h]
        scores = lax.dot_general(q, k, NT_DIMS, preferred_element_type=F32) * decay_ref[h]
        o = jnp.dot(scores.astype(BF16), v, preferred_element_type=F32)
        o = o + jnp.dot(q, s_prev.astype(BF16), preferred_element_type=F32) * inter_ref[h]
        kd = (k.astype(F32) * upd_ref[h]).astype(BF16)
        s_ref[h] = math.exp(chunk * lg) * s_prev + lax.dot_general(kd, v, TN_DIMS, preferred_element_type=F32)
        g = g_ref[0, h].astype(F32)
        y = _rms(o) * gn_ref[h:h + 1, :]
        o_ref[:, h * hd:(h + 1) * hd] = (y * (g * _sigmoid(g))).astype(o_ref.dtype)

    @pl.when(c == pl.num_programs(1) - 1)
    def _():
        s_out_ref[0] = s_ref[...]


def _retention(q, k, v, g, s0, gn, *, s0_block0):
    batch, n_heads, seq, hd = q.shape
    chunk = min(256, seq)
    nc = seq // chunk
    log_gammas = tuple(math.log(1.0 - 2.0 ** (-5.0 - h)) for h in range(n_heads))
    qkv_spec = pl.BlockSpec((1, n_heads, chunk, hd), lambda b, c: (b, 0, c, 0))
    return pl.pallas_call(
        functools.partial(_retention_kernel, log_gammas=log_gammas),
        grid=(batch, nc),
        in_specs=[qkv_spec] * 4 + [pl.BlockSpec((1, n_heads, hd, hd), lambda b, c: (s0_block0 + b, 0, 0, 0)),
                                   _resident((n_heads, hd), lambda b, c: (0, 0))],
        out_specs=[pl.BlockSpec((chunk, n_heads * hd), lambda b, c: (b * nc + c, 0)),
                   pl.BlockSpec((1, n_heads, hd, hd), lambda b, c: (b, 0, 0, 0))],
        out_shape=[jax.ShapeDtypeStruct((batch * seq, n_heads * hd), BF16),
                   jax.ShapeDtypeStruct((batch, n_heads, hd, hd), F32)],
        scratch_shapes=[pltpu.VMEM((n_heads, hd, hd), F32), pltpu.VMEM((n_heads, chunk, chunk), F32),
                        pltpu.VMEM((n_heads, chunk, hd), F32), pltpu.VMEM((n_heads, chunk, hd), F32)],
        compiler_params=_params(2),
    )(q, k, v, g, s0, gn)


def _outproj_kernel(of_ref, or_ref, x_ref, w_ref, ln2_ref, xm_ref, h2_ref, wb_ref):
    @pl.when(pl.program_id(0) == 0)
    def _():
        wb_ref[...] = w_ref[...].astype(BF16)

    half = of_ref.shape[1]
    y = jnp.dot(of_ref[...], wb_ref[0:half, :], preferred_element_type=F32)
    y = y + jnp.dot(or_ref[...], wb_ref[half:2 * half, :], preferred_element_type=F32)
    xm = x_ref[...] + y
    xm_ref[...] = xm
    h2_ref[...] = (_rms(xm) * ln2_ref[...]).astype(BF16)


def _outproj(o_fox, o_ret, x, w_out, ln2):
    tokens, d = x.shape
    half = o_fox.shape[1]
    tm = min(512, tokens)
    row = lambda i: (i, 0)
    return pl.pallas_call(
        _outproj_kernel,
        grid=(tokens // tm,),
        in_specs=[pl.BlockSpec((tm, half), row), pl.BlockSpec((tm, half), row),
                  pl.BlockSpec((tm, d), row),
                  _resident((2 * half, d), lambda i: (0, 0)), _resident((1, d), lambda i: (0, 0))],
        out_specs=[pl.BlockSpec((tm, d), row), pl.BlockSpec((tm, d), row)],
        out_shape=[jax.ShapeDtypeStruct((tokens, d), F32), jax.ShapeDtypeStruct((tokens, d), BF16)],
        scratch_shapes=[pltpu.VMEM((2 * half, d), BF16)],
        compiler_params=_params(1),
    )(o_fox, o_ret, x, w_out, ln2)


ROW_BLOCK = 8
DOWN_CHUNK = 512
SUB_SPLIT = (1, 1)
FFN_TOKEN_TILE = 1024
FFN_WIDTH_TILE = 512


def _convffn_kernel(*refs, n_seg, seg, tps, carried):
    if carried:
        (h2_ref, xm_ref, wa_ref, wg_ref, cwa_ref, cwg_ref, cba_ref, cbg_ref, wd_ref,
         out_ref, ta_ref, tg_ref, z_ref, ca_ref, cg_ref) = refs
        hists = (None, None)
        carries = (ca_ref, cg_ref)
    else:
        (h2_ref, xm_ref, wa_ref, wg_ref, cwa_ref, cwg_ref, cba_ref, cbg_ref, wd_ref, ha_ref, hg_ref,
         out_ref, ta_ref, tg_ref, z_ref) = refs
        hists = (ha_ref, hg_ref)
        carries = (None, None)
    i = pl.program_id(0)
    j = pl.program_id(1)
    tf = wa_ref.shape[1]
    d = out_ref.shape[1]
    tm = n_seg * seg
    col0 = pl.multiple_of(j * tf, tf)

    @pl.when(j == 0)
    def _():
        out_ref[...] = xm_ref[...]

    if carried:
        @pl.when(i % tps == 0)
        def _():
            for carry_ref in carries:
                carry_ref[0:2, pl.ds(col0, tf)] = jnp.zeros((2, tf), F32)

    unit = tm // sum(SUB_SPLIT)
    starts = [unit * sum(SUB_SPLIT[:n]) for n in range(len(SUB_SPLIT) + 1)]
    halves = ((wa_ref, cwa_ref, cba_ref, ta_ref, carries[0], hists[0]),
              (wg_ref, cwg_ref, cbg_ref, tg_ref, carries[1], hists[1]))
    ups = [[jnp.dot(h2_ref[lo:hi, :], w_ref[...], preferred_element_type=F32)
            for w_ref, *_ in halves] for lo, hi in zip(starts[:-1], starts[1:])]

    row = lax.broadcasted_iota(jnp.int32, (ROW_BLOCK, tf), 0)
    taps = [[jnp.broadcast_to(cw_ref[t:t + 1, :], (ROW_BLOCK, tf)) for t in range(3)]
            for _, cw_ref, *_ in halves]
    bias = [jnp.broadcast_to(cb_ref[...], (ROW_BLOCK, tf)) for _, _, cb_ref, *_ in halves]
    prev = [None, None]

    def conv_block(x, r0):
        _, _, _, tail_ref, carry_ref, hist_ref = halves[x]
        s, off = divmod(r0, seg)
        if off == 0:
            if carried:
                h0 = carry_ref[0:1, pl.ds(col0, tf)]
                h1 = carry_ref[1:2, pl.ds(col0, tf)]
            else:
                h0 = hist_ref[s, 0:1, :]
                h1 = hist_ref[s, 1:2, :]
            prev[x] = (jnp.broadcast_to(h1, (ROW_BLOCK, tf)), jnp.where(row == 0, h0, h1))
        si = max(n for n, lo in enumerate(starts[:-1]) if lo <= r0)
        cur = ups[si][x][r0 - starts[si]:r0 - starts[si] + ROW_BLOCK]
        rot1 = pltpu.roll(cur, 1, 0)
        rot2 = pltpu.roll(cur, 2, 0)
        um1 = jnp.where(row == 0, prev[x][0], rot1)
        um2 = jnp.where(row <= 1, prev[x][1], rot2)
        prev[x] = (rot1, rot2)
        if off == seg - ROW_BLOCK:
            tail_ref[s] = cur[ROW_BLOCK - 2:ROW_BLOCK, :]
        if carried and r0 == tm - ROW_BLOCK:
            carry_ref[0:2, pl.ds(col0, tf)] = cur[ROW_BLOCK - 2:ROW_BLOCK, :]
        return bias[x] + (um2 * taps[x][0] + um1 * taps[x][1] + cur * taps[x][2])

    pack = 2 * ROW_BLOCK
    for lo, hi in zip(starts[:-1], starts[1:]):
        for r0 in range(lo, hi, pack):
            a = jnp.concatenate([conv_block(0, r0), conv_block(0, r0 + ROW_BLOCK)], axis=0)
            g = jnp.concatenate([conv_block(1, r0), conv_block(1, r0 + ROW_BLOCK)], axis=0)
            z_ref[r0:r0 + pack, :] = ((g * _sigmoid(g)) * a).astype(BF16)
        z = z_ref[lo:hi, :]
        for c0 in range(0, d, DOWN_CHUNK):
            out_ref[lo:hi, c0:c0 + DOWN_CHUNK] += jnp.dot(
                z, wd_ref[:, c0:c0 + DOWN_CHUNK], preferred_element_type=F32)


def _convffn(h2, xm, w_up, conv_w, conv_b, w_down, hist, *, batch, seq):
    tokens, d = xm.shape
    f = w_down.shape[0]
    tf = min(FFN_WIDTH_TILE, f)
    nf = f // tf
    tm = min(FFN_TOKEN_TILE, tokens)
    seg = min(seq, tm)
    n_seg = tm // seg
    tps = seq // seg
    carried = hist is None
    assert not carried or n_seg == 1, "a carried conv state needs one sequence per token tile"
    row = lambda i, j: (i, 0)
    a_col = lambda i, j: (0, j)
    g_col = lambda i, j: (0, nf + j)
    in_specs = [pl.BlockSpec((tm, d), row), _resident((tm, d), row),
                pl.BlockSpec((d, tf), a_col), pl.BlockSpec((d, tf), g_col),
                pl.BlockSpec((3, tf), a_col), pl.BlockSpec((3, tf), g_col),
                pl.BlockSpec((1, tf), a_col), pl.BlockSpec((1, tf), g_col),
                pl.BlockSpec((tf, d), lambda i, j: (j, 0))]
    args = [h2, xm, w_up, w_up, conv_w, conv_w, conv_b, conv_b, w_down]
    scratch = [pltpu.VMEM((tm, tf), BF16)]
    if carried:
        scratch += [pltpu.VMEM((8, f), F32), pltpu.VMEM((8, f), F32)]
    else:
        in_specs += [pl.BlockSpec((n_seg, 2, tf), lambda i, j: (i, 0, j)),
                     pl.BlockSpec((n_seg, 2, tf), lambda i, j: (i, 0, nf + j))]
        args += [hist, hist]
    tail_spec = pl.BlockSpec((n_seg, 2, tf), lambda i, j: (i, 0, j))
    tail_shape = jax.ShapeDtypeStruct((batch * tps, 2, f), F32)
    out, tails_a, tails_g = pl.pallas_call(
        functools.partial(_convffn_kernel, n_seg=n_seg, seg=seg, tps=tps, carried=carried),
        grid=(tokens // tm, nf),
        in_specs=in_specs,
        out_specs=[pl.BlockSpec((tm, d), row), tail_spec, tail_spec],
        out_shape=[jax.ShapeDtypeStruct((tokens, d), F32), tail_shape, tail_shape],
        scratch_shapes=scratch,
        compiler_params=_params(2),
    )(*args)
    last = lambda t: t.reshape(batch, tps, 2, f)[:, tps - 1]
    return out, jnp.concatenate([last(tails_a), last(tails_g)], axis=-1)


def _rope_tables(pos, rows):
    half = LANES // 2
    inv = jnp.power(ROPE_BASE, -jnp.arange(half, dtype=F32) / half)
    ang = pos.astype(F32)[:, None] * inv[None, :]
    cos = jnp.cos(ang)
    sin = jnp.sin(ang)
    cos_t = jnp.concatenate([cos, cos], axis=1)
    sin_t = jnp.concatenate([-sin, sin], axis=1)
    reps = rows // pos.shape[0]
    return jnp.tile(cos_t, (reps, 1)), jnp.tile(sin_t, (reps, 1))


def kernel(x_prompt, x_sample, cache_fox_k, cache_fox_v, cache_fox_logf, state_ret, state_conv,
           ln1, w_in, b_f, fox_qn, fox_kn, ret_gn, w_out, ln2, w_up, conv_w, conv_b, w_down):
    depth = ln1.shape[0]
    bp, tp, d = x_prompt.shape
    bs, ts, _ = x_sample.shape
    past = cache_fox_k.shape[2]
    n_heads, hd = cache_fox_k.shape[3], cache_fox_k.shape[4]
    d_head = n_heads * hd
    assert hd == LANES and n_heads % HEADS_PER_STEP == 0 and ret_gn.shape[1:] == (n_heads, hd)

    cos_p, sin_p = _rope_tables(jnp.arange(tp), max(tp, min(INPROJ_TOKEN_TILE, bp * tp)))
    cos_s, sin_s = _rope_tables(past + jnp.arange(ts), max(ts, min(INPROJ_TOKEN_TILE, bs * ts)))
    cache_k = cache_fox_k.reshape(depth * bs, past * n_heads, hd)
    cache_v = cache_fox_v.reshape(depth * bs, past * n_heads, hd)
    state_r = state_ret.reshape(depth * bs, n_heads, hd, hd)
    zero_state = jnp.zeros((bp, n_heads, hd, hd), F32)

    xp = x_prompt.reshape(bp * tp, d)
    xs = x_sample.reshape(bs * ts, d)
    st_p, st_s = [], []
    for l in range(depth):
        r0 = N_FOX_KINDS * d_head + n_heads
        w_fox = w_in[l][:, :N_FOX_KINDS * d_head].astype(BF16)
        w_ret = w_in[l][:, r0:].astype(BF16)
        wf = jnp.pad(w_in[l][:, N_FOX_KINDS * d_head:r0], ((0, 0), (0, hd - n_heads))).astype(BF16)
        bfp = jnp.pad(b_f[l], (0, hd - n_heads)).reshape(1, hd)
        qn = fox_qn[l].reshape(1, hd)
        kn = fox_kn[l].reshape(1, hd)
        wu = w_up[l].astype(BF16)
        wd = w_down[l].astype(BF16)
        cw = conv_w[l]
        cb = conv_b[l].reshape(1, -1)
        ln1_l = ln1[l].reshape(1, d)
        ln2_l = ln2[l].reshape(1, d)

        def group(x, cos_t, sin_t, batch, seq):
            return _inproj(x, ln1_l, w_fox, w_ret, wf, bfp, qn, kn, cos_t, sin_t,
                           batch=batch, seq=seq, n_heads=n_heads)

        fq, fkb, fvb, rq, rk, rv, rg, fk, fv, lf = group(xp, cos_p, sin_p, bp, tp)
        c_row = _cumsum(lf.reshape(bp, tp, n_heads).transpose(0, 2, 1))
        o_fox = _fox_prompt(fq, fkb, fvb, c_row, qn, kn)
        o_ret, s_ret_p = _retention(rq, rk, rv, rg, zero_state, ret_gn[l], s0_block0=0)
        xm, h2 = _outproj(o_fox, o_ret, xp, w_out[l], ln2_l)
        xp, conv_p = _convffn(h2, xm, wu, cw, cb, wd, None, batch=bp, seq=tp)
        st_p.append((fk.reshape(bp, tp, n_heads, hd), fv.reshape(bp, tp, n_heads, hd),
                     lf.reshape(bp, tp, n_heads), s_ret_p, conv_p))

        fq, fkb, fvb, rq, rk, rv, rg, fk, fv, lf = group(xs, cos_s, sin_s, bs, ts)
        lf_all = jnp.concatenate([cache_fox_logf[l], lf.reshape(bs, ts, n_heads)], axis=1)
        pad = (-lf_all.shape[1]) % LANES
        lf_all = jnp.pad(lf_all, ((0, 0), (0, pad), (0, 0)))
        c_row = _cumsum(lf_all.transpose(0, 2, 1))
        o_fox = _fox_sample(fq, fkb, fvb, cache_k, cache_v, c_row, layer=l)
        o_ret, s_ret_s = _retention(rq, rk, rv, rg, state_r, ret_gn[l], s0_block0=l * bs)
        xm, h2 = _outproj(o_fox, o_ret, xs, w_out[l], ln2_l)
        xs, conv_s = _convffn(h2, xm, wu, cw, cb, wd, state_conv[l], batch=bs, seq=ts)
        st_s.append((fk.reshape(bs, ts, n_heads, hd), fv.reshape(bs, ts, n_heads, hd),
                     lf.reshape(bs, ts, n_heads), s_ret_s, conv_s))

    stack = lambda st, k: jnp.stack([s[k] for s in st])
    return (xp.reshape(bp, tp, d), xs.reshape(bs, ts, d),
            stack(st_p, 0), stack(st_p, 1), stack(st_p, 2), stack(st_p, 3), stack(st_p, 4),
            stack(st_s, 0), stack(st_s, 1), stack(st_s, 2), stack(st_s, 3), stack(st_s, 4))
```

```python
import functools
import math

import jax
import jax.numpy as jnp
from jax import lax
from jax.experimental import pallas as pl
from jax.experimental.pallas import tpu as pltpu

EPS = 1e-6
ROPE_BASE = 10000.0
MASKED_LOGIT = -1e30
LANES = 128
VMEM_LIMIT_BYTES = 56 * 1024 * 1024
LOG2E = math.log2(math.e)
FOX_BOUNDED_LIMIT = 100.0
F32_EXP2_UNDERFLOW = 152.0

F32 = jnp.float32
BF16 = jnp.bfloat16

N_FOX_KINDS = 3
N_RET_KINDS = 4
HEADS_PER_STEP = 2
INPROJ_TOKEN_TILE = 1024
FOX_HEADS_PER_STEP = 4

NT_DIMS = (((1,), (1,)), ((), ()))
TN_DIMS = (((0,), (0,)), ((), ()))


def _params(n_axes):
    return pltpu.CompilerParams(dimension_semantics=("arbitrary",) * n_axes,
                                vmem_limit_bytes=VMEM_LIMIT_BYTES)


def _resident(block_shape, index_map):
    return pl.BlockSpec(block_shape, index_map, pipeline_mode=pl.Buffered(1))


def _rms(x):
    return x * lax.rsqrt(jnp.mean(x * x, axis=-1, keepdims=True) + EPS)


def _sigmoid(x):
    return 1.0 / (1.0 + jnp.exp(-x))


def _inproj_kernel(x_ref, ln1_ref, wfq_ref, wfk_ref, wfv_ref, wrq_ref, wrk_ref, wrv_ref, wrg_ref,
                   wf_ref, bf_ref, qn_ref, kn_ref, cos_ref, sin_ref,
                   fq_ref, fkb_ref, fvb_ref, rq_ref, rk_ref, rv_ref, rg_ref, fk_ref, fv_ref, logf_ref,
                   h_ref, *, nb, seg, n_fox):
    j = pl.program_id(1)

    @pl.when(j == 0)
    def _():
        h = (_rms(x_ref[...]) * ln1_ref[...]).astype(BF16)
        h_ref[...] = h
        z = jnp.dot(h, wf_ref[...], preferred_element_type=F32) + bf_ref[...]
        logf = jnp.minimum(z, 0.0) - jnp.log1p(jnp.exp(-jnp.abs(z)))
        logf_ref[...] = logf[:, :n_fox]

    cos = cos_ref[...]
    sin = sin_ref[...]
    hd = LANES

    def project(w_ref):
        p = jnp.dot(h_ref[...], w_ref[...], preferred_element_type=F32)
        return [p[:, e * hd:(e + 1) * hd] for e in range(HEADS_PER_STEP)]

    def head_major(a):
        return a.reshape(nb, seg, hd).astype(BF16)

    def rope(a):
        return a * cos + pltpu.roll(a, hd // 2, 1) * sin

    for e, a in enumerate(project(wfq_ref)):
        fq_ref[:, e] = head_major(_rms(a) * qn_ref[...])
    for e, a in enumerate(project(wfk_ref)):
        fk = _rms(a) * kn_ref[...]
        fk_ref[:, e * hd:(e + 1) * hd] = fk
        fkb_ref[:, e] = head_major(fk)
    for e, a in enumerate(project(wfv_ref)):
        fv_ref[:, e * hd:(e + 1) * hd] = a
        fvb_ref[:, e] = head_major(a)
    for e, a in enumerate(project(wrq_ref)):
        rq_ref[:, e] = head_major(rope(a))
    for e, a in enumerate(project(wrk_ref)):
        rk_ref[:, e] = head_major(rope(a) * (hd ** -0.5))
    for e, a in enumerate(project(wrv_ref)):
        rv_ref[:, e] = head_major(a)
    for e, a in enumerate(project(wrg_ref)):
        rg_ref[:, e] = head_major(a)


def _inproj(x, ln1, w_fox, w_ret, wf, bfp, qn, kn, cos_t, sin_t, *, batch, seq, n_heads):
    tokens, d = x.shape
    hd = LANES
    tm = min(INPROJ_TOKEN_TILE, tokens)
    seg = min(seq, tm)
    nb = tm // seg
    tps = seq // seg
    n_groups = n_heads // HEADS_PER_STEP
    gcols = HEADS_PER_STEP * hd
    grid = (tokens // tm, n_groups)

    def w_spec(kind):
        return pl.BlockSpec((d, gcols), lambda i, j: (0, kind * n_groups + j))

    hm_shape = jax.ShapeDtypeStruct((batch, n_heads, seq, hd), BF16)
    hm_spec = pl.BlockSpec((nb, HEADS_PER_STEP, seg, hd), lambda i, j: (i // tps, j, i % tps, 0))
    tok_shape = jax.ShapeDtypeStruct((tokens, n_heads * hd), F32)
    tok_spec = pl.BlockSpec((tm, gcols), lambda i, j: (i, j))

    return pl.pallas_call(
        functools.partial(_inproj_kernel, nb=nb, seg=seg, n_fox=n_heads),
        grid=grid,
        in_specs=[pl.BlockSpec((tm, d), lambda i, j: (i, 0)),
                  _resident((1, d), lambda i, j: (0, 0))]
                 + [w_spec(k) for k in range(N_FOX_KINDS)] + [w_spec(k) for k in range(N_RET_KINDS)]
                 + [_resident((d, hd), lambda i, j: (0, 0)),
                    _resident((1, hd), lambda i, j: (0, 0)),
                    _resident((1, hd), lambda i, j: (0, 0)),
                    _resident((1, hd), lambda i, j: (0, 0)),
                    pl.BlockSpec((tm, hd), lambda i, j: (i % tps, 0)),
                    pl.BlockSpec((tm, hd), lambda i, j: (i % tps, 0))],
        out_specs=[hm_spec] * 7 + [tok_spec, tok_spec,
                                   pl.BlockSpec((tm, n_heads), lambda i, j: (i, 0))],
        out_shape=[hm_shape] * 7 + [tok_shape, tok_shape,
                                    jax.ShapeDtypeStruct((tokens, n_heads), F32)],
        scratch_shapes=[pltpu.VMEM((tm, d), BF16)],
        compiler_params=_params(2),
    )(x, ln1, w_fox, w_fox, w_fox, w_ret, w_ret, w_ret, w_ret, wf, bfp, qn, kn, cos_t, sin_t)


def _cumsum_kernel(lf_ref, c_ref, *, ch):
    total = lf_ref.shape[2]
    r = lax.broadcasted_iota(jnp.int32, (ch, ch), 0)
    c = lax.broadcasted_iota(jnp.int32, (ch, ch), 1)
    upper = (r <= c).astype(F32)
    carry = jnp.zeros((lf_ref.shape[1], 1), F32)
    for k in range(total // ch):
        x = lf_ref[0, :, k * ch:(k + 1) * ch]
        cs = jnp.dot(x, upper, precision=lax.Precision.HIGHEST, preferred_element_type=F32) + carry
        c_ref[0, :, k * ch:(k + 1) * ch] = cs
        carry = cs[:, ch - 1:ch]


def _cumsum(lf_t):
    batch, n_heads, total = lf_t.shape
    ch = 256 if total % 256 == 0 else LANES
    spec = pl.BlockSpec((1, n_heads, total), lambda b: (b, 0, 0))
    return pl.pallas_call(
        functools.partial(_cumsum_kernel, ch=ch),
        grid=(batch,), in_specs=[spec], out_specs=spec,
        out_shape=jax.ShapeDtypeStruct(lf_t.shape, F32),
        compiler_params=_params(1),
    )(lf_t)


def _softmax_step(z, v, m_ref, l_ref, acc_ref):
    tk = z.shape[1]
    m_prev = m_ref[...]
    m_next = jnp.maximum(m_prev, jnp.max(z, axis=1, keepdims=True))
    p = jnp.exp2(z - pltpu.repeat(m_next, tk // LANES, 1))
    alpha = jnp.exp2(m_prev - m_next)
    l_ref[...] = alpha * l_ref[...] + jnp.sum(p, axis=1, keepdims=True)
    acc_ref[...] = alpha * acc_ref[...] + jnp.dot(p.astype(BF16), v, preferred_element_type=F32)
    m_ref[...] = m_next


def _fox_prompt_kernel(first_ref, q_ref, k_ref, v_ref, c_ref, o_ref, m_ref, l_ref, acc_ref, *, tq, scale):
    b, hg, qi = pl.program_id(0), pl.program_id(1), pl.program_id(2)
    group, hd = q_ref.shape[1], q_ref.shape[3]
    m_ref[...] = jnp.full(m_ref.shape, MASKED_LOGIT, F32)
    l_ref[...] = jnp.zeros(l_ref.shape, F32)
    acc_ref[...] = jnp.zeros(acc_ref.shape, F32)

    def logits(e, k0):
        k = k_ref[0, e, pl.ds(k0, tq), :]
        s = lax.dot_general(q_ref[0, e], k, NT_DIMS, preferred_element_type=F32)
        return s * (scale * LOG2E) - c_ref[0, e, :, pl.ds(k0, tq)] * LOG2E

    def body(j, carry):
        k0 = pl.multiple_of(j * tq, tq)
        for e in range(group):
            _softmax_step(logits(e, k0), v_ref[0, e, pl.ds(k0, tq), :],
                          m_ref.at[e], l_ref.at[e], acc_ref.at[e])
        return carry

    first = first_ref[(b * pl.num_programs(1) + hg) * pl.num_programs(2) + qi]
    lax.fori_loop(first, qi, body, 0)

    k0 = pl.multiple_of(qi * tq, tq)
    row = lax.broadcasted_iota(jnp.int32, (tq, tq), 0)
    col = lax.broadcasted_iota(jnp.int32, (tq, tq), 1)
    for e in range(group):
        z = jnp.where(col <= row, logits(e, k0), MASKED_LOGIT)
        _softmax_step(z, v_ref[0, e, pl.ds(k0, tq), :], m_ref.at[e], l_ref.at[e], acc_ref.at[e])
        o_ref[:, e * hd:(e + 1) * hd] = (acc_ref[e] / l_ref[e]).astype(o_ref.dtype)


def _fox_prompt_bounded_kernel(first_ref, q_ref, k_ref, v_ref, c2_ref, r2_ref, o_ref, l_ref, acc_ref, *, tq, scale):
    b, hg, qi = pl.program_id(0), pl.program_id(1), pl.program_id(2)
    group, hd = q_ref.shape[1], q_ref.shape[3]
    l_ref[...] = jnp.zeros(l_ref.shape, F32)
    acc_ref[...] = jnp.zeros(acc_ref.shape, F32)
    q0 = pl.multiple_of(qi * tq, tq)
    row = lax.broadcasted_iota(jnp.int32, (tq, tq), 0)
    col = lax.broadcasted_iota(jnp.int32, (tq, tq), 1)
    row_terms = [jnp.sum(jnp.where(row == col, r2_ref[0, e, :, pl.ds(q0, tq)], 0.0), axis=1, keepdims=True)
                 for e in range(group)]

    def step(e, k0, masked):
        k = k_ref[0, e, pl.ds(k0, tq), :]
        s = lax.dot_general(q_ref[0, e], k, NT_DIMS, preferred_element_type=F32)
        z = (s * (scale * LOG2E) - c2_ref[0, e, :, pl.ds(k0, tq)]) + row_terms[e]
        if masked:
            z = jnp.where(col <= row, z, MASKED_LOGIT)
        p = jnp.exp2(z)
        part = p[:, 0:LANES]
        for c0 in range(LANES, tq, LANES):
            part = part + p[:, c0:c0 + LANES]
        l_ref[e] += part
        acc_ref[e] += jnp.dot(p.astype(BF16), v_ref[0, e, pl.ds(k0, tq), :], preferred_element_type=F32)

    def body(j, carry):
        k0 = pl.multiple_of(j * tq, tq)
        for e in range(group):
            step(e, k0, False)
        return carry

    first = first_ref[(b * pl.num_programs(1) + hg) * pl.num_programs(2) + qi]
    lax.fori_loop(first, qi, body, 0)
    for e in range(group):
        step(e, q0, True)
        l = jnp.sum(l_ref[e], axis=1, keepdims=True)
        o_ref[:, e * hd:(e + 1) * hd] = (acc_ref[e] / l).astype(o_ref.dtype)


def _qk_logit_bound(qn, kn, hd, scale):
    bf16_slack = (1.0 + 2.0 ** -8) ** 2
    return (hd * scale * LOG2E * bf16_slack) * jnp.max(jnp.abs(qn)) * jnp.max(jnp.abs(kn))


def _first_key_tiles(c_row, bound, *, tq, group):
    batch, n_heads, seq = c_row.shape
    c_first_row = c_row[:, :, 0::tq]
    c_last_key = c_row[:, :, tq - 1::tq]
    gap = (c_last_key[:, :, None, :] - c_first_row[:, :, :, None]) * LOG2E
    nq = seq // tq
    earlier = jnp.arange(nq)[None, :] < jnp.arange(nq)[:, None]
    dead = (gap > 2.0 * bound + F32_EXP2_UNDERFLOW) & earlier
    first = jnp.sum(dead, axis=-1).astype(jnp.int32)
    return jnp.min(first.reshape(batch, n_heads // group, group, nq), axis=2).reshape(-1)


def _fox_prompt(q, k, v, c_row, qn, kn):
    batch, n_heads, seq, hd = q.shape
    tq = min(512, seq)
    nq = seq // tq
    group = FOX_HEADS_PER_STEP
    scale = hd ** -0.5
    bound = _qk_logit_bound(qn, kn, hd, scale)
    first = _first_key_tiles(c_row, bound, tq=tq, group=group)
    kv_spec = pl.BlockSpec((1, group, seq, hd), lambda b, h, i, first: (b, h, 0, 0))
    row_spec = pl.BlockSpec((1, group, 1, seq), lambda b, h, i, first: (b, h, 0, 0))
    q_spec = pl.BlockSpec((1, group, tq, hd), lambda b, h, i, first: (b, h, i, 0))
    out_spec = pl.BlockSpec((tq, group * hd), lambda b, h, i, first: (b * nq + i, h))
    out_shape = jax.ShapeDtypeStruct((batch * seq, n_heads * hd), BF16)
    stats = pltpu.VMEM((group, tq, LANES), F32)
    acc = pltpu.VMEM((group, tq, hd), F32)

    def running_max(c4):
        return pl.pallas_call(
            functools.partial(_fox_prompt_kernel, tq=tq, scale=scale),
            grid_spec=pltpu.PrefetchScalarGridSpec(
                num_scalar_prefetch=1, grid=(batch, n_heads // group, nq),
                in_specs=[q_spec, kv_spec, kv_spec, row_spec], out_specs=out_spec,
                scratch_shapes=[stats, stats, acc]),
            out_shape=out_shape, compiler_params=_params(3),
        )(first, q, k, v, c4)

    def bounded(c4):
        return pl.pallas_call(
            functools.partial(_fox_prompt_bounded_kernel, tq=tq, scale=scale),
            grid_spec=pltpu.PrefetchScalarGridSpec(
                num_scalar_prefetch=1, grid=(batch, n_heads // group, nq),
                in_specs=[q_spec, kv_spec, kv_spec, row_spec, row_spec], out_specs=out_spec,
                scratch_shapes=[stats, acc]),
            out_shape=out_shape, compiler_params=_params(3),
        )(first, q, k, v, c4 * LOG2E, c4 * LOG2E - bound)

    return lax.cond(2.0 * bound <= FOX_BOUNDED_LIMIT, bounded, running_max,
                    c_row.reshape(batch, n_heads, 1, seq))


def _fox_sample_kernel(q_ref, kn_ref, vn_ref, kc_ref, vc_ref, c_ref, o_ref, *, past, scale):
    n_heads, new, hd = q_ref.shape[1:]
    row = lax.broadcasted_iota(jnp.int32, (new, new), 0)
    col = lax.broadcasted_iota(jnp.int32, (new, new), 1)
    for h in range(n_heads):
        q = q_ref[0, h]
        kc = kc_ref[0, pl.ds(h, past, stride=n_heads), :].astype(BF16)
        vc = vc_ref[0, pl.ds(h, past, stride=n_heads), :].astype(BF16)
        z_c = (lax.dot_general(q, kc, NT_DIMS, preferred_element_type=F32) * scale
               - c_ref[0, h:h + 1, 0:past])
        z_n = (lax.dot_general(q, kn_ref[0, h], NT_DIMS, preferred_element_type=F32) * scale
               - c_ref[0, h:h + 1, past:past + new])
        z_n = jnp.where(col <= row, z_n, MASKED_LOGIT)
        m = jnp.maximum(jnp.max(z_c, axis=1, keepdims=True), jnp.max(z_n, axis=1, keepdims=True))
        p_c = jnp.exp(z_c - m)
        p_n = jnp.exp(z_n - m)
        l = jnp.sum(p_c, axis=1, keepdims=True) + jnp.sum(p_n, axis=1, keepdims=True)
        acc = (jnp.dot(p_c.astype(BF16), vc, preferred_element_type=F32)
               + jnp.dot(p_n.astype(BF16), vn_ref[0, h], preferred_element_type=F32))
        o_ref[:, h * hd:(h + 1) * hd] = (acc / l).astype(o_ref.dtype)


def _fox_sample(q, k_new, v_new, cache_k, cache_v, c_row, *, layer):
    batch, n_heads, new, hd = q.shape
    past = cache_k.shape[1] // n_heads
    new_spec = pl.BlockSpec((1, n_heads, new, hd), lambda b: (b, 0, 0, 0))
    cache_spec = pl.BlockSpec((1, past * n_heads, hd), lambda b: (layer * batch + b, 0, 0))
    return pl.pallas_call(
        functools.partial(_fox_sample_kernel, past=past, scale=hd ** -0.5),
        grid=(batch,),
        in_specs=[new_spec, new_spec, new_spec, cache_spec, cache_spec,
                  pl.BlockSpec((1, n_heads, c_row.shape[2]), lambda b: (b, 0, 0))],
        out_specs=pl.BlockSpec((new, n_heads * hd), lambda b: (b, 0)),
        out_shape=jax.ShapeDtypeStruct((batch * new, n_heads * hd), BF16),
        compiler_params=_params(1),
    )(q, k_new, v_new, cache_k, cache_v, c_row)


def _retention_kernel(q_ref, k_ref, v_ref, g_ref, s0_ref, gn_ref, o_ref, s_out_ref,
                      s_ref, decay_ref, inter_ref, upd_ref, *, log_gammas):
    b = pl.program_id(0)
    c = pl.program_id(1)
    n_heads, chunk, hd = q_ref.shape[1:]

    @pl.when((b == 0) & (c == 0))
    def _():
        i = lax.broadcasted_iota(jnp.int32, (chunk, chunk), 0)
        jj = lax.broadcasted_iota(jnp.int32, (chunk, chunk), 1)
        diff = (i - jj).astype(F32)
        pos = lax.broadcasted_iota(jnp.int32, (chunk, hd), 0).astype(F32)
        for h, lg in enumerate(log_gammas):
            decay_ref[h] = jnp.where(diff >= 0.0, jnp.exp(jnp.maximum(diff, 0.0) * lg), 0.0)
            inter_ref[h] = jnp.exp((pos + 1.0) * lg)
            upd_ref[h] = jnp.exp((chunk - 1.0 - pos) * lg)

    @pl.when(c == 0)
    def _():
        s_ref[...] = s0_ref[0]

    for h, lg in enumerate(log_gammas):
        q = q_ref[0, h]
        k = k_ref[0, h]
        v = v_ref[0, h]
        s_prev = s_ref[h]
        scores = lax.dot_general(q, k, NT_DIMS, preferred_element_type=F32) * decay_ref[h]
        o = jnp.dot(scores.astype(BF16), v, preferred_element_type=F32)
        o = o + jnp.dot(q, s_prev.astype(BF16), preferred_element_type=F32) * inter_ref[h]
        kd = (k.astype(F32) * upd_ref[h]).astype(BF16)
        s_ref[h] = math.exp(chunk * lg) * s_prev + lax.dot_general(kd, v, TN_DIMS, preferred_element_type=F32)
        g = g_ref[0, h].astype(F32)
        y = _rms(o) * gn_ref[h:h + 1, :]
        o_ref[:, h * hd:(h + 1) * hd] = (y * (g * _sigmoid(g))).astype(o_ref.dtype)

    @pl.when(c == pl.num_programs(1) - 1)
    def _():
        s_out_ref[0] = s_ref[...]


def _retention(q, k, v, g, s0, gn, *, s0_block0):
    batch, n_heads, seq, hd = q.shape
    chunk = min(256, seq)
    nc = seq // chunk
    log_gammas = tuple(math.log(1.0 - 2.0 ** (-5.0 - h)) for h in range(n_heads))
    qkv_spec = pl.BlockSpec((1, n_heads, chunk, hd), lambda b, c: (b, 0, c, 0))
    return pl.pallas_call(
        functools.partial(_retention_kernel, log_gammas=log_gammas),
        grid=(batch, nc),
        in_specs=[qkv_spec] * 4 + [pl.BlockSpec((1, n_heads, hd, hd), lambda b, c: (s0_block0 + b, 0, 0, 0)),
                                   _resident((n_heads, hd), lambda b, c: (0, 0))],
        out_specs=[pl.BlockSpec((chunk, n_heads * hd), lambda b, c: (b * nc + c, 0)),
                   pl.BlockSpec((1, n_heads, hd, hd), lambda b, c: (b, 0, 0, 0))],
        out_shape=[jax.ShapeDtypeStruct((batch * seq, n_heads * hd), BF16),
                   jax.ShapeDtypeStruct((batch, n_heads, hd, hd), F32)],
        scratch_shapes=[pltpu.VMEM((n_heads, hd, hd), F32), pltpu.VMEM((n_heads, chunk, chunk), F32),
                        pltpu.VMEM((n_heads, chunk, hd), F32), pltpu.VMEM((n_heads, chunk, hd), F32)],
        compiler_params=_params(2),
    )(q, k, v, g, s0, gn)


def _outproj_kernel(of_ref, or_ref, x_ref, w_ref, ln2_ref, xm_ref, h2_ref, wb_ref):
    @pl.when(pl.program_id(0) == 0)
    def _():
        wb_ref[...] = w_ref[...].astype(BF16)

    half = of_ref.shape[1]
    y = jnp.dot(of_ref[...], wb_ref[0:half, :], preferred_element_type=F32)
    y = y + jnp.dot(or_ref[...], wb_ref[half:2 * half, :], preferred_element_type=F32)
    xm = x_ref[...] + y
    xm_ref[...] = xm
    h2_ref[...] = (_rms(xm) * ln2_ref[...]).astype(BF16)


def _outproj(o_fox, o_ret, x, w_out, ln2):
    tokens, d = x.shape
    half = o_fox.shape[1]
    tm = min(512, tokens)
    row = lambda i: (i, 0)
    return pl.pallas_call(
        _outproj_kernel,
        grid=(tokens // tm,),
        in_specs=[pl.BlockSpec((tm, half), row), pl.BlockSpec((tm, half), row),
                  pl.BlockSpec((tm, d), row),
                  _resident((2 * half, d), lambda i: (0, 0)), _resident((1, d), lambda i: (0, 0))],
        out_specs=[pl.BlockSpec((tm, d), row), pl.BlockSpec((tm, d), row)],
        out_shape=[jax.ShapeDtypeStruct((tokens, d), F32), jax.ShapeDtypeStruct((tokens, d), BF16)],
        scratch_shapes=[pltpu.VMEM((2 * half, d), BF16)],
        compiler_params=_params(1),
    )(o_fox, o_ret, x, w_out, ln2)


ROW_BLOCK = 8
DOWN_CHUNK = 512
SUB_SPLIT = (1, 1)
FFN_TOKEN_TILE = 1024
FFN_WIDTH_TILE = 512


def _convffn_kernel(*refs, n_seg, seg, tps, carried):
    if carried:
        (h2_ref, xm_ref, wa_ref, wg_ref, cwa_ref, cwg_ref, cba_ref, cbg_ref, wd_ref,
         out_ref, ta_ref, tg_ref, z_ref, ca_ref, cg_ref) = refs
        hists = (None, None)
        carries = (ca_ref, cg_ref)
    else:
        (h2_ref, xm_ref, wa_ref, wg_ref, cwa_ref, cwg_ref, cba_ref, cbg_ref, wd_ref, ha_ref, hg_ref,
         out_ref, ta_ref, tg_ref, z_ref) = refs
        hists = (ha_ref, hg_ref)
        carries = (None, None)
    i = pl.program_id(0)
    j = pl.program_id(1)
    tf = wa_ref.shape[1]
    d = out_ref.shape[1]
    tm = n_seg * seg
    col0 = pl.multiple_of(j * tf, tf)

    xw = xm_ref.shape[1]

    @pl.when(j == 0)
    def _():
        out_ref[:, 0:xw] = xm_ref[...]
        out_ref[:, xw:d] = jnp.zeros((tm, d - xw), F32)

    @pl.when((j > 0) & (j < d // xw))
    def _():
        out_ref[:, pl.ds(pl.multiple_of(j * xw, xw), xw)] += xm_ref[...]

    if carried:
        @pl.when(i % tps == 0)
        def _():
            for carry_ref in carries:
                carry_ref[0:2, pl.ds(col0, tf)] = jnp.zeros((2, tf), F32)

    unit = tm // sum(SUB_SPLIT)
    starts = [unit * sum(SUB_SPLIT[:n]) for n in range(len(SUB_SPLIT) + 1)]
    halves = ((wa_ref, cwa_ref, cba_ref, ta_ref, carries[0], hists[0]),
              (wg_ref, cwg_ref, cbg_ref, tg_ref, carries[1], hists[1]))
    ups = [[jnp.dot(h2_ref[lo:hi, :], w_ref[...], preferred_element_type=F32)
            for w_ref, *_ in halves] for lo, hi in zip(starts[:-1], starts[1:])]

    row = lax.broadcasted_iota(jnp.int32, (ROW_BLOCK, tf), 0)
    taps = [[jnp.broadcast_to(cw_ref[t:t + 1, :], (ROW_BLOCK, tf)) for t in range(3)]
            for _, cw_ref, *_ in halves]
    bias = [jnp.broadcast_to(cb_ref[...], (ROW_BLOCK, tf)) for _, _, cb_ref, *_ in halves]
    prev = [None, None]

    def conv_block(x, r0):
        _, _, _, tail_ref, carry_ref, hist_ref = halves[x]
        s, off = divmod(r0, seg)
        if off == 0:
            if carried:
                h0 = carry_ref[0:1, pl.ds(col0, tf)]
                h1 = carry_ref[1:2, pl.ds(col0, tf)]
            else:
                h0 = hist_ref[s, 0:1, :]
                h1 = hist_ref[s, 1:2, :]
            prev[x] = (jnp.broadcast_to(h1, (ROW_BLOCK, tf)), jnp.where(row == 0, h0, h1))
        si = max(n for n, lo in enumerate(starts[:-1]) if lo <= r0)
        cur = ups[si][x][r0 - starts[si]:r0 - starts[si] + ROW_BLOCK]
        rot1 = pltpu.roll(cur, 1, 0)
        rot2 = pltpu.roll(cur, 2, 0)
        um1 = jnp.where(row == 0, prev[x][0], rot1)
        um2 = jnp.where(row <= 1, prev[x][1], rot2)
        prev[x] = (rot1, rot2)
        if off == seg - ROW_BLOCK:
            tail_ref[s] = cur[ROW_BLOCK - 2:ROW_BLOCK, :]
        if carried and r0 == tm - ROW_BLOCK:
            carry_ref[0:2, pl.ds(col0, tf)] = cur[ROW_BLOCK - 2:ROW_BLOCK, :]
        return bias[x] + (um2 * taps[x][0] + um1 * taps[x][1] + cur * taps[x][2])

    pack = 2 * ROW_BLOCK
    for lo, hi in zip(starts[:-1], starts[1:]):
        for r0 in range(lo, hi, pack):
            a = jnp.concatenate([conv_block(0, r0), conv_block(0, r0 + ROW_BLOCK)], axis=0)
            g = jnp.concatenate([conv_block(1, r0), conv_block(1, r0 + ROW_BLOCK)], axis=0)
            z_ref[r0:r0 + pack, :] = ((g * _sigmoid(g)) * a).astype(BF16)
        z = z_ref[lo:hi, :]
        for c0 in range(0, d, DOWN_CHUNK):
            out_ref[lo:hi, c0:c0 + DOWN_CHUNK] += jnp.dot(
                z, wd_ref[:, c0:c0 + DOWN_CHUNK], preferred_element_type=F32)


def _convffn(h2, xm, w_up, conv_w, conv_b, w_down, hist, *, batch, seq):
    tokens, d = xm.shape
    f = w_down.shape[0]
    tf = min(FFN_WIDTH_TILE, f)
    nf = f // tf
    tm = min(FFN_TOKEN_TILE, tokens)
    seg = min(seq, tm)
    n_seg = tm // seg
    tps = seq // seg
    carried = hist is None
    assert not carried or n_seg == 1, "a carried conv state needs one sequence per token tile"
    row = lambda i, j: (i, 0)
    a_col = lambda i, j: (0, j)
    g_col = lambda i, j: (0, nf + j)
    n_slabs = 1
    while 2 * n_slabs <= nf and d % (2 * n_slabs * LANES) == 0:
        n_slabs *= 2
    in_specs = [pl.BlockSpec((tm, d), row),
                pl.BlockSpec((tm, d // n_slabs), lambda i, j: (i, jnp.minimum(j, n_slabs - 1))),
                pl.BlockSpec((d, tf), a_col), pl.BlockSpec((d, tf), g_col),
                pl.BlockSpec((3, tf), a_col), pl.BlockSpec((3, tf), g_col),
                pl.BlockSpec((1, tf), a_col), pl.BlockSpec((1, tf), g_col),
                pl.BlockSpec((tf, d), lambda i, j: (j, 0))]
    args = [h2, xm, w_up, w_up, conv_w, conv_w, conv_b, conv_b, w_down]
    scratch = [pltpu.VMEM((tm, tf), BF16)]
    if carried:
        scratch += [pltpu.VMEM((8, f), F32), pltpu.VMEM((8, f), F32)]
    else:
        in_specs += [pl.BlockSpec((n_seg, 2, tf), lambda i, j: (i, 0, j)),
                     pl.BlockSpec((n_seg, 2, tf), lambda i, j: (i, 0, nf + j))]
        args += [hist, hist]
    tail_spec = pl.BlockSpec((n_seg, 2, tf), lambda i, j: (i, 0, j))
    tail_shape = jax.ShapeDtypeStruct((batch * tps, 2, f), F32)
    out, tails_a, tails_g = pl.pallas_call(
        functools.partial(_convffn_kernel, n_seg=n_seg, seg=seg, tps=tps, carried=carried),
        grid=(tokens // tm, nf),
        in_specs=in_specs,
        out_specs=[pl.BlockSpec((tm, d), row), tail_spec, tail_spec],
        out_shape=[jax.ShapeDtypeStruct((tokens, d), F32), tail_shape, tail_shape],
        scratch_shapes=scratch,
        compiler_params=_params(2),
    )(*args)
    last = lambda t: t.reshape(batch, tps, 2, f)[:, tps - 1]
    return out, jnp.concatenate([last(tails_a), last(tails_g)], axis=-1)


def _rope_tables(pos, rows):
    half = LANES // 2
    inv = jnp.power(ROPE_BASE, -jnp.arange(half, dtype=F32) / half)
    ang = pos.astype(F32)[:, None] * inv[None, :]
    cos = jnp.cos(ang)
    sin = jnp.sin(ang)
    cos_t = jnp.concatenate([cos, cos], axis=1)
    sin_t = jnp.concatenate([-sin, sin], axis=1)
    reps = rows // pos.shape[0]
    return jnp.tile(cos_t, (reps, 1)), jnp.tile(sin_t, (reps, 1))


def kernel(x_prompt, x_sample, cache_fox_k, cache_fox_v, cache_fox_logf, state_ret, state_conv,
           ln1, w_in, b_f, fox_qn, fox_kn, ret_gn, w_out, ln2, w_up, conv_w, conv_b, w_down):
    depth = ln1.shape[0]
    bp, tp, d = x_prompt.shape
    bs, ts, _ = x_sample.shape
    past = cache_fox_k.shape[2]
    n_heads, hd = cache_fox_k.shape[3], cache_fox_k.shape[4]
    d_head = n_heads * hd
    assert hd == LANES and n_heads % HEADS_PER_STEP == 0 and ret_gn.shape[1:] == (n_heads, hd)

    cos_p, sin_p = _rope_tables(jnp.arange(tp), max(tp, min(INPROJ_TOKEN_TILE, bp * tp)))
    cos_s, sin_s = _rope_tables(past + jnp.arange(ts), max(ts, min(INPROJ_TOKEN_TILE, bs * ts)))
    cache_k = cache_fox_k.reshape(depth * bs, past * n_heads, hd)
    cache_v = cache_fox_v.reshape(depth * bs, past * n_heads, hd)
    state_r = state_ret.reshape(depth * bs, n_heads, hd, hd)
    zero_state = jnp.zeros((bp, n_heads, hd, hd), F32)

    xp = x_prompt.reshape(bp * tp, d)
    xs = x_sample.reshape(bs * ts, d)
    st_p, st_s = [], []
    for l in range(depth):
        r0 = N_FOX_KINDS * d_head + n_heads
        w_fox = w_in[l][:, :N_FOX_KINDS * d_head].astype(BF16)
        w_ret = w_in[l][:, r0:].astype(BF16)
        wf = jnp.pad(w_in[l][:, N_FOX_KINDS * d_head:r0], ((0, 0), (0, hd - n_heads))).astype(BF16)
        bfp = jnp.pad(b_f[l], (0, hd - n_heads)).reshape(1, hd)
        qn = fox_qn[l].reshape(1, hd)
        kn = fox_kn[l].reshape(1, hd)
        wu = w_up[l].astype(BF16)
        wd = w_down[l].astype(BF16)
        cw = conv_w[l]
        cb = conv_b[l].reshape(1, -1)
        ln1_l = ln1[l].reshape(1, d)
        ln2_l = ln2[l].reshape(1, d)

        def group(x, cos_t, sin_t, batch, seq):
            return _inproj(x, ln1_l, w_fox, w_ret, wf, bfp, qn, kn, cos_t, sin_t,
                           batch=batch, seq=seq, n_heads=n_heads)

        fq, fkb, fvb, rq, rk, rv, rg, fk, fv, lf = group(xp, cos_p, sin_p, bp, tp)
        c_row = _cumsum(lf.reshape(bp, tp, n_heads).transpose(0, 2, 1))
        o_fox = _fox_prompt(fq, fkb, fvb, c_row, qn, kn)
        o_ret, s_ret_p = _retention(rq, rk, rv, rg, zero_state, ret_gn[l], s0_block0=0)
        xm, h2 = _outproj(o_fox, o_ret, xp, w_out[l], ln2_l)
        xp, conv_p = _convffn(h2, xm, wu, cw, cb, wd, None, batch=bp, seq=tp)
        st_p.append((fk.reshape(bp, tp, n_heads, hd), fv.reshape(bp, tp, n_heads, hd),
                     lf.reshape(bp, tp, n_heads), s_ret_p, conv_p))

        fq, fkb, fvb, rq, rk, rv, rg, fk, fv, lf = group(xs, cos_s, sin_s, bs, ts)
        lf_all = jnp.concatenate([cache_fox_logf[l], lf.reshape(bs, ts, n_heads)], axis=1)
        pad = (-lf_all.shape[1]) % LANES
        lf_all = jnp.pad(lf_all, ((0, 0), (0, pad), (0, 0)))
        c_row = _cumsum(lf_all.transpose(0, 2, 1))
        o_fox = _fox_sample(fq, fkb, fvb, cache_k, cache_v, c_row, layer=l)
        o_ret, s_ret_s = _retention(rq, rk, rv, rg, state_r, ret_gn[l], s0_block0=l * bs)
        xm, h2 = _outproj(o_fox, o_ret, xs, w_out[l], ln2_l)
        xs, conv_s = _convffn(h2, xm, wu, cw, cb, wd, state_conv[l], batch=bs, seq=ts)
        st_s.append((fk.reshape(bs, ts, n_heads, hd), fv.reshape(bs, ts, n_heads, hd),
                     lf.reshape(bs, ts, n_heads), s_ret_s, conv_s))

    stack = lambda st, k: jnp.stack([s[k] for s in st])
    return (xp.reshape(bp, tp, d), xs.reshape(bs, ts, d),
            stack(st_p, 0), stack(st_p, 1), stack(st_p, 2), stack(st_p, 3), stack(st_p, 4),
            stack(st_s, 0), stack(st_s, 1), stack(st_s, 2), stack(st_s, 3), stack(st_s, 4))
```

```python
import functools
import math

import jax
import jax.numpy as jnp
import numpy as np
from jax import lax
from jax.experimental import pallas as pl
from jax.experimental.pallas import tpu as pltpu

EPS = 1e-6
ROPE_BASE = 10000.0
MASKED_LOGIT = -1e30
LANES = 128
VMEM_LIMIT_BYTES = 56 * 1024 * 1024
LOG2E = math.log2(math.e)
FOX_BOUNDED_LIMIT = 100.0
F32_EXP2_UNDERFLOW = 152.0

F32 = jnp.float32
BF16 = jnp.bfloat16

N_FOX_KINDS = 3
N_RET_KINDS = 4
HEADS_PER_STEP = 2
INPROJ_TOKEN_TILE = 1024
CUMSUM_ROWS = 64
FOX_HEADS_PER_STEP = 4

NT_DIMS = (((1,), (1,)), ((), ()))
TN_DIMS = (((0,), (0,)), ((), ()))


def _params(n_axes):
    return pltpu.CompilerParams(dimension_semantics=("arbitrary",) * n_axes,
                                vmem_limit_bytes=VMEM_LIMIT_BYTES)


def _resident(block_shape, index_map):
    return pl.BlockSpec(block_shape, index_map, pipeline_mode=pl.Buffered(1))


def _rms(x):
    return x * lax.rsqrt(jnp.mean(x * x, axis=-1, keepdims=True) + EPS)


def _sigmoid(x):
    return 1.0 / (1.0 + jnp.exp(-x))


def _inproj_kernel(x_ref, ln1_ref, wfq_ref, wfk_ref, wfv_ref, wrq_ref, wrk_ref, wrv_ref, wrg_ref,
                   wf_ref, bf_ref, qn_ref, kn_ref, cos_ref, sin_ref,
                   fq_ref, fkb_ref, fvb_ref, rq_ref, rk_ref, rv_ref, rg_ref, fk_ref, fv_ref, logf_ref,
                   h_ref, *, nb, seg, n_fox):
    j = pl.program_id(1)

    @pl.when(j == 0)
    def _():
        h = (_rms(x_ref[...]) * ln1_ref[...]).astype(BF16)
        h_ref[...] = h
        z = jnp.dot(h, wf_ref[...], preferred_element_type=F32) + bf_ref[...]
        logf = jnp.minimum(z, 0.0) - jnp.log1p(jnp.exp(-jnp.abs(z)))
        logf_ref[...] = logf[:, :n_fox]

    cos = cos_ref[...]
    sin = sin_ref[...]
    hd = LANES

    def project(w_ref):
        p = jnp.dot(h_ref[...], w_ref[...], preferred_element_type=F32)
        return [p[:, e * hd:(e + 1) * hd] for e in range(HEADS_PER_STEP)]

    def head_major(a):
        return a.reshape(nb, seg, hd).astype(BF16)

    def rope(a):
        return a * cos + pltpu.roll(a, hd // 2, 1) * sin

    for e, a in enumerate(project(wfq_ref)):
        fq_ref[:, e] = head_major(_rms(a) * qn_ref[...])
    for e, a in enumerate(project(wfk_ref)):
        fk = _rms(a) * kn_ref[...]
        fk_ref[:, e * hd:(e + 1) * hd] = fk
        fkb_ref[:, e] = head_major(fk)
    for e, a in enumerate(project(wfv_ref)):
        fv_ref[:, e * hd:(e + 1) * hd] = a
        fvb_ref[:, e] = head_major(a)
    for e, a in enumerate(project(wrq_ref)):
        rq_ref[:, e] = head_major(rope(a))
    for e, a in enumerate(project(wrk_ref)):
        rk_ref[:, e] = head_major(rope(a) * (hd ** -0.5))
    for e, a in enumerate(project(wrv_ref)):
        rv_ref[:, e] = head_major(a)
    for e, a in enumerate(project(wrg_ref)):
        rg_ref[:, e] = head_major(a)


def _inproj(x, ln1, w_fox, w_ret, wf, bfp, qn, kn, cos_t, sin_t, *, batch, seq, n_heads):
    tokens, d = x.shape
    hd = LANES
    tm = min(INPROJ_TOKEN_TILE, tokens)
    seg = min(seq, tm)
    nb = tm // seg
    tps = seq // seg
    n_groups = n_heads // HEADS_PER_STEP
    gcols = HEADS_PER_STEP * hd
    grid = (tokens // tm, n_groups)

    def w_spec(kind):
        return pl.BlockSpec((d, gcols), lambda i, j: (0, kind * n_groups + j))

    hm_shape = jax.ShapeDtypeStruct((batch, n_heads, seq, hd), BF16)
    hm_spec = pl.BlockSpec((nb, HEADS_PER_STEP, seg, hd), lambda i, j: (i // tps, j, i % tps, 0))
    tok_shape = jax.ShapeDtypeStruct((tokens, n_heads * hd), F32)
    tok_spec = pl.BlockSpec((tm, gcols), lambda i, j: (i, j))

    return pl.pallas_call(
        functools.partial(_inproj_kernel, nb=nb, seg=seg, n_fox=n_heads),
        grid=grid,
        in_specs=[pl.BlockSpec((tm, d), lambda i, j: (i, 0)),
                  _resident((1, d), lambda i, j: (0, 0))]
                 + [w_spec(k) for k in range(N_FOX_KINDS)] + [w_spec(k) for k in range(N_RET_KINDS)]
                 + [_resident((d, hd), lambda i, j: (0, 0)),
                    _resident((1, hd), lambda i, j: (0, 0)),
                    _resident((1, hd), lambda i, j: (0, 0)),
                    _resident((1, hd), lambda i, j: (0, 0)),
                    pl.BlockSpec((tm, hd), lambda i, j: (i % tps, 0)),
                    pl.BlockSpec((tm, hd), lambda i, j: (i % tps, 0))],
        out_specs=[hm_spec] * 7 + [tok_spec, tok_spec,
                                   pl.BlockSpec((tm, n_heads), lambda i, j: (i, 0))],
        out_shape=[hm_shape] * 7 + [tok_shape, tok_shape,
                                    jax.ShapeDtypeStruct((tokens, n_heads), F32)],
        scratch_shapes=[pltpu.VMEM((tm, d), BF16)],
        compiler_params=_params(2),
    )(x, ln1, w_fox, w_fox, w_fox, w_ret, w_ret, w_ret, w_ret, wf, bfp, qn, kn, cos_t, sin_t)


def _cumsum_kernel(lf_ref, c_ref, *, ch):
    rows, total = lf_ref.shape
    r = lax.broadcasted_iota(jnp.int32, (ch, ch), 0)
    c = lax.broadcasted_iota(jnp.int32, (ch, ch), 1)
    upper = (r <= c).astype(F32)
    carry = jnp.zeros((rows, 1), F32)
    for k in range(total // ch):
        x = lf_ref[:, k * ch:(k + 1) * ch]
        cs = jnp.dot(x, upper, precision=lax.Precision.HIGHEST, preferred_element_type=F32) + carry
        c_ref[:, k * ch:(k + 1) * ch] = cs
        carry = cs[:, ch - 1:ch]


def _cumsum(lf_t):
    batch, n_heads, total = lf_t.shape
    rows = batch * n_heads
    band = min(CUMSUM_ROWS, rows)
    ch = 256 if total % 256 == 0 else LANES
    spec = pl.BlockSpec((band, total), lambda b: (b, 0))
    out = pl.pallas_call(
        functools.partial(_cumsum_kernel, ch=ch),
        grid=(rows // band,), in_specs=[spec], out_specs=spec,
        out_shape=jax.ShapeDtypeStruct((rows, total), F32),
        compiler_params=_params(1),
    )(lf_t.reshape(rows, total))
    return out.reshape(batch, n_heads, total)


def _softmax_step(z, v, m_ref, l_ref, acc_ref):
    tk = z.shape[1]
    m_prev = m_ref[...]
    m_next = jnp.maximum(m_prev, jnp.max(z, axis=1, keepdims=True))
    p = jnp.exp2(z - pltpu.repeat(m_next, tk // LANES, 1))
    alpha = jnp.exp2(m_prev - m_next)
    l_ref[...] = alpha * l_ref[...] + jnp.sum(p, axis=1, keepdims=True)
    acc_ref[...] = alpha * acc_ref[...] + jnp.dot(p.astype(BF16), v, preferred_element_type=F32)
    m_ref[...] = m_next


def _fox_prompt_kernel(first_ref, q_ref, k_ref, v_ref, c_ref, o_ref, m_ref, l_ref, acc_ref, *, tq, scale):
    b, hg, qi = pl.program_id(0), pl.program_id(1), pl.program_id(2)
    group, hd = q_ref.shape[1], q_ref.shape[3]
    m_ref[...] = jnp.full(m_ref.shape, MASKED_LOGIT, F32)
    l_ref[...] = jnp.zeros(l_ref.shape, F32)
    acc_ref[...] = jnp.zeros(acc_ref.shape, F32)

    def logits(e, k0):
        k = k_ref[0, e, pl.ds(k0, tq), :]
        s = lax.dot_general(q_ref[0, e], k, NT_DIMS, preferred_element_type=F32)
        return s * (scale * LOG2E) - c_ref[0, e, :, pl.ds(k0, tq)] * LOG2E

    def body(j, carry):
        k0 = pl.multiple_of(j * tq, tq)
        for e in range(group):
            _softmax_step(logits(e, k0), v_ref[0, e, pl.ds(k0, tq), :],
                          m_ref.at[e], l_ref.at[e], acc_ref.at[e])
        return carry

    first = first_ref[(b * pl.num_programs(1) + hg) * pl.num_programs(2) + qi]
    lax.fori_loop(first, qi, body, 0)

    k0 = pl.multiple_of(qi * tq, tq)
    row = lax.broadcasted_iota(jnp.int32, (tq, tq), 0)
    col = lax.broadcasted_iota(jnp.int32, (tq, tq), 1)
    for e in range(group):
        z = jnp.where(col <= row, logits(e, k0), MASKED_LOGIT)
        _softmax_step(z, v_ref[0, e, pl.ds(k0, tq), :], m_ref.at[e], l_ref.at[e], acc_ref.at[e])
        o_ref[:, e * hd:(e + 1) * hd] = (acc_ref[e] / l_ref[e]).astype(o_ref.dtype)


def _fox_prompt_bounded_kernel(first_ref, q_ref, k_ref, v_ref, c2_ref, r2_ref, o_ref, l_ref, acc_ref, *, tq, scale):
    b, hg, qi = pl.program_id(0), pl.program_id(1), pl.program_id(2)
    group, hd = q_ref.shape[1], q_ref.shape[3]
    l_ref[...] = jnp.zeros(l_ref.shape, F32)
    acc_ref[...] = jnp.zeros(acc_ref.shape, F32)
    q0 = pl.multiple_of(qi * tq, tq)
    row = lax.broadcasted_iota(jnp.int32, (tq, tq), 0)
    col = lax.broadcasted_iota(jnp.int32, (tq, tq), 1)
    row_terms = [jnp.sum(jnp.where(row == col, r2_ref[0, e, :, pl.ds(q0, tq)], 0.0), axis=1, keepdims=True)
                 for e in range(group)]

    def step(e, k0, masked):
        k = k_ref[0, e, pl.ds(k0, tq), :]
        s = lax.dot_general(q_ref[0, e], k, NT_DIMS, preferred_element_type=F32)
        z = (s * (scale * LOG2E) - c2_ref[0, e, :, pl.ds(k0, tq)]) + row_terms[e]
        if masked:
            z = jnp.where(col <= row, z, MASKED_LOGIT)
        p = jnp.exp2(z)
        part = p[:, 0:LANES]
        for c0 in range(LANES, tq, LANES):
            part = part + p[:, c0:c0 + LANES]
        l_ref[e] += part
        acc_ref[e] += jnp.dot(p.astype(BF16), v_ref[0, e, pl.ds(k0, tq), :], preferred_element_type=F32)

    def body(j, carry):
        k0 = pl.multiple_of(j * tq, tq)
        for e in range(group):
            step(e, k0, False)
        return carry

    first = first_ref[(b * pl.num_programs(1) + hg) * pl.num_programs(2) + qi]
    lax.fori_loop(first, qi, body, 0)
    for e in range(group):
        step(e, q0, True)
        l = jnp.sum(l_ref[e], axis=1, keepdims=True)
        o_ref[:, e * hd:(e + 1) * hd] = (acc_ref[e] / l).astype(o_ref.dtype)


def _qk_logit_bound(qn, kn, hd, scale):
    bf16_slack = (1.0 + 2.0 ** -8) ** 2
    return (hd * scale * LOG2E * bf16_slack) * jnp.max(jnp.abs(qn)) * jnp.max(jnp.abs(kn))


def _first_key_tiles(c_row, bound, *, tq, group):
    batch, n_heads, seq = c_row.shape
    c_first_row = c_row[:, :, 0::tq]
    c_last_key = c_row[:, :, tq - 1::tq]
    gap = (c_last_key[:, :, None, :] - c_first_row[:, :, :, None]) * LOG2E
    nq = seq // tq
    earlier = jnp.arange(nq)[None, :] < jnp.arange(nq)[:, None]
    dead = (gap > 2.0 * bound + F32_EXP2_UNDERFLOW) & earlier
    first = jnp.sum(dead, axis=-1).astype(jnp.int32)
    return jnp.min(first.reshape(batch, n_heads // group, group, nq), axis=2).reshape(-1)


def _fox_prompt(q, k, v, c_row, qn, kn):
    batch, n_heads, seq, hd = q.shape
    tq = min(512, seq)
    nq = seq // tq
    group = FOX_HEADS_PER_STEP
    scale = hd ** -0.5
    bound = _qk_logit_bound(qn, kn, hd, scale)
    first = _first_key_tiles(c_row, bound, tq=tq, group=group)
    kv_spec = pl.BlockSpec((1, group, seq, hd), lambda b, h, i, first: (b, h, 0, 0))
    row_spec = pl.BlockSpec((1, group, 1, seq), lambda b, h, i, first: (b, h, 0, 0))
    q_spec = pl.BlockSpec((1, group, tq, hd), lambda b, h, i, first: (b, h, i, 0))
    out_spec = pl.BlockSpec((tq, group * hd), lambda b, h, i, first: (b * nq + i, h))
    out_shape = jax.ShapeDtypeStruct((batch * seq, n_heads * hd), BF16)
    stats = pltpu.VMEM((group, tq, LANES), F32)
    acc = pltpu.VMEM((group, tq, hd), F32)

    def running_max(c4):
        return pl.pallas_call(
            functools.partial(_fox_prompt_kernel, tq=tq, scale=scale),
            grid_spec=pltpu.PrefetchScalarGridSpec(
                num_scalar_prefetch=1, grid=(batch, n_heads // group, nq),
                in_specs=[q_spec, kv_spec, kv_spec, row_spec], out_specs=out_spec,
                scratch_shapes=[stats, stats, acc]),
            out_shape=out_shape, compiler_params=_params(3),
        )(first, q, k, v, c4)

    def bounded(c4):
        return pl.pallas_call(
            functools.partial(_fox_prompt_bounded_kernel, tq=tq, scale=scale),
            grid_spec=pltpu.PrefetchScalarGridSpec(
                num_scalar_prefetch=1, grid=(batch, n_heads // group, nq),
                in_specs=[q_spec, kv_spec, kv_spec, row_spec, row_spec], out_specs=out_spec,
                scratch_shapes=[stats, acc]),
            out_shape=out_shape, compiler_params=_params(3),
        )(first, q, k, v, c4 * LOG2E, c4 * LOG2E - bound)

    return lax.cond(2.0 * bound <= FOX_BOUNDED_LIMIT, bounded, running_max,
                    c_row.reshape(batch, n_heads, 1, seq))


def _fox_sample_kernel(q_ref, kn_ref, vn_ref, kc_ref, vc_ref, c_ref, o_ref, *, past, scale):
    n_heads, new, hd = q_ref.shape[1:]
    row = lax.broadcasted_iota(jnp.int32, (new, new), 0)
    col = lax.broadcasted_iota(jnp.int32, (new, new), 1)
    for h in range(n_heads):
        q = q_ref[0, h]
        kc = kc_ref[0, pl.ds(h, past, stride=n_heads), :].astype(BF16)
        vc = vc_ref[0, pl.ds(h, past, stride=n_heads), :].astype(BF16)
        z_c = (lax.dot_general(q, kc, NT_DIMS, preferred_element_type=F32) * scale
               - c_ref[0, h:h + 1, 0:past])
        z_n = (lax.dot_general(q, kn_ref[0, h], NT_DIMS, preferred_element_type=F32) * scale
               - c_ref[0, h:h + 1, past:past + new])
        z_n = jnp.where(col <= row, z_n, MASKED_LOGIT)
        m = jnp.maximum(jnp.max(z_c, axis=1, keepdims=True), jnp.max(z_n, axis=1, keepdims=True))
        p_c = jnp.exp(z_c - m)
        p_n = jnp.exp(z_n - m)
        l = jnp.sum(p_c, axis=1, keepdims=True) + jnp.sum(p_n, axis=1, keepdims=True)
        acc = (jnp.dot(p_c.astype(BF16), vc, preferred_element_type=F32)
               + jnp.dot(p_n.astype(BF16), vn_ref[0, h], preferred_element_type=F32))
        o_ref[:, h * hd:(h + 1) * hd] = (acc / l).astype(o_ref.dtype)


def _fox_sample(q, k_new, v_new, cache_k, cache_v, c_row, *, layer):
    batch, n_heads, new, hd = q.shape
    past = cache_k.shape[1] // n_heads
    new_spec = pl.BlockSpec((1, n_heads, new, hd), lambda b: (b, 0, 0, 0))
    cache_spec = pl.BlockSpec((1, past * n_heads, hd), lambda b: (layer * batch + b, 0, 0))
    return pl.pallas_call(
        functools.partial(_fox_sample_kernel, past=past, scale=hd ** -0.5),
        grid=(batch,),
        in_specs=[new_spec, new_spec, new_spec, cache_spec, cache_spec,
                  pl.BlockSpec((1, n_heads, c_row.shape[2]), lambda b: (b, 0, 0))],
        out_specs=pl.BlockSpec((new, n_heads * hd), lambda b: (b, 0)),
        out_shape=jax.ShapeDtypeStruct((batch * new, n_heads * hd), BF16),
        compiler_params=_params(1),
    )(q, k_new, v_new, cache_k, cache_v, c_row)


def _retention_kernel(q_ref, k_ref, v_ref, g_ref, s0_ref, gn_ref, o_ref, s_out_ref,
                      s_ref, decay_ref, inter_ref, upd_ref, *, log_gammas):
    b = pl.program_id(0)
    c = pl.program_id(1)
    n_heads, chunk, hd = q_ref.shape[1:]

    @pl.when((b == 0) & (c == 0))
    def _():
        i = lax.broadcasted_iota(jnp.int32, (chunk, chunk), 0)
        jj = lax.broadcasted_iota(jnp.int32, (chunk, chunk), 1)
        diff = (i - jj).astype(F32)
        pos = lax.broadcasted_iota(jnp.int32, (chunk, hd), 0).astype(F32)
        for h, lg in enumerate(log_gammas):
            decay_ref[h] = jnp.where(diff >= 0.0, jnp.exp(jnp.maximum(diff, 0.0) * lg), 0.0)
            inter_ref[h] = jnp.exp((pos + 1.0) * lg)
            upd_ref[h] = jnp.exp((chunk - 1.0 - pos) * lg)

    @pl.when(c == 0)
    def _():
        s_ref[...] = s0_ref[0]

    for h, lg in enumerate(log_gammas):
        q = q_ref[0, h]
        k = k_ref[0, h]
        v = v_ref[0, h]
        s_prev = s_ref[h]
        scores = lax.dot_general(q, k, NT_DIMS, preferred_element_type=F32) * decay_ref[h]
        o = jnp.dot(scores.astype(BF16), v, preferred_element_type=F32)
        o = o + jnp.dot(q, s_prev.astype(BF16), preferred_element_type=F32) * inter_ref[h]
        kd = (k.astype(F32) * upd_ref[h]).astype(BF16)
        s_ref[h] = math.exp(chunk * lg) * s_prev + lax.dot_general(kd, v, TN_DIMS, preferred_element_type=F32)
        g = g_ref[0, h].astype(F32)
        y = _rms(o) * gn_ref[h:h + 1, :]
        o_ref[:, h * hd:(h + 1) * hd] = (y * (g * _sigmoid(g))).astype(o_ref.dtype)

    @pl.when(c == pl.num_programs(1) - 1)
    def _():
        s_out_ref[0] = s_ref[...]


def _retention(q, k, v, g, s0, gn, *, s0_block0):
    batch, n_heads, seq, hd = q.shape
    chunk = min(256, seq)
    nc = seq // chunk
    log_gammas = tuple(math.log(1.0 - 2.0 ** (-5.0 - h)) for h in range(n_heads))
    qkv_spec = pl.BlockSpec((1, n_heads, chunk, hd), lambda b, c: (b, 0, c, 0))
    return pl.pallas_call(
        functools.partial(_retention_kernel, log_gammas=log_gammas),
        grid=(batch, nc),
        in_specs=[qkv_spec] * 4 + [pl.BlockSpec((1, n_heads, hd, hd), lambda b, c: (s0_block0 + b, 0, 0, 0)),
                                   _resident((n_heads, hd), lambda b, c: (0, 0))],
        out_specs=[pl.BlockSpec((chunk, n_heads * hd), lambda b, c: (b * nc + c, 0)),
                   pl.BlockSpec((1, n_heads, hd, hd), lambda b, c: (b, 0, 0, 0))],
        out_shape=[jax.ShapeDtypeStruct((batch * seq, n_heads * hd), BF16),
                   jax.ShapeDtypeStruct((batch, n_heads, hd, hd), F32)],
        scratch_shapes=[pltpu.VMEM((n_heads, hd, hd), F32), pltpu.VMEM((n_heads, chunk, chunk), F32),
                        pltpu.VMEM((n_heads, chunk, hd), F32), pltpu.VMEM((n_heads, chunk, hd), F32)],
        compiler_params=_params(2),
    )(q, k, v, g, s0, gn)


def _outproj_kernel(of_ref, or_ref, x_ref, w_ref, ln2_ref, xm_ref, h2_ref, wb_ref):
    @pl.when(pl.program_id(0) == 0)
    def _():
        wb_ref[...] = w_ref[...].astype(BF16)

    half = of_ref.shape[1]
    y = jnp.dot(of_ref[...], wb_ref[0:half, :], preferred_element_type=F32)
    y = y + jnp.dot(or_ref[...], wb_ref[half:2 * half, :], preferred_element_type=F32)
    xm = x_ref[...] + y
    xm_ref[...] = xm
    h2_ref[...] = (_rms(xm) * ln2_ref[...]).astype(BF16)


def _outproj(o_fox, o_ret, x, w_out, ln2):
    tokens, d = x.shape
    half = o_fox.shape[1]
    tm = min(512, tokens)
    row = lambda i: (i, 0)
    return pl.pallas_call(
        _outproj_kernel,
        grid=(tokens // tm,),
        in_specs=[pl.BlockSpec((tm, half), row), pl.BlockSpec((tm, half), row),
                  pl.BlockSpec((tm, d), row),
                  _resident((2 * half, d), lambda i: (0, 0)), _resident((1, d), lambda i: (0, 0))],
        out_specs=[pl.BlockSpec((tm, d), row), pl.BlockSpec((tm, d), row)],
        out_shape=[jax.ShapeDtypeStruct((tokens, d), F32), jax.ShapeDtypeStruct((tokens, d), BF16)],
        scratch_shapes=[pltpu.VMEM((2 * half, d), BF16)],
        compiler_params=_params(1),
    )(o_fox, o_ret, x, w_out, ln2)


ROW_BLOCK = 8
DOWN_CHUNK = 512
SUB_SPLIT = (1, 1)
FFN_TOKEN_TILE = 1024
FFN_WIDTH_TILE = 512


def _convffn_kernel(*refs, n_seg, seg, tps, carried):
    if carried:
        (h2_ref, xm_ref, wa_ref, wg_ref, cwa_ref, cwg_ref, cba_ref, cbg_ref, wd_ref,
         out_ref, ta_ref, tg_ref, z_ref, ca_ref, cg_ref) = refs
        hists = (None, None)
        carries = (ca_ref, cg_ref)
    else:
        (h2_ref, xm_ref, wa_ref, wg_ref, cwa_ref, cwg_ref, cba_ref, cbg_ref, wd_ref, ha_ref, hg_ref,
         out_ref, ta_ref, tg_ref, z_ref) = refs
        hists = (ha_ref, hg_ref)
        carries = (None, None)
    i = pl.program_id(0)
    j = pl.program_id(1)
    tf = wa_ref.shape[1]
    d = out_ref.shape[1]
    tm = n_seg * seg
    col0 = pl.multiple_of(j * tf, tf)

    xw = xm_ref.shape[1]

    @pl.when(j == 0)
    def _():
        out_ref[:, 0:xw] = xm_ref[...]
        out_ref[:, xw:d] = jnp.zeros((tm, d - xw), F32)

    @pl.when((j > 0) & (j < d // xw))
    def _():
        out_ref[:, pl.ds(pl.multiple_of(j * xw, xw), xw)] += xm_ref[...]

    if carried:
        @pl.when(i % tps == 0)
        def _():
            for carry_ref in carries:
                carry_ref[0:2, pl.ds(col0, tf)] = jnp.zeros((2, tf), F32)

    unit = tm // sum(SUB_SPLIT)
    starts = [unit * sum(SUB_SPLIT[:n]) for n in range(len(SUB_SPLIT) + 1)]
    halves = ((wa_ref, cwa_ref, cba_ref, ta_ref, carries[0], hists[0]),
              (wg_ref, cwg_ref, cbg_ref, tg_ref, carries[1], hists[1]))
    ups = [[jnp.dot(h2_ref[lo:hi, :], w_ref[...], preferred_element_type=F32)
            for w_ref, *_ in halves] for lo, hi in zip(starts[:-1], starts[1:])]

    row = lax.broadcasted_iota(jnp.int32, (ROW_BLOCK, tf), 0)
    taps = [[jnp.broadcast_to(cw_ref[t:t + 1, :], (ROW_BLOCK, tf)) for t in range(3)]
            for _, cw_ref, *_ in halves]
    bias = [jnp.broadcast_to(cb_ref[...], (ROW_BLOCK, tf)) for _, _, cb_ref, *_ in halves]
    prev = [None, None]

    def conv_block(x, r0):
        _, _, _, tail_ref, carry_ref, hist_ref = halves[x]
        s, off = divmod(r0, seg)
        if off == 0:
            if carried:
                h0 = carry_ref[0:1, pl.ds(col0, tf)]
                h1 = carry_ref[1:2, pl.ds(col0, tf)]
            else:
                h0 = hist_ref[s, 0:1, :]
                h1 = hist_ref[s, 1:2, :]
            prev[x] = (jnp.broadcast_to(h1, (ROW_BLOCK, tf)), jnp.where(row == 0, h0, h1))
        si = max(n for n, lo in enumerate(starts[:-1]) if lo <= r0)
        cur = ups[si][x][r0 - starts[si]:r0 - starts[si] + ROW_BLOCK]
        rot1 = pltpu.roll(cur, 1, 0)
        rot2 = pltpu.roll(cur, 2, 0)
        um1 = jnp.where(row == 0, prev[x][0], rot1)
        um2 = jnp.where(row <= 1, prev[x][1], rot2)
        prev[x] = (rot1, rot2)
        if off == seg - ROW_BLOCK:
            tail_ref[s] = cur[ROW_BLOCK - 2:ROW_BLOCK, :]
        if carried and r0 == tm - ROW_BLOCK:
            carry_ref[0:2, pl.ds(col0, tf)] = cur[ROW_BLOCK - 2:ROW_BLOCK, :]
        return bias[x] + (um2 * taps[x][0] + um1 * taps[x][1] + cur * taps[x][2])

    pack = 2 * ROW_BLOCK
    for lo, hi in zip(starts[:-1], starts[1:]):
        for r0 in range(lo, hi, pack):
            a = jnp.concatenate([conv_block(0, r0), conv_block(0, r0 + ROW_BLOCK)], axis=0)
            g = jnp.concatenate([conv_block(1, r0), conv_block(1, r0 + ROW_BLOCK)], axis=0)
            z_ref[r0:r0 + pack, :] = ((g * _sigmoid(g)) * a).astype(BF16)
        z = z_ref[lo:hi, :]
        for c0 in range(0, d, DOWN_CHUNK):
            out_ref[lo:hi, c0:c0 + DOWN_CHUNK] += jnp.dot(
                z, wd_ref[:, c0:c0 + DOWN_CHUNK], preferred_element_type=F32)


def _convffn(h2, xm, w_up, conv_w, conv_b, w_down, hist, *, batch, seq):
    tokens, d = xm.shape
    f = w_down.shape[0]
    tf = min(FFN_WIDTH_TILE, f)
    nf = f // tf
    tm = min(FFN_TOKEN_TILE, tokens)
    seg = min(seq, tm)
    n_seg = tm // seg
    tps = seq // seg
    carried = hist is None
    assert not carried or n_seg == 1, "a carried conv state needs one sequence per token tile"
    row = lambda i, j: (i, 0)
    a_col = lambda i, j: (0, j)
    g_col = lambda i, j: (0, nf + j)
    n_slabs = 1
    while 2 * n_slabs <= nf and d % (2 * n_slabs * LANES) == 0:
        n_slabs *= 2
    in_specs = [pl.BlockSpec((tm, d), row),
                pl.BlockSpec((tm, d // n_slabs), lambda i, j: (i, jnp.minimum(j, n_slabs - 1))),
                pl.BlockSpec((d, tf), a_col), pl.BlockSpec((d, tf), g_col),
                pl.BlockSpec((3, tf), a_col), pl.BlockSpec((3, tf), g_col),
                pl.BlockSpec((1, tf), a_col), pl.BlockSpec((1, tf), g_col),
                pl.BlockSpec((tf, d), lambda i, j: (j, 0))]
    args = [h2, xm, w_up, w_up, conv_w, conv_w, conv_b, conv_b, w_down]
    scratch = [pltpu.VMEM((tm, tf), BF16)]
    if carried:
        scratch += [pltpu.VMEM((8, f), F32), pltpu.VMEM((8, f), F32)]
    else:
        in_specs += [pl.BlockSpec((n_seg, 2, tf), lambda i, j: (i, 0, j)),
                     pl.BlockSpec((n_seg, 2, tf), lambda i, j: (i, 0, nf + j))]
        args += [hist, hist]
    tail_spec = pl.BlockSpec((n_seg, 2, tf), lambda i, j: (i, 0, j))
    tail_shape = jax.ShapeDtypeStruct((batch * tps, 2, f), F32)
    out, tails_a, tails_g = pl.pallas_call(
        functools.partial(_convffn_kernel, n_seg=n_seg, seg=seg, tps=tps, carried=carried),
        grid=(tokens // tm, nf),
        in_specs=in_specs,
        out_specs=[pl.BlockSpec((tm, d), row), tail_spec, tail_spec],
        out_shape=[jax.ShapeDtypeStruct((tokens, d), F32), tail_shape, tail_shape],
        scratch_shapes=scratch,
        compiler_params=_params(2),
    )(*args)
    last = lambda t: t.reshape(batch, tps, 2, f)[:, tps - 1]
    return out, jnp.concatenate([last(tails_a), last(tails_g)], axis=-1)


def _rope_tables(pos, rows):
    half = LANES // 2
    inv = np.power(np.float32(ROPE_BASE), -np.arange(half, dtype=np.float32) / np.float32(half))
    ang = pos.astype(np.float32)[:, None] * inv[None, :]
    cos = np.cos(ang)
    sin = np.sin(ang)
    cos_t = np.concatenate([cos, cos], axis=1)
    sin_t = np.concatenate([-sin, sin], axis=1)
    reps = rows // pos.shape[0]
    return jnp.asarray(np.tile(cos_t, (reps, 1))), jnp.asarray(np.tile(sin_t, (reps, 1)))


def kernel(x_prompt, x_sample, cache_fox_k, cache_fox_v, cache_fox_logf, state_ret, state_conv,
           ln1, w_in, b_f, fox_qn, fox_kn, ret_gn, w_out, ln2, w_up, conv_w, conv_b, w_down):
    depth = ln1.shape[0]
    bp, tp, d = x_prompt.shape
    bs, ts, _ = x_sample.shape
    past = cache_fox_k.shape[2]
    n_heads, hd = cache_fox_k.shape[3], cache_fox_k.shape[4]
    d_head = n_heads * hd
    assert hd == LANES and n_heads % HEADS_PER_STEP == 0 and ret_gn.shape[1:] == (n_heads, hd)

    cos_p, sin_p = _rope_tables(np.arange(tp), max(tp, min(INPROJ_TOKEN_TILE, bp * tp)))
    cos_s, sin_s = _rope_tables(past + np.arange(ts), max(ts, min(INPROJ_TOKEN_TILE, bs * ts)))
    cache_k = cache_fox_k.reshape(depth * bs, past * n_heads, hd)
    cache_v = cache_fox_v.reshape(depth * bs, past * n_heads, hd)
    state_r = state_ret.reshape(depth * bs, n_heads, hd, hd)
    zero_state = jnp.zeros((bp, n_heads, hd, hd), F32)

    xp = x_prompt.reshape(bp * tp, d)
    xs = x_sample.reshape(bs * ts, d)
    st_p, st_s = [], []
    for l in range(depth):
        r0 = N_FOX_KINDS * d_head + n_heads
        w_fox = w_in[l][:, :N_FOX_KINDS * d_head].astype(BF16)
        w_ret = w_in[l][:, r0:].astype(BF16)
        wf = jnp.pad(w_in[l][:, N_FOX_KINDS * d_head:r0], ((0, 0), (0, hd - n_heads))).astype(BF16)
        bfp = jnp.pad(b_f[l], (0, hd - n_heads)).reshape(1, hd)
        qn = fox_qn[l].reshape(1, hd)
        kn = fox_kn[l].reshape(1, hd)
        wu = w_up[l].astype(BF16)
        wd = w_down[l].astype(BF16)
        cw = conv_w[l]
        cb = conv_b[l].reshape(1, -1)
        ln1_l = ln1[l].reshape(1, d)
        ln2_l = ln2[l].reshape(1, d)

        def group(x, cos_t, sin_t, batch, seq):
            return _inproj(x, ln1_l, w_fox, w_ret, wf, bfp, qn, kn, cos_t, sin_t,
                           batch=batch, seq=seq, n_heads=n_heads)

        fq, fkb, fvb, rq, rk, rv, rg, fk, fv, lf = group(xp, cos_p, sin_p, bp, tp)
        c_row = _cumsum(lf.reshape(bp, tp, n_heads).transpose(0, 2, 1))
        o_fox = _fox_prompt(fq, fkb, fvb, c_row, qn, kn)
        o_ret, s_ret_p = _retention(rq, rk, rv, rg, zero_state, ret_gn[l], s0_block0=0)
        xm, h2 = _outproj(o_fox, o_ret, xp, w_out[l], ln2_l)
        xp, conv_p = _convffn(h2, xm, wu, cw, cb, wd, None, batch=bp, seq=tp)
        st_p.append((fk.reshape(bp, tp, n_heads, hd), fv.reshape(bp, tp, n_heads, hd),
                     lf.reshape(bp, tp, n_heads), s_ret_p, conv_p))

        fq, fkb, fvb, rq, rk, rv, rg, fk, fv, lf = group(xs, cos_s, sin_s, bs, ts)
        lf_all = jnp.concatenate([cache_fox_logf[l], lf.reshape(bs, ts, n_heads)], axis=1)
        pad = (-lf_all.shape[1]) % LANES
        lf_all = jnp.pad(lf_all, ((0, 0), (0, pad), (0, 0)))
        c_row = _cumsum(lf_all.transpose(0, 2, 1))
        o_fox = _fox_sample(fq, fkb, fvb, cache_k, cache_v, c_row, layer=l)
        o_ret, s_ret_s = _retention(rq, rk, rv, rg, state_r, ret_gn[l], s0_block0=l * bs)
        xm, h2 = _outproj(o_fox, o_ret, xs, w_out[l], ln2_l)
        xs, conv_s = _convffn(h2, xm, wu, cw, cb, wd, state_conv[l], batch=bs, seq=ts)
        st_s.append((fk.reshape(bs, ts, n_heads, hd), fv.reshape(bs, ts, n_heads, hd),
                     lf.reshape(bs, ts, n_heads), s_ret_s, conv_s))

    stack = lambda st, k: jnp.stack([s[k] for s in st])
    return (xp.reshape(bp, tp, d), xs.reshape(bs, ts, d),
            stack(st_p, 0), stack(st_p, 1), stack(st_p, 2), stack(st_p, 3), stack(st_p, 4),
            stack(st_s, 0), stack(st_s, 1), stack(st_s, 2), stack(st_s, 3), stack(st_s, 4))
```

```python
import functools
import math

import jax
import jax.numpy as jnp
import numpy as np
from jax import lax
from jax.experimental import pallas as pl
from jax.experimental.pallas import tpu as pltpu

EPS = 1e-6
ROPE_BASE = 10000.0
MASKED_LOGIT = -1e30
LANES = 128
VMEM_LIMIT_BYTES = 56 * 1024 * 1024
LOG2E = math.log2(math.e)
FOX_BOUNDED_LIMIT = 100.0
F32_EXP2_UNDERFLOW = 152.0

F32 = jnp.float32
BF16 = jnp.bfloat16

N_FOX_KINDS = 3
N_RET_KINDS = 4
HEADS_PER_STEP = 2
INPROJ_TOKEN_TILE = 1024
CUMSUM_ROWS = 64
FOX_HEADS_PER_STEP = 4
FOX_QUERY_TILE = 512
FOX_KEY_TILE = 256

NT_DIMS = (((1,), (1,)), ((), ()))
TN_DIMS = (((0,), (0,)), ((), ()))


def _params(n_axes):
    return pltpu.CompilerParams(dimension_semantics=("arbitrary",) * n_axes,
                                vmem_limit_bytes=VMEM_LIMIT_BYTES)


def _resident(block_shape, index_map):
    return pl.BlockSpec(block_shape, index_map, pipeline_mode=pl.Buffered(1))


def _rms(x):
    return x * lax.rsqrt(jnp.mean(x * x, axis=-1, keepdims=True) + EPS)


def _sigmoid(x):
    return 1.0 / (1.0 + jnp.exp(-x))


def _inproj_kernel(x_ref, ln1_ref, wfq_ref, wfk_ref, wfv_ref, wrq_ref, wrk_ref, wrv_ref, wrg_ref,
                   wf_ref, bf_ref, qn_ref, kn_ref, cos_ref, sin_ref,
                   fq_ref, fkb_ref, fvb_ref, rq_ref, rk_ref, rv_ref, rg_ref, fk_ref, fv_ref, logf_ref,
                   h_ref, *, nb, seg, n_fox):
    j = pl.program_id(1)

    @pl.when(j == 0)
    def _():
        h = (_rms(x_ref[...]) * ln1_ref[...]).astype(BF16)
        h_ref[...] = h
        z = jnp.dot(h, wf_ref[...], preferred_element_type=F32) + bf_ref[...]
        logf = jnp.minimum(z, 0.0) - jnp.log1p(jnp.exp(-jnp.abs(z)))
        logf_ref[...] = logf[:, :n_fox]

    cos = cos_ref[...]
    sin = sin_ref[...]
    hd = LANES

    def project(w_ref):
        p = jnp.dot(h_ref[...], w_ref[...], preferred_element_type=F32)
        return [p[:, e * hd:(e + 1) * hd] for e in range(HEADS_PER_STEP)]

    def head_major(a):
        return a.reshape(nb, seg, hd).astype(BF16)

    def rope(a):
        return a * cos + pltpu.roll(a, hd // 2, 1) * sin

    for e, a in enumerate(project(wfq_ref)):
        fq_ref[:, e] = head_major(_rms(a) * qn_ref[...])
    for e, a in enumerate(project(wfk_ref)):
        fk = _rms(a) * kn_ref[...]
        fk_ref[:, e * hd:(e + 1) * hd] = fk
        fkb_ref[:, e] = head_major(fk)
    for e, a in enumerate(project(wfv_ref)):
        fv_ref[:, e * hd:(e + 1) * hd] = a
        fvb_ref[:, e] = head_major(a)
    for e, a in enumerate(project(wrq_ref)):
        rq_ref[:, e] = head_major(rope(a))
    for e, a in enumerate(project(wrk_ref)):
        rk_ref[:, e] = head_major(rope(a) * (hd ** -0.5))
    for e, a in enumerate(project(wrv_ref)):
        rv_ref[:, e] = head_major(a)
    for e, a in enumerate(project(wrg_ref)):
        rg_ref[:, e] = head_major(a)


def _inproj(x, ln1, w_fox, w_ret, wf, bfp, qn, kn, cos_t, sin_t, *, batch, seq, n_heads):
    tokens, d = x.shape
    hd = LANES
    tm = min(INPROJ_TOKEN_TILE, tokens)
    seg = min(seq, tm)
    nb = tm // seg
    tps = seq // seg
    n_groups = n_heads // HEADS_PER_STEP
    gcols = HEADS_PER_STEP * hd
    grid = (tokens // tm, n_groups)

    def w_spec(kind):
        return pl.BlockSpec((d, gcols), lambda i, j: (0, kind * n_groups + j))

    hm_shape = jax.ShapeDtypeStruct((batch, n_heads, seq, hd), BF16)
    hm_spec = pl.BlockSpec((nb, HEADS_PER_STEP, seg, hd), lambda i, j: (i // tps, j, i % tps, 0))
    tok_shape = jax.ShapeDtypeStruct((tokens, n_heads * hd), F32)
    tok_spec = pl.BlockSpec((tm, gcols), lambda i, j: (i, j))

    return pl.pallas_call(
        functools.partial(_inproj_kernel, nb=nb, seg=seg, n_fox=n_heads),
        grid=grid,
        in_specs=[pl.BlockSpec((tm, d), lambda i, j: (i, 0)),
                  _resident((1, d), lambda i, j: (0, 0))]
                 + [w_spec(k) for k in range(N_FOX_KINDS)] + [w_spec(k) for k in range(N_RET_KINDS)]
                 + [_resident((d, hd), lambda i, j: (0, 0)),
                    _resident((1, hd), lambda i, j: (0, 0)),
                    _resident((1, hd), lambda i, j: (0, 0)),
                    _resident((1, hd), lambda i, j: (0, 0)),
                    pl.BlockSpec((tm, hd), lambda i, j: (i % tps, 0)),
                    pl.BlockSpec((tm, hd), lambda i, j: (i % tps, 0))],
        out_specs=[hm_spec] * 7 + [tok_spec, tok_spec,
                                   pl.BlockSpec((tm, n_heads), lambda i, j: (i, 0))],
        out_shape=[hm_shape] * 7 + [tok_shape, tok_shape,
                                    jax.ShapeDtypeStruct((tokens, n_heads), F32)],
        scratch_shapes=[pltpu.VMEM((tm, d), BF16)],
        compiler_params=_params(2),
    )(x, ln1, w_fox, w_fox, w_fox, w_ret, w_ret, w_ret, w_ret, wf, bfp, qn, kn, cos_t, sin_t)


def _cumsum_kernel(lf_ref, c_ref, *, ch):
    rows, total = lf_ref.shape
    r = lax.broadcasted_iota(jnp.int32, (ch, ch), 0)
    c = lax.broadcasted_iota(jnp.int32, (ch, ch), 1)
    upper = (r <= c).astype(F32)
    carry = jnp.zeros((rows, 1), F32)
    for k in range(total // ch):
        x = lf_ref[:, k * ch:(k + 1) * ch]
        cs = jnp.dot(x, upper, precision=lax.Precision.HIGHEST, preferred_element_type=F32) + carry
        c_ref[:, k * ch:(k + 1) * ch] = cs
        carry = cs[:, ch - 1:ch]


def _cumsum(lf_t):
    batch, n_heads, total = lf_t.shape
    rows = batch * n_heads
    band = min(CUMSUM_ROWS, rows)
    ch = 256 if total % 256 == 0 else LANES
    spec = pl.BlockSpec((band, total), lambda b: (b, 0))
    out = pl.pallas_call(
        functools.partial(_cumsum_kernel, ch=ch),
        grid=(rows // band,), in_specs=[spec], out_specs=spec,
        out_shape=jax.ShapeDtypeStruct((rows, total), F32),
        compiler_params=_params(1),
    )(lf_t.reshape(rows, total))
    return out.reshape(batch, n_heads, total)


def _softmax_step(z, v, m_ref, l_ref, acc_ref):
    tk = z.shape[1]
    m_prev = m_ref[...]
    m_next = jnp.maximum(m_prev, jnp.max(z, axis=1, keepdims=True))
    p = jnp.exp2(z - pltpu.repeat(m_next, tk // LANES, 1))
    alpha = jnp.exp2(m_prev - m_next)
    l_ref[...] = alpha * l_ref[...] + jnp.sum(p, axis=1, keepdims=True)
    acc_ref[...] = alpha * acc_ref[...] + jnp.dot(p.astype(BF16), v, preferred_element_type=F32)
    m_ref[...] = m_next


def _fox_prompt_kernel(first_ref, q_ref, k_ref, v_ref, c_ref, o_ref, m_ref, l_ref, acc_ref, *, tq, tk, scale):
    b, hg, qi = pl.program_id(0), pl.program_id(1), pl.program_id(2)
    group, hd = q_ref.shape[1], q_ref.shape[3]
    m_ref[...] = jnp.full(m_ref.shape, MASKED_LOGIT, F32)
    l_ref[...] = jnp.zeros(l_ref.shape, F32)
    acc_ref[...] = jnp.zeros(acc_ref.shape, F32)

    def logits(e, k0):
        k = k_ref[0, e, pl.ds(k0, tk), :]
        s = lax.dot_general(q_ref[0, e], k, NT_DIMS, preferred_element_type=F32)
        return s * (scale * LOG2E) - c_ref[0, e, :, pl.ds(k0, tk)] * LOG2E

    def body(j, carry):
        k0 = pl.multiple_of(j * tk, tk)
        for e in range(group):
            _softmax_step(logits(e, k0), v_ref[0, e, pl.ds(k0, tk), :],
                          m_ref.at[e], l_ref.at[e], acc_ref.at[e])
        return carry

    first = first_ref[(b * pl.num_programs(1) + hg) * pl.num_programs(2) + qi]
    lax.fori_loop(first, qi * (tq // tk), body, 0)

    row = lax.broadcasted_iota(jnp.int32, (tq, tk), 0)
    col = lax.broadcasted_iota(jnp.int32, (tq, tk), 1)
    for e in range(group):
        for off in range(0, tq, tk):
            k0 = pl.multiple_of(qi * tq + off, tk)
            z = jnp.where(col + off <= row, logits(e, k0), MASKED_LOGIT)
            _softmax_step(z, v_ref[0, e, pl.ds(k0, tk), :], m_ref.at[e], l_ref.at[e], acc_ref.at[e])
        o_ref[:, e * hd:(e + 1) * hd] = (acc_ref[e] / l_ref[e]).astype(o_ref.dtype)


def _fox_prompt_bounded_kernel(first_ref, q_ref, k_ref, v_ref, c2_ref, r2_ref, o_ref, l_ref, acc_ref, *, tq, tk, scale):
    b, hg, qi = pl.program_id(0), pl.program_id(1), pl.program_id(2)
    group, hd = q_ref.shape[1], q_ref.shape[3]
    l_ref[...] = jnp.zeros(l_ref.shape, F32)
    acc_ref[...] = jnp.zeros(acc_ref.shape, F32)
    q0 = pl.multiple_of(qi * tq, tq)
    row_q = lax.broadcasted_iota(jnp.int32, (tq, tq), 0)
    col_q = lax.broadcasted_iota(jnp.int32, (tq, tq), 1)
    row_terms = [jnp.sum(jnp.where(row_q == col_q, r2_ref[0, e, :, pl.ds(q0, tq)], 0.0), axis=1, keepdims=True)
                 for e in range(group)]
    row = lax.broadcasted_iota(jnp.int32, (tq, tk), 0)
    col = lax.broadcasted_iota(jnp.int32, (tq, tk), 1)

    def step(e, k0, diag_offset=None):
        k = k_ref[0, e, pl.ds(k0, tk), :]
        s = lax.dot_general(q_ref[0, e], k, NT_DIMS, preferred_element_type=F32)
        z = (s * (scale * LOG2E) - c2_ref[0, e, :, pl.ds(k0, tk)]) + row_terms[e]
        if diag_offset is not None:
            z = jnp.where(col + diag_offset <= row, z, MASKED_LOGIT)
        p = jnp.exp2(z)
        part = p[:, 0:LANES]
        for c0 in range(LANES, tk, LANES):
            part = part + p[:, c0:c0 + LANES]
        l_ref[e] += part
        acc_ref[e] += jnp.dot(p.astype(BF16), v_ref[0, e, pl.ds(k0, tk), :], preferred_element_type=F32)

    def body(j, carry):
        k0 = pl.multiple_of(j * tk, tk)
        for e in range(group):
            step(e, k0)
        return carry

    first = first_ref[(b * pl.num_programs(1) + hg) * pl.num_programs(2) + qi]
    lax.fori_loop(first, qi * (tq // tk), body, 0)
    for e in range(group):
        for off in range(0, tq, tk):
            step(e, pl.multiple_of(q0 + off, tk), off)
        l = jnp.sum(l_ref[e], axis=1, keepdims=True)
        o_ref[:, e * hd:(e + 1) * hd] = (acc_ref[e] / l).astype(o_ref.dtype)


def _qk_logit_bound(qn, kn, hd, scale):
    bf16_slack = (1.0 + 2.0 ** -8) ** 2
    return (hd * scale * LOG2E * bf16_slack) * jnp.max(jnp.abs(qn)) * jnp.max(jnp.abs(kn))


def _first_key_tiles(c_row, bound, *, tq, tk, group):
    batch, n_heads, seq = c_row.shape
    c_first_row = c_row[:, :, 0::tq]
    c_last_key = c_row[:, :, tk - 1::tk]
    gap = (c_last_key[:, :, None, :] - c_first_row[:, :, :, None]) * LOG2E
    nq = seq // tq
    earlier = (jnp.arange(seq // tk)[None, :] + 1) * tk <= jnp.arange(nq)[:, None] * tq
    dead = (gap > 2.0 * bound + F32_EXP2_UNDERFLOW) & earlier
    first = jnp.sum(dead, axis=-1).astype(jnp.int32)
    return jnp.min(first.reshape(batch, n_heads // group, group, nq), axis=2).reshape(-1)


def _fox_prompt(q, k, v, c_row, qn, kn):
    batch, n_heads, seq, hd = q.shape
    tq = min(FOX_QUERY_TILE, seq)
    tk = min(FOX_KEY_TILE, tq)
    nq = seq // tq
    group = FOX_HEADS_PER_STEP
    scale = hd ** -0.5
    bound = _qk_logit_bound(qn, kn, hd, scale)
    first = _first_key_tiles(c_row, bound, tq=tq, tk=tk, group=group)
    kv_spec = pl.BlockSpec((1, group, seq, hd), lambda b, h, i, first: (b, h, 0, 0))
    row_spec = pl.BlockSpec((1, group, 1, seq), lambda b, h, i, first: (b, h, 0, 0))
    q_spec = pl.BlockSpec((1, group, tq, hd), lambda b, h, i, first: (b, h, i, 0))
    out_spec = pl.BlockSpec((tq, group * hd), lambda b, h, i, first: (b * nq + i, h))
    out_shape = jax.ShapeDtypeStruct((batch * seq, n_heads * hd), BF16)
    stats = pltpu.VMEM((group, tq, LANES), F32)
    acc = pltpu.VMEM((group, tq, hd), F32)

    def running_max(c4):
        return pl.pallas_call(
            functools.partial(_fox_prompt_kernel, tq=tq, tk=tk, scale=scale),
            grid_spec=pltpu.PrefetchScalarGridSpec(
                num_scalar_prefetch=1, grid=(batch, n_heads // group, nq),
                in_specs=[q_spec, kv_spec, kv_spec, row_spec], out_specs=out_spec,
                scratch_shapes=[stats, stats, acc]),
            out_shape=out_shape, compiler_params=_params(3),
        )(first, q, k, v, c4)

    def bounded(c4):
        return pl.pallas_call(
            functools.partial(_fox_prompt_bounded_kernel, tq=tq, tk=tk, scale=scale),
            grid_spec=pltpu.PrefetchScalarGridSpec(
                num_scalar_prefetch=1, grid=(batch, n_heads // group, nq),
                in_specs=[q_spec, kv_spec, kv_spec, row_spec, row_spec], out_specs=out_spec,
                scratch_shapes=[stats, acc]),
            out_shape=out_shape, compiler_params=_params(3),
        )(first, q, k, v, c4 * LOG2E, c4 * LOG2E - bound)

    return lax.cond(2.0 * bound <= FOX_BOUNDED_LIMIT, bounded, running_max,
                    c_row.reshape(batch, n_heads, 1, seq))


def _fox_sample_kernel(q_ref, kn_ref, vn_ref, kc_ref, vc_ref, c_ref, o_ref, *, past, scale):
    n_heads, new, hd = q_ref.shape[1:]
    row = lax.broadcasted_iota(jnp.int32, (new, new), 0)
    col = lax.broadcasted_iota(jnp.int32, (new, new), 1)
    for h in range(n_heads):
        q = q_ref[0, h]
        kc = kc_ref[0, pl.ds(h, past, stride=n_heads), :].astype(BF16)
        vc = vc_ref[0, pl.ds(h, past, stride=n_heads), :].astype(BF16)
        z_c = (lax.dot_general(q, kc, NT_DIMS, preferred_element_type=F32) * scale
               - c_ref[0, h:h + 1, 0:past])
        z_n = (lax.dot_general(q, kn_ref[0, h], NT_DIMS, preferred_element_type=F32) * scale
               - c_ref[0, h:h + 1, past:past + new])
        z_n = jnp.where(col <= row, z_n, MASKED_LOGIT)
        m = jnp.maximum(jnp.max(z_c, axis=1, keepdims=True), jnp.max(z_n, axis=1, keepdims=True))
        p_c = jnp.exp(z_c - m)
        p_n = jnp.exp(z_n - m)
        l = jnp.sum(p_c, axis=1, keepdims=True) + jnp.sum(p_n, axis=1, keepdims=True)
        acc = (jnp.dot(p_c.astype(BF16), vc, preferred_element_type=F32)
               + jnp.dot(p_n.astype(BF16), vn_ref[0, h], preferred_element_type=F32))
        o_ref[:, h * hd:(h + 1) * hd] = (acc / l).astype(o_ref.dtype)


def _fox_sample(q, k_new, v_new, cache_k, cache_v, c_row, *, layer):
    batch, n_heads, new, hd = q.shape
    past = cache_k.shape[1] // n_heads
    new_spec = pl.BlockSpec((1, n_heads, new, hd), lambda b: (b, 0, 0, 0))
    cache_spec = pl.BlockSpec((1, past * n_heads, hd), lambda b: (layer * batch + b, 0, 0))
    return pl.pallas_call(
        functools.partial(_fox_sample_kernel, past=past, scale=hd ** -0.5),
        grid=(batch,),
        in_specs=[new_spec, new_spec, new_spec, cache_spec, cache_spec,
                  pl.BlockSpec((1, n_heads, c_row.shape[2]), lambda b: (b, 0, 0))],
        out_specs=pl.BlockSpec((new, n_heads * hd), lambda b: (b, 0)),
        out_shape=jax.ShapeDtypeStruct((batch * new, n_heads * hd), BF16),
        compiler_params=_params(1),
    )(q, k_new, v_new, cache_k, cache_v, c_row)


def _retention_kernel(q_ref, k_ref, v_ref, g_ref, s0_ref, gn_ref, o_ref, s_out_ref,
                      s_ref, decay_ref, inter_ref, upd_ref, *, log_gammas):
    b = pl.program_id(0)
    c = pl.program_id(1)
    n_heads, chunk, hd = q_ref.shape[1:]

    @pl.when((b == 0) & (c == 0))
    def _():
        i = lax.broadcasted_iota(jnp.int32, (chunk, chunk), 0)
        jj = lax.broadcasted_iota(jnp.int32, (chunk, chunk), 1)
        diff = (i - jj).astype(F32)
        pos = lax.broadcasted_iota(jnp.int32, (chunk, hd), 0).astype(F32)
        for h, lg in enumerate(log_gammas):
            decay_ref[h] = jnp.where(diff >= 0.0, jnp.exp(jnp.maximum(diff, 0.0) * lg), 0.0)
            inter_ref[h] = jnp.exp((pos + 1.0) * lg)
            upd_ref[h] = jnp.exp((chunk - 1.0 - pos) * lg)

    @pl.when(c == 0)
    def _():
        s_ref[...] = s0_ref[0]

    for h, lg in enumerate(log_gammas):
        q = q_ref[0, h]
        k = k_ref[0, h]
        v = v_ref[0, h]
        s_prev = s_ref[h]
        scores = lax.dot_general(q, k, NT_DIMS, preferred_element_type=F32) * decay_ref[h]
        o = jnp.dot(scores.astype(BF16), v, preferred_element_type=F32)
        o = o + jnp.dot(q, s_prev.astype(BF16), preferred_element_type=F32) * inter_ref[h]
        kd = (k.astype(F32) * upd_ref[h]).astype(BF16)
        s_ref[h] = math.exp(chunk * lg) * s_prev + lax.dot_general(kd, v, TN_DIMS, preferred_element_type=F32)
        g = g_ref[0, h].astype(F32)
        y = _rms(o) * gn_ref[h:h + 1, :]
        o_ref[:, h * hd:(h + 1) * hd] = (y * (g * _sigmoid(g))).astype(o_ref.dtype)

    @pl.when(c == pl.num_programs(1) - 1)
    def _():
        s_out_ref[0] = s_ref[...]


def _retention(q, k, v, g, s0, gn, *, s0_block0):
    batch, n_heads, seq, hd = q.shape
    chunk = min(256, seq)
    nc = seq // chunk
    log_gammas = tuple(math.log(1.0 - 2.0 ** (-5.0 - h)) for h in range(n_heads))
    qkv_spec = pl.BlockSpec((1, n_heads, chunk, hd), lambda b, c: (b, 0, c, 0))
    return pl.pallas_call(
        functools.partial(_retention_kernel, log_gammas=log_gammas),
        grid=(batch, nc),
        in_specs=[qkv_spec] * 4 + [pl.BlockSpec((1, n_heads, hd, hd), lambda b, c: (s0_block0 + b, 0, 0, 0)),
                                   _resident((n_heads, hd), lambda b, c: (0, 0))],
        out_specs=[pl.BlockSpec((chunk, n_heads * hd), lambda b, c: (b * nc + c, 0)),
                   pl.BlockSpec((1, n_heads, hd, hd), lambda b, c: (b, 0, 0, 0))],
        out_shape=[jax.ShapeDtypeStruct((batch * seq, n_heads * hd), BF16),
                   jax.ShapeDtypeStruct((batch, n_heads, hd, hd), F32)],
        scratch_shapes=[pltpu.VMEM((n_heads, hd, hd), F32), pltpu.VMEM((n_heads, chunk, chunk), F32),
                        pltpu.VMEM((n_heads, chunk, hd), F32), pltpu.VMEM((n_heads, chunk, hd), F32)],
        compiler_params=_params(2),
    )(q, k, v, g, s0, gn)


def _outproj_kernel(of_ref, or_ref, x_ref, w_ref, ln2_ref, xm_ref, h2_ref, wb_ref):
    @pl.when(pl.program_id(0) == 0)
    def _():
        wb_ref[...] = w_ref[...].astype(BF16)

    half = of_ref.shape[1]
    y = jnp.dot(of_ref[...], wb_ref[0:half, :], preferred_element_type=F32)
    y = y + jnp.dot(or_ref[...], wb_ref[half:2 * half, :], preferred_element_type=F32)
    xm = x_ref[...] + y
    xm_ref[...] = xm
    h2_ref[...] = (_rms(xm) * ln2_ref[...]).astype(BF16)


def _outproj(o_fox, o_ret, x, w_out, ln2):
    tokens, d = x.shape
    half = o_fox.shape[1]
    tm = min(512, tokens)
    row = lambda i: (i, 0)
    return pl.pallas_call(
        _outproj_kernel,
        grid=(tokens // tm,),
        in_specs=[pl.BlockSpec((tm, half), row), pl.BlockSpec((tm, half), row),
                  pl.BlockSpec((tm, d), row),
                  _resident((2 * half, d), lambda i: (0, 0)), _resident((1, d), lambda i: (0, 0))],
        out_specs=[pl.BlockSpec((tm, d), row), pl.BlockSpec((tm, d), row)],
        out_shape=[jax.ShapeDtypeStruct((tokens, d), F32), jax.ShapeDtypeStruct((tokens, d), BF16)],
        scratch_shapes=[pltpu.VMEM((2 * half, d), BF16)],
        compiler_params=_params(1),
    )(o_fox, o_ret, x, w_out, ln2)


ROW_BLOCK = 8
DOWN_CHUNK = 512
SUB_SPLIT = (1, 1)
FFN_TOKEN_TILE = 1024
FFN_WIDTH_TILE = 512


def _convffn_kernel(*refs, n_seg, seg, tps, carried):
    if carried:
        (h2_ref, xm_ref, wa_ref, wg_ref, cwa_ref, cwg_ref, cba_ref, cbg_ref, wd_ref,
         out_ref, ta_ref, tg_ref, z_ref, ca_ref, cg_ref) = refs
        hists = (None, None)
        carries = (ca_ref, cg_ref)
    else:
        (h2_ref, xm_ref, wa_ref, wg_ref, cwa_ref, cwg_ref, cba_ref, cbg_ref, wd_ref, ha_ref, hg_ref,
         out_ref, ta_ref, tg_ref, z_ref) = refs
        hists = (ha_ref, hg_ref)
        carries = (None, None)
    i = pl.program_id(0)
    j = pl.program_id(1)
    tf = wa_ref.shape[1]
    d = out_ref.shape[1]
    tm = n_seg * seg
    col0 = pl.multiple_of(j * tf, tf)

    xw = xm_ref.shape[1]

    @pl.when(j == 0)
    def _():
        out_ref[:, 0:xw] = xm_ref[...]
        out_ref[:, xw:d] = jnp.zeros((tm, d - xw), F32)

    @pl.when((j > 0) & (j < d // xw))
    def _():
        out_ref[:, pl.ds(pl.multiple_of(j * xw, xw), xw)] += xm_ref[...]

    if carried:
        @pl.when(i % tps == 0)
        def _():
            for carry_ref in carries:
                carry_ref[0:2, pl.ds(col0, tf)] = jnp.zeros((2, tf), F32)

    unit = tm // sum(SUB_SPLIT)
    starts = [unit * sum(SUB_SPLIT[:n]) for n in range(len(SUB_SPLIT) + 1)]
    halves = ((wa_ref, cwa_ref, cba_ref, ta_ref, carries[0], hists[0]),
              (wg_ref, cwg_ref, cbg_ref, tg_ref, carries[1], hists[1]))
    ups = [[jnp.dot(h2_ref[lo:hi, :], w_ref[...], preferred_element_type=F32)
            for w_ref, *_ in halves] for lo, hi in zip(starts[:-1], starts[1:])]

    row = lax.broadcasted_iota(jnp.int32, (ROW_BLOCK, tf), 0)
    taps = [[jnp.broadcast_to(cw_ref[t:t + 1, :], (ROW_BLOCK, tf)) for t in range(3)]
            for _, cw_ref, *_ in halves]
    bias = [jnp.broadcast_to(cb_ref[...], (ROW_BLOCK, tf)) for _, _, cb_ref, *_ in halves]
    prev = [None, None]

    def conv_block(x, r0):
        _, _, _, tail_ref, carry_ref, hist_ref = halves[x]
        s, off = divmod(r0, seg)
        if off == 0:
            if carried:
                h0 = carry_ref[0:1, pl.ds(col0, tf)]
                h1 = carry_ref[1:2, pl.ds(col0, tf)]
            else:
                h0 = hist_ref[s, 0:1, :]
                h1 = hist_ref[s, 1:2, :]
            prev[x] = (jnp.broadcast_to(h1, (ROW_BLOCK, tf)), jnp.where(row == 0, h0, h1))
        si = max(n for n, lo in enumerate(starts[:-1]) if lo <= r0)
        cur = ups[si][x][r0 - starts[si]:r0 - starts[si] + ROW_BLOCK]
        rot1 = pltpu.roll(cur, 1, 0)
        rot2 = pltpu.roll(cur, 2, 0)
        um1 = jnp.where(row == 0, prev[x][0], rot1)
        um2 = jnp.where(row <= 1, prev[x][1], rot2)
        prev[x] = (rot1, rot2)
        if off == seg - ROW_BLOCK:
            tail_ref[s] = cur[ROW_BLOCK - 2:ROW_BLOCK, :]
        if carried and r0 == tm - ROW_BLOCK:
            carry_ref[0:2, pl.ds(col0, tf)] = cur[ROW_BLOCK - 2:ROW_BLOCK, :]
        return bias[x] + (um2 * taps[x][0] + um1 * taps[x][1] + cur * taps[x][2])

    pack = 2 * ROW_BLOCK
    for lo, hi in zip(starts[:-1], starts[1:]):
        for r0 in range(lo, hi, pack):
            a = jnp.concatenate([conv_block(0, r0), conv_block(0, r0 + ROW_BLOCK)], axis=0)
            g = jnp.concatenate([conv_block(1, r0), conv_block(1, r0 + ROW_BLOCK)], axis=0)
            z_ref[r0:r0 + pack, :] = ((g * _sigmoid(g)) * a).astype(BF16)
        z = z_ref[lo:hi, :]
        for c0 in range(0, d, DOWN_CHUNK):
            out_ref[lo:hi, c0:c0 + DOWN_CHUNK] += jnp.dot(
                z, wd_ref[:, c0:c0 + DOWN_CHUNK], preferred_element_type=F32)


def _convffn(h2, xm, w_up, conv_w, conv_b, w_down, hist, *, batch, seq):
    tokens, d = xm.shape
    f = w_down.shape[0]
    tf = min(FFN_WIDTH_TILE, f)
    nf = f // tf
    tm = min(FFN_TOKEN_TILE, tokens)
    seg = min(seq, tm)
    n_seg = tm // seg
    tps = seq // seg
    carried = hist is None
    assert not carried or n_seg == 1, "a carried conv state needs one sequence per token tile"
    row = lambda i, j: (i, 0)
    a_col = lambda i, j: (0, j)
    g_col = lambda i, j: (0, nf + j)
    n_slabs = 1
    while 2 * n_slabs <= nf and d % (2 * n_slabs * LANES) == 0:
        n_slabs *= 2
    in_specs = [pl.BlockSpec((tm, d), row),
                pl.BlockSpec((tm, d // n_slabs), lambda i, j: (i, jnp.minimum(j, n_slabs - 1))),
                pl.BlockSpec((d, tf), a_col), pl.BlockSpec((d, tf), g_col),
                pl.BlockSpec((3, tf), a_col), pl.BlockSpec((3, tf), g_col),
                pl.BlockSpec((1, tf), a_col), pl.BlockSpec((1, tf), g_col),
                pl.BlockSpec((tf, d), lambda i, j: (j, 0))]
    args = [h2, xm, w_up, w_up, conv_w, conv_w, conv_b, conv_b, w_down]
    scratch = [pltpu.VMEM((tm, tf), BF16)]
    if carried:
        scratch += [pltpu.VMEM((8, f), F32), pltpu.VMEM((8, f), F32)]
    else:
        in_specs += [pl.BlockSpec((n_seg, 2, tf), lambda i, j: (i, 0, j)),
                     pl.BlockSpec((n_seg, 2, tf), lambda i, j: (i, 0, nf + j))]
        args += [hist, hist]
    tail_spec = pl.BlockSpec((n_seg, 2, tf), lambda i, j: (i, 0, j))
    tail_shape = jax.ShapeDtypeStruct((batch * tps, 2, f), F32)
    out, tails_a, tails_g = pl.pallas_call(
        functools.partial(_convffn_kernel, n_seg=n_seg, seg=seg, tps=tps, carried=carried),
        grid=(tokens // tm, nf),
        in_specs=in_specs,
        out_specs=[pl.BlockSpec((tm, d), row), tail_spec, tail_spec],
        out_shape=[jax.ShapeDtypeStruct((tokens, d), F32), tail_shape, tail_shape],
        scratch_shapes=scratch,
        compiler_params=_params(2),
    )(*args)
    last = lambda t: t.reshape(batch, tps, 2, f)[:, tps - 1]
    return out, jnp.concatenate([last(tails_a), last(tails_g)], axis=-1)


def _rope_tables(pos, rows):
    half = LANES // 2
    inv = np.power(np.float32(ROPE_BASE), -np.arange(half, dtype=np.float32) / np.float32(half))
    ang = pos.astype(np.float32)[:, None] * inv[None, :]
    cos = np.cos(ang)
    sin = np.sin(ang)
    cos_t = np.concatenate([cos, cos], axis=1)
    sin_t = np.concatenate([-sin, sin], axis=1)
    reps = rows // pos.shape[0]
    return jnp.asarray(np.tile(cos_t, (reps, 1))), jnp.asarray(np.tile(sin_t, (reps, 1)))


def kernel(x_prompt, x_sample, cache_fox_k, cache_fox_v, cache_fox_logf, state_ret, state_conv,
           ln1, w_in, b_f, fox_qn, fox_kn, ret_gn, w_out, ln2, w_up, conv_w, conv_b, w_down):
    depth = ln1.shape[0]
    bp, tp, d = x_prompt.shape
    bs, ts, _ = x_sample.shape
    past = cache_fox_k.shape[2]
    n_heads, hd = cache_fox_k.shape[3], cache_fox_k.shape[4]
    d_head = n_heads * hd
    assert hd == LANES and n_heads % HEADS_PER_STEP == 0 and ret_gn.shape[1:] == (n_heads, hd)

    cos_p, sin_p = _rope_tables(np.arange(tp), max(tp, min(INPROJ_TOKEN_TILE, bp * tp)))
    cos_s, sin_s = _rope_tables(past + np.arange(ts), max(ts, min(INPROJ_TOKEN_TILE, bs * ts)))
    cache_k = cache_fox_k.reshape(depth * bs, past * n_heads, hd)
    cache_v = cache_fox_v.reshape(depth * bs, past * n_heads, hd)
    state_r = state_ret.reshape(depth * bs, n_heads, hd, hd)
    zero_state = jnp.zeros((bp, n_heads, hd, hd), F32)

    xp = x_prompt.reshape(bp * tp, d)
    xs = x_sample.reshape(bs * ts, d)
    st_p, st_s = [], []
    for l in range(depth):
        r0 = N_FOX_KINDS * d_head + n_heads
        w_fox = w_in[l][:, :N_FOX_KINDS * d_head].astype(BF16)
        w_ret = w_in[l][:, r0:].astype(BF16)
        wf = jnp.pad(w_in[l][:, N_FOX_KINDS * d_head:r0], ((0, 0), (0, hd - n_heads))).astype(BF16)
        bfp = jnp.pad(b_f[l], (0, hd - n_heads)).reshape(1, hd)
        qn = fox_qn[l].reshape(1, hd)
        kn = fox_kn[l].reshape(1, hd)
        wu = w_up[l].astype(BF16)
        wd = w_down[l].astype(BF16)
        cw = conv_w[l]
        cb = conv_b[l].reshape(1, -1)
        ln1_l = ln1[l].reshape(1, d)
        ln2_l = ln2[l].reshape(1, d)

        def group(x, cos_t, sin_t, batch, seq):
            return _inproj(x, ln1_l, w_fox, w_ret, wf, bfp, qn, kn, cos_t, sin_t,
                           batch=batch, seq=seq, n_heads=n_heads)

        fq, fkb, fvb, rq, rk, rv, rg, fk, fv, lf = group(xp, cos_p, sin_p, bp, tp)
        c_row = _cumsum(lf.reshape(bp, tp, n_heads).transpose(0, 2, 1))
        o_fox = _fox_prompt(fq, fkb, fvb, c_row, qn, kn)
        o_ret, s_ret_p = _retention(rq, rk, rv, rg, zero_state, ret_gn[l], s0_block0=0)
        xm, h2 = _outproj(o_fox, o_ret, xp, w_out[l], ln2_l)
        xp, conv_p = _convffn(h2, xm, wu, cw, cb, wd, None, batch=bp, seq=tp)
        st_p.append((fk.reshape(bp, tp, n_heads, hd), fv.reshape(bp, tp, n_heads, hd),
                     lf.reshape(bp, tp, n_heads), s_ret_p, conv_p))

        fq, fkb, fvb, rq, rk, rv, rg, fk, fv, lf = group(xs, cos_s, sin_s, bs, ts)
        lf_all = jnp.concatenate([cache_fox_logf[l], lf.reshape(bs, ts, n_heads)], axis=1)
        pad = (-lf_all.shape[1]) % LANES
        lf_all = jnp.pad(lf_all, ((0, 0), (0, pad), (0, 0)))
        c_row = _cumsum(lf_all.transpose(0, 2, 1))
        o_fox = _fox_sample(fq, fkb, fvb, cache_k, cache_v, c_row, layer=l)
        o_ret, s_ret_s = _retention(rq, rk, rv, rg, state_r, ret_gn[l], s0_block0=l * bs)
        xm, h2 = _outproj(o_fox, o_ret, xs, w_out[l], ln2_l)
        xs, conv_s = _convffn(h2, xm, wu, cw, cb, wd, state_conv[l], batch=bs, seq=ts)
        st_s.append((fk.reshape(bs, ts, n_heads, hd), fv.reshape(bs, ts, n_heads, hd),
                     lf.reshape(bs, ts, n_heads), s_ret_s, conv_s))

    stack = lambda st, k: jnp.stack([s[k] for s in st])
    return (xp.reshape(bp, tp, d), xs.reshape(bs, ts, d),
            stack(st_p, 0), stack(st_p, 1), stack(st_p, 2), stack(st_p, 3), stack(st_p, 4),
            stack(st_s, 0), stack(st_s, 1), stack(st_s, 2), stack(st_s, 3), stack(st_s, 4))
```

```python
import functools
import math

import jax
import jax.numpy as jnp
from jax import lax
from jax.experimental import pallas as pl
from jax.experimental.pallas import tpu as pltpu

EPS = 1e-6
ROPE_BASE = 10000.0
MASKED_LOGIT = -1e30
LANES = 128
VMEM_LIMIT_BYTES = 56 * 1024 * 1024
LOG2E = math.log2(math.e)
FOX_BOUNDED_LIMIT = 100.0
F32_EXP2_UNDERFLOW = 152.0

F32 = jnp.float32
BF16 = jnp.bfloat16

N_FOX_KINDS = 3
N_RET_KINDS = 4
HEADS_PER_STEP = 2
INPROJ_TOKEN_TILE = 1024
CUMSUM_ROWS = 64
FOX_HEADS_PER_STEP = 4
FOX_QUERY_TILE = 512
FOX_KEY_TILE = 256

NT_DIMS = (((1,), (1,)), ((), ()))
TN_DIMS = (((0,), (0,)), ((), ()))


def _params(n_axes):
    return pltpu.CompilerParams(dimension_semantics=("arbitrary",) * n_axes,
                                vmem_limit_bytes=VMEM_LIMIT_BYTES)


def _resident(block_shape, index_map):
    return pl.BlockSpec(block_shape, index_map, pipeline_mode=pl.Buffered(1))


def _rms(x):
    return x * lax.rsqrt(jnp.mean(x * x, axis=-1, keepdims=True) + EPS)


def _sigmoid(x):
    return 1.0 / (1.0 + jnp.exp(-x))


def _inproj_kernel(x_ref, ln1_ref, wfq_ref, wfk_ref, wfv_ref, wrq_ref, wrk_ref, wrv_ref, wrg_ref,
                   wf_ref, bf_ref, qn_ref, kn_ref, cos_ref, sin_ref,
                   fq_ref, fkb_ref, fvb_ref, rq_ref, rk_ref, rv_ref, rg_ref, fk_ref, fv_ref, logf_ref,
                   h_ref, cosf_ref, sinf_ref, *, nb, seg, n_fox):
    j = pl.program_id(1)

    @pl.when(j == 0)
    def _():
        h = (_rms(x_ref[...]) * ln1_ref[...]).astype(BF16)
        h_ref[...] = h
        z = jnp.dot(h, wf_ref[...], preferred_element_type=F32) + bf_ref[...]
        logf = jnp.minimum(z, 0.0) - jnp.log1p(jnp.exp(-jnp.abs(z)))
        logf_ref[...] = logf[:, :n_fox]
        cos_half = cos_ref[...]
        sin_half = sin_ref[...]
        cosf_ref[...] = jnp.concatenate([cos_half, cos_half], axis=1)
        sinf_ref[...] = jnp.concatenate([-sin_half, sin_half], axis=1)

    cos = cosf_ref[...]
    sin = sinf_ref[...]
    hd = LANES

    def project(w_ref):
        p = jnp.dot(h_ref[...], w_ref[...], preferred_element_type=F32)
        return [p[:, e * hd:(e + 1) * hd] for e in range(HEADS_PER_STEP)]

    def head_major(a):
        return a.reshape(nb, seg, hd).astype(BF16)

    def rope(a):
        return a * cos + pltpu.roll(a, hd // 2, 1) * sin

    for e, a in enumerate(project(wfq_ref)):
        fq_ref[:, e] = head_major(_rms(a) * qn_ref[...])
    for e, a in enumerate(project(wfk_ref)):
        fk = _rms(a) * kn_ref[...]
        fk_ref[:, e * hd:(e + 1) * hd] = fk
        fkb_ref[:, e] = head_major(fk)
    for e, a in enumerate(project(wfv_ref)):
        fv_ref[:, e * hd:(e + 1) * hd] = a
        fvb_ref[:, e] = head_major(a)
    for e, a in enumerate(project(wrq_ref)):
        rq_ref[:, e] = head_major(rope(a))
    for e, a in enumerate(project(wrk_ref)):
        rk_ref[:, e] = head_major(rope(a) * (hd ** -0.5))
    for e, a in enumerate(project(wrv_ref)):
        rv_ref[:, e] = head_major(a)
    for e, a in enumerate(project(wrg_ref)):
        rg_ref[:, e] = head_major(a)


def _inproj(x, ln1, w_fox, w_ret, wf, bfp, qn, kn, cos_t, sin_t, *, batch, seq, n_heads):
    tokens, d = x.shape
    hd = LANES
    tm = min(INPROJ_TOKEN_TILE, tokens)
    seg = min(seq, tm)
    nb = tm // seg
    tps = seq // seg
    n_groups = n_heads // HEADS_PER_STEP
    gcols = HEADS_PER_STEP * hd
    grid = (tokens // tm, n_groups)

    def w_spec(kind):
        return pl.BlockSpec((d, gcols), lambda i, j: (0, kind * n_groups + j))

    hm_shape = jax.ShapeDtypeStruct((batch, n_heads, seq, hd), BF16)
    hm_spec = pl.BlockSpec((nb, HEADS_PER_STEP, seg, hd), lambda i, j: (i // tps, j, i % tps, 0))
    tok_shape = jax.ShapeDtypeStruct((tokens, n_heads * hd), F32)
    tok_spec = pl.BlockSpec((tm, gcols), lambda i, j: (i, j))

    return pl.pallas_call(
        functools.partial(_inproj_kernel, nb=nb, seg=seg, n_fox=n_heads),
        grid=grid,
        in_specs=[pl.BlockSpec((tm, d), lambda i, j: (i, 0)),
                  _resident((1, d), lambda i, j: (0, 0))]
                 + [w_spec(k) for k in range(N_FOX_KINDS)] + [w_spec(k) for k in range(N_RET_KINDS)]
                 + [_resident((d, hd), lambda i, j: (0, 0)),
                    _resident((1, hd), lambda i, j: (0, 0)),
                    _resident((1, hd), lambda i, j: (0, 0)),
                    _resident((1, hd), lambda i, j: (0, 0)),
                    pl.BlockSpec((tm, hd // 2), lambda i, j: (i % tps, 0)),
                    pl.BlockSpec((tm, hd // 2), lambda i, j: (i % tps, 0))],
        out_specs=[hm_spec] * 7 + [tok_spec, tok_spec,
                                   pl.BlockSpec((tm, n_heads), lambda i, j: (i, 0))],
        out_shape=[hm_shape] * 7 + [tok_shape, tok_shape,
                                    jax.ShapeDtypeStruct((tokens, n_heads), F32)],
        scratch_shapes=[pltpu.VMEM((tm, d), BF16), pltpu.VMEM((tm, hd), F32), pltpu.VMEM((tm, hd), F32)],
        compiler_params=_params(2),
    )(x, ln1, w_fox, w_fox, w_fox, w_ret, w_ret, w_ret, w_ret, wf, bfp, qn, kn, cos_t, sin_t)


def _cumsum_kernel(lf_ref, c_ref, *, ch):
    rows, total = lf_ref.shape
    r = lax.broadcasted_iota(jnp.int32, (ch, ch), 0)
    c = lax.broadcasted_iota(jnp.int32, (ch, ch), 1)
    upper = (r <= c).astype(F32)
    carry = jnp.zeros((rows, 1), F32)
    for k in range(total // ch):
        x = lf_ref[:, k * ch:(k + 1) * ch]
        cs = jnp.dot(x, upper, precision=lax.Precision.HIGHEST, preferred_element_type=F32) + carry
        c_ref[:, k * ch:(k + 1) * ch] = cs
        carry = cs[:, ch - 1:ch]


def _cumsum(lf_t):
    batch, n_heads, total = lf_t.shape
    rows = batch * n_heads
    band = min(CUMSUM_ROWS, rows)
    ch = 256 if total % 256 == 0 else LANES
    spec = pl.BlockSpec((band, total), lambda b: (b, 0))
    out = pl.pallas_call(
        functools.partial(_cumsum_kernel, ch=ch),
        grid=(rows // band,), in_specs=[spec], out_specs=spec,
        out_shape=jax.ShapeDtypeStruct((rows, total), F32),
        compiler_params=_params(1),
    )(lf_t.reshape(rows, total))
    return out.reshape(batch, n_heads, total)


def _softmax_step(z, v, m_ref, l_ref, acc_ref):
    tk = z.shape[1]
    m_prev = m_ref[...]
    m_next = jnp.maximum(m_prev, jnp.max(z, axis=1, keepdims=True))
    p = jnp.exp2(z - pltpu.repeat(m_next, tk // LANES, 1))
    alpha = jnp.exp2(m_prev - m_next)
    l_ref[...] = alpha * l_ref[...] + jnp.sum(p, axis=1, keepdims=True)
    acc_ref[...] = alpha * acc_ref[...] + jnp.dot(p.astype(BF16), v, preferred_element_type=F32)
    m_ref[...] = m_next


def _fox_prompt_kernel(first_ref, q_ref, k_ref, v_ref, c_ref, o_ref, m_ref, l_ref, acc_ref, *, tq, tk, scale):
    b, hg, qi = pl.program_id(0), pl.program_id(1), pl.program_id(2)
    group, hd = q_ref.shape[1], q_ref.shape[3]
    m_ref[...] = jnp.full(m_ref.shape, MASKED_LOGIT, F32)
    l_ref[...] = jnp.zeros(l_ref.shape, F32)
    acc_ref[...] = jnp.zeros(acc_ref.shape, F32)

    def logits(e, k0):
        k = k_ref[0, e, pl.ds(k0, tk), :]
        s = lax.dot_general(q_ref[0, e], k, NT_DIMS, preferred_element_type=F32)
        return s * (scale * LOG2E) - c_ref[0, e, :, pl.ds(k0, tk)] * LOG2E

    def body(j, carry):
        k0 = pl.multiple_of(j * tk, tk)
        for e in range(group):
            _softmax_step(logits(e, k0), v_ref[0, e, pl.ds(k0, tk), :],
                          m_ref.at[e], l_ref.at[e], acc_ref.at[e])
        return carry

    first = first_ref[(b * pl.num_programs(1) + hg) * pl.num_programs(2) + qi]
    lax.fori_loop(first, qi * (tq // tk), body, 0)

    row = lax.broadcasted_iota(jnp.int32, (tq, tk), 0)
    col = lax.broadcasted_iota(jnp.int32, (tq, tk), 1)
    for e in range(group):
        for off in range(0, tq, tk):
            k0 = pl.multiple_of(qi * tq + off, tk)
            z = jnp.where(col + off <= row, logits(e, k0), MASKED_LOGIT)
            _softmax_step(z, v_ref[0, e, pl.ds(k0, tk), :], m_ref.at[e], l_ref.at[e], acc_ref.at[e])
        o_ref[:, e * hd:(e + 1) * hd] = (acc_ref[e] / l_ref[e]).astype(o_ref.dtype)


def _fox_prompt_bounded_kernel(first_ref, q_ref, k_ref, v_ref, c2_ref, r2_ref, o_ref, l_ref, acc_ref, *, tq, tk, scale):
    b, hg, qi = pl.program_id(0), pl.program_id(1), pl.program_id(2)
    group, hd = q_ref.shape[1], q_ref.shape[3]
    l_ref[...] = jnp.zeros(l_ref.shape, F32)
    acc_ref[...] = jnp.zeros(acc_ref.shape, F32)
    q0 = pl.multiple_of(qi * tq, tq)
    row_q = lax.broadcasted_iota(jnp.int32, (tq, tq), 0)
    col_q = lax.broadcasted_iota(jnp.int32, (tq, tq), 1)
    row_terms = [jnp.sum(jnp.where(row_q == col_q, r2_ref[0, e, :, pl.ds(q0, tq)], 0.0), axis=1, keepdims=True)
                 for e in range(group)]
    row = lax.broadcasted_iota(jnp.int32, (tq, tk), 0)
    col = lax.broadcasted_iota(jnp.int32, (tq, tk), 1)

    def step(e, k0, diag_offset=None):
        k = k_ref[0, e, pl.ds(k0, tk), :]
        s = lax.dot_general(q_ref[0, e], k, NT_DIMS, preferred_element_type=F32)
        z = (s * (scale * LOG2E) - c2_ref[0, e, :, pl.ds(k0, tk)]) + row_terms[e]
        if diag_offset is not None:
            z = jnp.where(col + diag_offset <= row, z, MASKED_LOGIT)
        p = jnp.exp2(z)
        part = p[:, 0:LANES]
        for c0 in range(LANES, tk, LANES):
            part = part + p[:, c0:c0 + LANES]
        l_ref[e] += part
        acc_ref[e] += jnp.dot(p.astype(BF16), v_ref[0, e, pl.ds(k0, tk), :], preferred_element_type=F32)

    def body(j, carry):
        k0 = pl.multiple_of(j * tk, tk)
        for e in range(group):
            step(e, k0)
        return carry

    first = first_ref[(b * pl.num_programs(1) + hg) * pl.num_programs(2) + qi]
    lax.fori_loop(first, qi * (tq // tk), body, 0)
    for e in range(group):
        for off in range(0, tq, tk):
            step(e, pl.multiple_of(q0 + off, tk), off)
        l = jnp.sum(l_ref[e], axis=1, keepdims=True)
        o_ref[:, e * hd:(e + 1) * hd] = (acc_ref[e] / l).astype(o_ref.dtype)


def _qk_logit_bound(qn, kn, hd, scale):
    bf16_slack = (1.0 + 2.0 ** -8) ** 2
    return (hd * scale * LOG2E * bf16_slack) * jnp.max(jnp.abs(qn)) * jnp.max(jnp.abs(kn))


def _first_key_tiles(c_row, bound, *, tq, tk, group):
    batch, n_heads, seq = c_row.shape
    c_first_row = c_row[:, :, 0::tq]
    c_last_key = c_row[:, :, tk - 1::tk]
    gap = (c_last_key[:, :, None, :] - c_first_row[:, :, :, None]) * LOG2E
    nq = seq // tq
    earlier = (jnp.arange(seq // tk)[None, :] + 1) * tk <= jnp.arange(nq)[:, None] * tq
    dead = (gap > 2.0 * bound + F32_EXP2_UNDERFLOW) & earlier
    first = jnp.sum(dead, axis=-1).astype(jnp.int32)
    return jnp.min(first.reshape(batch, n_heads // group, group, nq), axis=2).reshape(-1)


def _fox_prompt(q, k, v, c_row, qn, kn):
    batch, n_heads, seq, hd = q.shape
    tq = min(FOX_QUERY_TILE, seq)
    tk = min(FOX_KEY_TILE, tq)
    nq = seq // tq
    group = FOX_HEADS_PER_STEP
    scale = hd ** -0.5
    bound = _qk_logit_bound(qn, kn, hd, scale)
    first = _first_key_tiles(c_row, bound, tq=tq, tk=tk, group=group)
    kv_spec = pl.BlockSpec((1, group, seq, hd), lambda b, h, i, first: (b, h, 0, 0))
    row_spec = pl.BlockSpec((1, group, 1, seq), lambda b, h, i, first: (b, h, 0, 0))
    q_spec = pl.BlockSpec((1, group, tq, hd), lambda b, h, i, first: (b, h, i, 0))
    out_spec = pl.BlockSpec((tq, group * hd), lambda b, h, i, first: (b * nq + i, h))
    out_shape = jax.ShapeDtypeStruct((batch * seq, n_heads * hd), BF16)
    stats = pltpu.VMEM((group, tq, LANES), F32)
    acc = pltpu.VMEM((group, tq, hd), F32)

    def running_max(c4):
        return pl.pallas_call(
            functools.partial(_fox_prompt_kernel, tq=tq, tk=tk, scale=scale),
            grid_spec=pltpu.PrefetchScalarGridSpec(
                num_scalar_prefetch=1, grid=(batch, n_heads // group, nq),
                in_specs=[q_spec, kv_spec, kv_spec, row_spec], out_specs=out_spec,
                scratch_shapes=[stats, stats, acc]),
            out_shape=out_shape, compiler_params=_params(3),
        )(first, q, k, v, c4)

    def bounded(c4):
        return pl.pallas_call(
            functools.partial(_fox_prompt_bounded_kernel, tq=tq, tk=tk, scale=scale),
            grid_spec=pltpu.PrefetchScalarGridSpec(
                num_scalar_prefetch=1, grid=(batch, n_heads // group, nq),
                in_specs=[q_spec, kv_spec, kv_spec, row_spec, row_spec], out_specs=out_spec,
                scratch_shapes=[stats, acc]),
            out_shape=out_shape, compiler_params=_params(3),
        )(first, q, k, v, c4 * LOG2E, c4 * LOG2E - bound)

    return lax.cond(2.0 * bound <= FOX_BOUNDED_LIMIT, bounded, running_max,
                    c_row.reshape(batch, n_heads, 1, seq))


def _fox_sample_kernel(q_ref, kn_ref, vn_ref, kc_ref, vc_ref, c_ref, o_ref, *, past, scale):
    n_heads, new, hd = q_ref.shape[1:]
    row = lax.broadcasted_iota(jnp.int32, (new, new), 0)
    col = lax.broadcasted_iota(jnp.int32, (new, new), 1)
    for h in range(n_heads):
        q = q_ref[0, h]
        kc = kc_ref[0, pl.ds(h, past, stride=n_heads), :].astype(BF16)
        vc = vc_ref[0, pl.ds(h, past, stride=n_heads), :].astype(BF16)
        z_c = (lax.dot_general(q, kc, NT_DIMS, preferred_element_type=F32) * scale
               - c_ref[0, h:h + 1, 0:past])
        z_n = (lax.dot_general(q, kn_ref[0, h], NT_DIMS, preferred_element_type=F32) * scale
               - c_ref[0, h:h + 1, past:past + new])
        z_n = jnp.where(col <= row, z_n, MASKED_LOGIT)
        m = jnp.maximum(jnp.max(z_c, axis=1, keepdims=True), jnp.max(z_n, axis=1, keepdims=True))
        p_c = jnp.exp(z_c - m)
        p_n = jnp.exp(z_n - m)
        l = jnp.sum(p_c, axis=1, keepdims=True) + jnp.sum(p_n, axis=1, keepdims=True)
        acc = (jnp.dot(p_c.astype(BF16), vc, preferred_element_type=F32)
               + jnp.dot(p_n.astype(BF16), vn_ref[0, h], preferred_element_type=F32))
        o_ref[:, h * hd:(h + 1) * hd] = (acc / l).astype(o_ref.dtype)


def _fox_sample(q, k_new, v_new, cache_k, cache_v, c_row, *, layer):
    batch, n_heads, new, hd = q.shape
    past = cache_k.shape[1] // n_heads
    new_spec = pl.BlockSpec((1, n_heads, new, hd), lambda b: (b, 0, 0, 0))
    cache_spec = pl.BlockSpec((1, past * n_heads, hd), lambda b: (layer * batch + b, 0, 0))
    return pl.pallas_call(
        functools.partial(_fox_sample_kernel, past=past, scale=hd ** -0.5),
        grid=(batch,),
        in_specs=[new_spec, new_spec, new_spec, cache_spec, cache_spec,
                  pl.BlockSpec((1, n_heads, c_row.shape[2]), lambda b: (b, 0, 0))],
        out_specs=pl.BlockSpec((new, n_heads * hd), lambda b: (b, 0)),
        out_shape=jax.ShapeDtypeStruct((batch * new, n_heads * hd), BF16),
        compiler_params=_params(1),
    )(q, k_new, v_new, cache_k, cache_v, c_row)


def _retention_kernel(q_ref, k_ref, v_ref, g_ref, s0_ref, gn_ref, lg_ref, o_ref, s_out_ref,
                      s_ref, decay_ref, inter_ref, upd_ref, keep_ref):
    b = pl.program_id(0)
    c = pl.program_id(1)
    n_heads, chunk, hd = q_ref.shape[1:]

    @pl.when((b == 0) & (c == 0))
    def _():
        i = lax.broadcasted_iota(jnp.int32, (chunk, chunk), 0)
        jj = lax.broadcasted_iota(jnp.int32, (chunk, chunk), 1)
        diff = (i - jj).astype(F32)
        pos = lax.broadcasted_iota(jnp.int32, (chunk, hd), 0).astype(F32)
        for h in range(n_heads):
            lg = lg_ref[h:h + 1, :]
            decay_ref[h] = jnp.where(diff >= 0.0, jnp.exp(jnp.maximum(diff, 0.0) * lg[:, 0:1]), 0.0)
            inter_ref[h] = jnp.exp((pos + 1.0) * lg)
            upd_ref[h] = jnp.exp((chunk - 1.0 - pos) * lg)
            keep_ref[h:h + 1, :] = jnp.exp(chunk * lg)

    @pl.when(c == 0)
    def _():
        s_ref[...] = s0_ref[0]

    for h in range(n_heads):
        q = q_ref[0, h]
        k = k_ref[0, h]
        v = v_ref[0, h]
        s_prev = s_ref[h]
        scores = lax.dot_general(q, k, NT_DIMS, preferred_element_type=F32) * decay_ref[h]
        o = jnp.dot(scores.astype(BF16), v, preferred_element_type=F32)
        o = o + jnp.dot(q, s_prev.astype(BF16), preferred_element_type=F32) * inter_ref[h]
        kd = (k.astype(F32) * upd_ref[h]).astype(BF16)
        s_ref[h] = keep_ref[h:h + 1, :] * s_prev + lax.dot_general(kd, v, TN_DIMS, preferred_element_type=F32)
        g = g_ref[0, h].astype(F32)
        y = _rms(o) * gn_ref[h:h + 1, :]
        o_ref[:, h * hd:(h + 1) * hd] = (y * (g * _sigmoid(g))).astype(o_ref.dtype)

    @pl.when(c == pl.num_programs(1) - 1)
    def _():
        s_out_ref[0] = s_ref[...]


def _retention(q, k, v, g, s0, gn, *, s0_block0):
    batch, n_heads, seq, hd = q.shape
    chunk = min(256, seq)
    nc = seq // chunk
    log_gamma = jnp.log(1.0 - jnp.power(2.0, -5.0 - jnp.arange(n_heads, dtype=F32)))
    lg = jnp.broadcast_to(log_gamma[:, None], (n_heads, hd))
    qkv_spec = pl.BlockSpec((1, n_heads, chunk, hd), lambda b, c: (b, 0, c, 0))
    return pl.pallas_call(
        _retention_kernel,
        grid=(batch, nc),
        in_specs=[qkv_spec] * 4 + [pl.BlockSpec((1, n_heads, hd, hd), lambda b, c: (s0_block0 + b, 0, 0, 0)),
                                   _resident((n_heads, hd), lambda b, c: (0, 0)),
                                   _resident((n_heads, hd), lambda b, c: (0, 0))],
        out_specs=[pl.BlockSpec((chunk, n_heads * hd), lambda b, c: (b * nc + c, 0)),
                   pl.BlockSpec((1, n_heads, hd, hd), lambda b, c: (b, 0, 0, 0))],
        out_shape=[jax.ShapeDtypeStruct((batch * seq, n_heads * hd), BF16),
                   jax.ShapeDtypeStruct((batch, n_heads, hd, hd), F32)],
        scratch_shapes=[pltpu.VMEM((n_heads, hd, hd), F32), pltpu.VMEM((n_heads, chunk, chunk), F32),
                        pltpu.VMEM((n_heads, chunk, hd), F32), pltpu.VMEM((n_heads, chunk, hd), F32),
                        pltpu.VMEM((n_heads, hd), F32)],
        compiler_params=_params(2),
    )(q, k, v, g, s0, gn, lg)


def _outproj_kernel(of_ref, or_ref, x_ref, w_ref, ln2_ref, xm_ref, h2_ref, wb_ref):
    @pl.when(pl.program_id(0) == 0)
    def _():
        wb_ref[...] = w_ref[...].astype(BF16)

    half = of_ref.shape[1]
    y = jnp.dot(of_ref[...], wb_ref[0:half, :], preferred_element_type=F32)
    y = y + jnp.dot(or_ref[...], wb_ref[half:2 * half, :], preferred_element_type=F32)
    xm = x_ref[...] + y
    xm_ref[...] = xm
    h2_ref[...] = (_rms(xm) * ln2_ref[...]).astype(BF16)


def _outproj(o_fox, o_ret, x, w_out, ln2):
    tokens, d = x.shape
    half = o_fox.shape[1]
    tm = min(512, tokens)
    row = lambda i: (i, 0)
    return pl.pallas_call(
        _outproj_kernel,
        grid=(tokens // tm,),
        in_specs=[pl.BlockSpec((tm, half), row), pl.BlockSpec((tm, half), row),
                  pl.BlockSpec((tm, d), row),
                  _resident((2 * half, d), lambda i: (0, 0)), _resident((1, d), lambda i: (0, 0))],
        out_specs=[pl.BlockSpec((tm, d), row), pl.BlockSpec((tm, d), row)],
        out_shape=[jax.ShapeDtypeStruct((tokens, d), F32), jax.ShapeDtypeStruct((tokens, d), BF16)],
        scratch_shapes=[pltpu.VMEM((2 * half, d), BF16)],
        compiler_params=_params(1),
    )(o_fox, o_ret, x, w_out, ln2)


ROW_BLOCK = 8
DOWN_CHUNK = 512
SUB_SPLIT = (1, 1)
FFN_TOKEN_TILE = 1024
FFN_WIDTH_TILE = 512


def _convffn_kernel(*refs, n_seg, seg, tps, carried):
    if carried:
        (h2_ref, xm_ref, wa_ref, wg_ref, cwa_ref, cwg_ref, cba_ref, cbg_ref, wd_ref,
         out_ref, ta_ref, tg_ref, z_ref, ca_ref, cg_ref) = refs
        hists = (None, None)
        carries = (ca_ref, cg_ref)
    else:
        (h2_ref, xm_ref, wa_ref, wg_ref, cwa_ref, cwg_ref, cba_ref, cbg_ref, wd_ref, ha_ref, hg_ref,
         out_ref, ta_ref, tg_ref, z_ref) = refs
        hists = (ha_ref, hg_ref)
        carries = (None, None)
    i = pl.program_id(0)
    j = pl.program_id(1)
    tf = wa_ref.shape[1]
    d = out_ref.shape[1]
    tm = n_seg * seg
    col0 = pl.multiple_of(j * tf, tf)

    xw = xm_ref.shape[1]

    @pl.when(j == 0)
    def _():
        out_ref[:, 0:xw] = xm_ref[...]
        out_ref[:, xw:d] = jnp.zeros((tm, d - xw), F32)

    @pl.when((j > 0) & (j < d // xw))
    def _():
        out_ref[:, pl.ds(pl.multiple_of(j * xw, xw), xw)] += xm_ref[...]

    if carried:
        @pl.when(i % tps == 0)
        def _():
            for carry_ref in carries:
                carry_ref[0:2, pl.ds(col0, tf)] = jnp.zeros((2, tf), F32)

    unit = tm // sum(SUB_SPLIT)
    starts = [unit * sum(SUB_SPLIT[:n]) for n in range(len(SUB_SPLIT) + 1)]
    halves = ((wa_ref, cwa_ref, cba_ref, ta_ref, carries[0], hists[0]),
              (wg_ref, cwg_ref, cbg_ref, tg_ref, carries[1], hists[1]))
    ups = [[jnp.dot(h2_ref[lo:hi, :], w_ref[...], preferred_element_type=F32)
            for w_ref, *_ in halves] for lo, hi in zip(starts[:-1], starts[1:])]

    row = lax.broadcasted_iota(jnp.int32, (ROW_BLOCK, tf), 0)
    taps = [[jnp.broadcast_to(cw_ref[t:t + 1, :], (ROW_BLOCK, tf)) for t in range(3)]
            for _, cw_ref, *_ in halves]
    bias = [jnp.broadcast_to(cb_ref[...], (ROW_BLOCK, tf)) for _, _, cb_ref, *_ in halves]
    prev = [None, None]

    def conv_block(x, r0):
        _, _, _, tail_ref, carry_ref, hist_ref = halves[x]
        s, off = divmod(r0, seg)
        if off == 0:
            if carried:
                h0 = carry_ref[0:1, pl.ds(col0, tf)]
                h1 = carry_ref[1:2, pl.ds(col0, tf)]
            else:
                h0 = hist_ref[s, 0:1, :]
                h1 = hist_ref[s, 1:2, :]
            prev[x] = (jnp.broadcast_to(h1, (ROW_BLOCK, tf)), jnp.where(row == 0, h0, h1))
        si = max(n for n, lo in enumerate(starts[:-1]) if lo <= r0)
        cur = ups[si][x][r0 - starts[si]:r0 - starts[si] + ROW_BLOCK]
        rot1 = pltpu.roll(cur, 1, 0)
        rot2 = pltpu.roll(cur, 2, 0)
        um1 = jnp.where(row == 0, prev[x][0], rot1)
        um2 = jnp.where(row <= 1, prev[x][1], rot2)
        prev[x] = (rot1, rot2)
        if off == seg - ROW_BLOCK:
            tail_ref[s] = cur[ROW_BLOCK - 2:ROW_BLOCK, :]
        if carried and r0 == tm - ROW_BLOCK:
            carry_ref[0:2, pl.ds(col0, tf)] = cur[ROW_BLOCK - 2:ROW_BLOCK, :]
        return bias[x] + (um2 * taps[x][0] + um1 * taps[x][1] + cur * taps[x][2])

    pack = 2 * ROW_BLOCK
    for lo, hi in zip(starts[:-1], starts[1:]):
        for r0 in range(lo, hi, pack):
            a = jnp.concatenate([conv_block(0, r0), conv_block(0, r0 + ROW_BLOCK)], axis=0)
            g = jnp.concatenate([conv_block(1, r0), conv_block(1, r0 + ROW_BLOCK)], axis=0)
            z_ref[r0:r0 + pack, :] = ((g * _sigmoid(g)) * a).astype(BF16)
        z = z_ref[lo:hi, :]
        for c0 in range(0, d, DOWN_CHUNK):
            out_ref[lo:hi, c0:c0 + DOWN_CHUNK] += jnp.dot(
                z, wd_ref[:, c0:c0 + DOWN_CHUNK], preferred_element_type=F32)


def _convffn(h2, xm, w_up, conv_w, conv_b, w_down, hist, *, batch, seq):
    tokens, d = xm.shape
    f = w_down.shape[0]
    tf = min(FFN_WIDTH_TILE, f)
    nf = f // tf
    tm = min(FFN_TOKEN_TILE, tokens)
    seg = min(seq, tm)
    n_seg = tm // seg
    tps = seq // seg
    carried = hist is None
    assert not carried or n_seg == 1, "a carried conv state needs one sequence per token tile"
    row = lambda i, j: (i, 0)
    a_col = lambda i, j: (0, j)
    g_col = lambda i, j: (0, nf + j)
    n_slabs = 1
    while 2 * n_slabs <= nf and d % (2 * n_slabs * LANES) == 0:
        n_slabs *= 2
    in_specs = [pl.BlockSpec((tm, d), row),
                pl.BlockSpec((tm, d // n_slabs), lambda i, j: (i, jnp.minimum(j, n_slabs - 1))),
                pl.BlockSpec((d, tf), a_col), pl.BlockSpec((d, tf), g_col),
                pl.BlockSpec((3, tf), a_col), pl.BlockSpec((3, tf), g_col),
                pl.BlockSpec((1, tf), a_col), pl.BlockSpec((1, tf), g_col),
                pl.BlockSpec((tf, d), lambda i, j: (j, 0))]
    args = [h2, xm, w_up, w_up, conv_w, conv_w, conv_b, conv_b, w_down]
    scratch = [pltpu.VMEM((tm, tf), BF16)]
    if carried:
        scratch += [pltpu.VMEM((8, f), F32), pltpu.VMEM((8, f), F32)]
    else:
        in_specs += [pl.BlockSpec((n_seg, 2, tf), lambda i, j: (i, 0, j)),
                     pl.BlockSpec((n_seg, 2, tf), lambda i, j: (i, 0, nf + j))]
        args += [hist, hist]
    tail_spec = pl.BlockSpec((n_seg, 2, tf), lambda i, j: (i, 0, j))
    tail_shape = jax.ShapeDtypeStruct((batch * tps, 2, f), F32)
    out, tails_a, tails_g = pl.pallas_call(
        functools.partial(_convffn_kernel, n_seg=n_seg, seg=seg, tps=tps, carried=carried),
        grid=(tokens // tm, nf),
        in_specs=in_specs,
        out_specs=[pl.BlockSpec((tm, d), row), tail_spec, tail_spec],
        out_shape=[jax.ShapeDtypeStruct((tokens, d), F32), tail_shape, tail_shape],
        scratch_shapes=scratch,
        compiler_params=_params(2),
    )(*args)
    last = lambda t: t.reshape(batch, tps, 2, f)[:, tps - 1]
    return out, jnp.concatenate([last(tails_a), last(tails_g)], axis=-1)


def _rope_tables(pos, rows):
    half = LANES // 2
    inv = jnp.power(ROPE_BASE, -jnp.arange(half, dtype=F32) / half)
    ang = pos.astype(F32)[:, None] * inv[None, :]
    reps = rows // pos.shape[0]
    return jnp.tile(jnp.cos(ang), (reps, 1)), jnp.tile(jnp.sin(ang), (reps, 1))


def kernel(x_prompt, x_sample, cache_fox_k, cache_fox_v, cache_fox_logf, state_ret, state_conv,
           ln1, w_in, b_f, fox_qn, fox_kn, ret_gn, w_out, ln2, w_up, conv_w, conv_b, w_down):
    depth = ln1.shape[0]
    bp, tp, d = x_prompt.shape
    bs, ts, _ = x_sample.shape
    past = cache_fox_k.shape[2]
    n_heads, hd = cache_fox_k.shape[3], cache_fox_k.shape[4]
    d_head = n_heads * hd
    assert hd == LANES and n_heads % HEADS_PER_STEP == 0 and ret_gn.shape[1:] == (n_heads, hd)

    cos_p, sin_p = _rope_tables(jnp.arange(tp), max(tp, min(INPROJ_TOKEN_TILE, bp * tp)))
    cos_s, sin_s = _rope_tables(past + jnp.arange(ts), max(ts, min(INPROJ_TOKEN_TILE, bs * ts)))
    cache_k = cache_fox_k.reshape(depth * bs, past * n_heads, hd)
    cache_v = cache_fox_v.reshape(depth * bs, past * n_heads, hd)
    state_r = state_ret.reshape(depth * bs, n_heads, hd, hd)
    zero_state = jnp.zeros((bp, n_heads, hd, hd), F32)

    xp = x_prompt.reshape(bp * tp, d)
    xs = x_sample.reshape(bs * ts, d)
    st_p, st_s = [], []
    for l in range(depth):
        r0 = N_FOX_KINDS * d_head + n_heads
        w_fox = w_in[l][:, :N_FOX_KINDS * d_head].astype(BF16)
        w_ret = w_in[l][:, r0:].astype(BF16)
        wf = jnp.pad(w_in[l][:, N_FOX_KINDS * d_head:r0], ((0, 0), (0, hd - n_heads))).astype(BF16)
        bfp = jnp.pad(b_f[l], (0, hd - n_heads)).reshape(1, hd)
        qn = fox_qn[l].reshape(1, hd)
        kn = fox_kn[l].reshape(1, hd)
        wu = w_up[l].astype(BF16)
        wd = w_down[l].astype(BF16)
        cw = conv_w[l]
        cb = conv_b[l].reshape(1, -1)
        ln1_l = ln1[l].reshape(1, d)
        ln2_l = ln2[l].reshape(1, d)

        def group(x, cos_t, sin_t, batch, seq):
            return _inproj(x, ln1_l, w_fox, w_ret, wf, bfp, qn, kn, cos_t, sin_t,
                           batch=batch, seq=seq, n_heads=n_heads)

        fq, fkb, fvb, rq, rk, rv, rg, fk, fv, lf = group(xp, cos_p, sin_p, bp, tp)
        c_row = _cumsum(lf.reshape(bp, tp, n_heads).transpose(0, 2, 1))
        o_fox = _fox_prompt(fq, fkb, fvb, c_row, qn, kn)
        o_ret, s_ret_p = _retention(rq, rk, rv, rg, zero_state, ret_gn[l], s0_block0=0)
        xm, h2 = _outproj(o_fox, o_ret, xp, w_out[l], ln2_l)
        xp, conv_p = _convffn(h2, xm, wu, cw, cb, wd, None, batch=bp, seq=tp)
        st_p.append((fk.reshape(bp, tp, n_heads, hd), fv.reshape(bp, tp, n_heads, hd),
                     lf.reshape(bp, tp, n_heads), s_ret_p, conv_p))

        fq, fkb, fvb, rq, rk, rv, rg, fk, fv, lf = group(xs, cos_s, sin_s, bs, ts)
        lf_all = jnp.concatenate([cache_fox_logf[l], lf.reshape(bs, ts, n_heads)], axis=1)
        pad = (-lf_all.shape[1]) % LANES
        lf_all = jnp.pad(lf_all, ((0, 0), (0, pad), (0, 0)))
        c_row = _cumsum(lf_all.transpose(0, 2, 1))
        o_fox = _fox_sample(fq, fkb, fvb, cache_k, cache_v, c_row, layer=l)
        o_ret, s_ret_s = _retention(rq, rk, rv, rg, state_r, ret_gn[l], s0_block0=l * bs)
        xm, h2 = _outproj(o_fox, o_ret, xs, w_out[l], ln2_l)
        xs, conv_s = _convffn(h2, xm, wu, cw, cb, wd, state_conv[l], batch=bs, seq=ts)
        st_s.append((fk.reshape(bs, ts, n_heads, hd), fv.reshape(bs, ts, n_heads, hd),
                     lf.reshape(bs, ts, n_heads), s_ret_s, conv_s))

    stack = lambda st, k: jnp.stack([s[k] for s in st])
    return (xp.reshape(bp, tp, d), xs.reshape(bs, ts, d),
            stack(st_p, 0), stack(st_p, 1), stack(st_p, 2), stack(st_p, 3), stack(st_p, 4),
            stack(st_s, 0), stack(st_s, 1), stack(st_s, 2), stack(st_s, 3), stack(st_s, 4))
```

```python
import functools
import math

import jax
import jax.numpy as jnp
from jax import lax
from jax.experimental import pallas as pl
from jax.experimental.pallas import tpu as pltpu

EPS = 1e-6
ROPE_BASE = 10000.0
MASKED_LOGIT = -1e30
LANES = 128
VMEM_LIMIT_BYTES = 56 * 1024 * 1024
LOG2E = math.log2(math.e)
FOX_BOUNDED_LIMIT = 100.0
F32_EXP2_UNDERFLOW = 152.0

F32 = jnp.float32
BF16 = jnp.bfloat16

N_FOX_KINDS = 3
N_RET_KINDS = 4
HEADS_PER_STEP = 2
INPROJ_TOKEN_TILE = 1024
CUMSUM_ROWS = 64
FOX_HEADS_PER_STEP = 4
FOX_QUERY_TILE = 512
RETENTION_CHUNK = 256
RETENTION_CHUNKS_PER_STEP = 2
OUTPROJ_TOKEN_TILE = 512
CUMSUM_CHUNK = 256
FOX_KEY_TILE = 256

NT_DIMS = (((1,), (1,)), ((), ()))
TN_DIMS = (((0,), (0,)), ((), ()))


def _params(n_axes):
    return pltpu.CompilerParams(dimension_semantics=("arbitrary",) * n_axes,
                                vmem_limit_bytes=VMEM_LIMIT_BYTES)


def _resident(block_shape, index_map):
    return pl.BlockSpec(block_shape, index_map, pipeline_mode=pl.Buffered(1))


def _rms(x):
    return x * lax.rsqrt(jnp.mean(x * x, axis=-1, keepdims=True) + EPS)


def _sigmoid(x):
    return 1.0 / (1.0 + jnp.exp(-x))


def _inproj_kernel(x_ref, ln1_ref, wfq_ref, wfk_ref, wfv_ref, wrq_ref, wrk_ref, wrv_ref, wrg_ref,
                   wf_ref, bf_ref, qn_ref, kn_ref, cos_ref, sin_ref,
                   fq_ref, fkb_ref, fvb_ref, rq_ref, rk_ref, rv_ref, rg_ref, fk_ref, fv_ref, logf_ref,
                   h_ref, cosf_ref, sinf_ref, *, nb, seg, n_fox):
    j = pl.program_id(1)

    @pl.when(j == 0)
    def _():
        h = (_rms(x_ref[...]) * ln1_ref[...]).astype(BF16)
        h_ref[...] = h
        z = jnp.dot(h, wf_ref[...], preferred_element_type=F32) + bf_ref[...]
        logf = jnp.minimum(z, 0.0) - jnp.log1p(jnp.exp(-jnp.abs(z)))
        logf_ref[...] = logf[:, :n_fox]
        cos_half = cos_ref[...]
        sin_half = sin_ref[...]
        cosf_ref[...] = jnp.concatenate([cos_half, cos_half], axis=1)
        sinf_ref[...] = jnp.concatenate([-sin_half, sin_half], axis=1)

    cos = cosf_ref[...]
    sin = sinf_ref[...]
    hd = LANES

    def project(w_ref):
        p = jnp.dot(h_ref[...], w_ref[...], preferred_element_type=F32)
        return [p[:, e * hd:(e + 1) * hd] for e in range(HEADS_PER_STEP)]

    def head_major(a):
        return a.reshape(nb, seg, hd).astype(BF16)

    def rope(a):
        return a * cos + pltpu.roll(a, hd // 2, 1) * sin

    for e, a in enumerate(project(wfq_ref)):
        fq_ref[:, e] = head_major(_rms(a) * qn_ref[...])
    for e, a in enumerate(project(wfk_ref)):
        fk = _rms(a) * kn_ref[...]
        fk_ref[:, e * hd:(e + 1) * hd] = fk
        fkb_ref[:, e] = head_major(fk)
    for e, a in enumerate(project(wfv_ref)):
        fv_ref[:, e * hd:(e + 1) * hd] = a
        fvb_ref[:, e] = head_major(a)
    for e, a in enumerate(project(wrq_ref)):
        rq_ref[:, e] = head_major(rope(a))
    for e, a in enumerate(project(wrk_ref)):
        rk_ref[:, e] = head_major(rope(a) * (hd ** -0.5))
    for e, a in enumerate(project(wrv_ref)):
        rv_ref[:, e] = head_major(a)
    for e, a in enumerate(project(wrg_ref)):
        rg_ref[:, e] = head_major(a)


def _inproj(x, ln1, w_fox, w_ret, wf, bfp, qn, kn, cos_t, sin_t, *, batch, seq, n_heads):
    tokens, d = x.shape
    hd = LANES
    tm = min(INPROJ_TOKEN_TILE, tokens)
    seg = min(seq, tm)
    nb = tm // seg
    tps = seq // seg
    n_groups = n_heads // HEADS_PER_STEP
    gcols = HEADS_PER_STEP * hd
    grid = (tokens // tm, n_groups)

    def w_spec(kind):
        return pl.BlockSpec((d, gcols), lambda i, j: (0, kind * n_groups + j))

    hm_shape = jax.ShapeDtypeStruct((batch, n_heads, seq, hd), BF16)
    hm_spec = pl.BlockSpec((nb, HEADS_PER_STEP, seg, hd), lambda i, j: (i // tps, j, i % tps, 0))
    tok_shape = jax.ShapeDtypeStruct((tokens, n_heads * hd), F32)
    tok_spec = pl.BlockSpec((tm, gcols), lambda i, j: (i, j))

    return pl.pallas_call(
        functools.partial(_inproj_kernel, nb=nb, seg=seg, n_fox=n_heads),
        grid=grid,
        in_specs=[pl.BlockSpec((tm, d), lambda i, j: (i, 0)),
                  _resident((1, d), lambda i, j: (0, 0))]
                 + [w_spec(k) for k in range(N_FOX_KINDS)] + [w_spec(k) for k in range(N_RET_KINDS)]
                 + [_resident((d, hd), lambda i, j: (0, 0)),
                    _resident((1, hd), lambda i, j: (0, 0)),
                    _resident((1, hd), lambda i, j: (0, 0)),
                    _resident((1, hd), lambda i, j: (0, 0)),
                    pl.BlockSpec((tm, hd // 2), lambda i, j: (i % tps, 0)),
                    pl.BlockSpec((tm, hd // 2), lambda i, j: (i % tps, 0))],
        out_specs=[hm_spec] * 7 + [tok_spec, tok_spec,
                                   pl.BlockSpec((tm, n_heads), lambda i, j: (i, 0))],
        out_shape=[hm_shape] * 7 + [tok_shape, tok_shape,
                                    jax.ShapeDtypeStruct((tokens, n_heads), F32)],
        scratch_shapes=[pltpu.VMEM((tm, d), BF16), pltpu.VMEM((tm, hd), F32), pltpu.VMEM((tm, hd), F32)],
        compiler_params=_params(2),
    )(x, ln1, w_fox, w_fox, w_fox, w_ret, w_ret, w_ret, w_ret, wf, bfp, qn, kn, cos_t, sin_t)


def _cumsum_kernel(lf_ref, c_ref, *, ch):
    rows, total = lf_ref.shape
    r = lax.broadcasted_iota(jnp.int32, (ch, ch), 0)
    c = lax.broadcasted_iota(jnp.int32, (ch, ch), 1)
    upper = (r <= c).astype(F32)
    carry = jnp.zeros((rows, 1), F32)
    for k in range(total // ch):
        x = lf_ref[:, k * ch:(k + 1) * ch]
        cs = jnp.dot(x, upper, precision=lax.Precision.HIGHEST, preferred_element_type=F32) + carry
        c_ref[:, k * ch:(k + 1) * ch] = cs
        carry = cs[:, ch - 1:ch]


def _cumsum(lf_t):
    batch, n_heads, total = lf_t.shape
    rows = batch * n_heads
    band = min(CUMSUM_ROWS, rows)
    ch = CUMSUM_CHUNK if total % CUMSUM_CHUNK == 0 else LANES
    spec = pl.BlockSpec((band, total), lambda b: (b, 0))
    out = pl.pallas_call(
        functools.partial(_cumsum_kernel, ch=ch),
        grid=(rows // band,), in_specs=[spec], out_specs=spec,
        out_shape=jax.ShapeDtypeStruct((rows, total), F32),
        compiler_params=_params(1),
    )(lf_t.reshape(rows, total))
    return out.reshape(batch, n_heads, total)


def _softmax_step(z, v, m_ref, l_ref, acc_ref):
    tk = z.shape[1]
    m_prev = m_ref[...]
    m_next = jnp.maximum(m_prev, jnp.max(z, axis=1, keepdims=True))
    p = jnp.exp2(z - pltpu.repeat(m_next, tk // LANES, 1))
    alpha = jnp.exp2(m_prev - m_next)
    l_ref[...] = alpha * l_ref[...] + jnp.sum(p, axis=1, keepdims=True)
    acc_ref[...] = alpha * acc_ref[...] + jnp.dot(p.astype(BF16), v, preferred_element_type=F32)
    m_ref[...] = m_next


def _fox_prompt_kernel(first_ref, q_ref, k_ref, v_ref, c_ref, o_ref, m_ref, l_ref, acc_ref, *, tq, tk, scale):
    b, hg, qi = pl.program_id(0), pl.program_id(1), pl.program_id(2)
    group, hd = q_ref.shape[1], q_ref.shape[3]
    m_ref[...] = jnp.full(m_ref.shape, MASKED_LOGIT, F32)
    l_ref[...] = jnp.zeros(l_ref.shape, F32)
    acc_ref[...] = jnp.zeros(acc_ref.shape, F32)

    def logits(e, k0):
        k = k_ref[0, e, pl.ds(k0, tk), :]
        s = lax.dot_general(q_ref[0, e], k, NT_DIMS, preferred_element_type=F32)
        return s * (scale * LOG2E) - c_ref[0, e, :, pl.ds(k0, tk)] * LOG2E

    def body(j, carry):
        k0 = pl.multiple_of(j * tk, tk)
        for e in range(group):
            _softmax_step(logits(e, k0), v_ref[0, e, pl.ds(k0, tk), :],
                          m_ref.at[e], l_ref.at[e], acc_ref.at[e])
        return carry

    first = first_ref[(b * pl.num_programs(1) + hg) * pl.num_programs(2) + qi]
    lax.fori_loop(first, qi * (tq // tk), body, 0)

    row = lax.broadcasted_iota(jnp.int32, (tq, tk), 0)
    col = lax.broadcasted_iota(jnp.int32, (tq, tk), 1)
    for e in range(group):
        for off in range(0, tq, tk):
            k0 = pl.multiple_of(qi * tq + off, tk)
            z = jnp.where(col + off <= row, logits(e, k0), MASKED_LOGIT)
            _softmax_step(z, v_ref[0, e, pl.ds(k0, tk), :], m_ref.at[e], l_ref.at[e], acc_ref.at[e])
        o_ref[:, e * hd:(e + 1) * hd] = (acc_ref[e] / l_ref[e]).astype(o_ref.dtype)


def _fox_prompt_bounded_kernel(first_ref, q_ref, k_ref, v_ref, c2_ref, r2_ref, o_ref, l_ref, acc_ref, *, tq, tk, scale):
    b, hg, qi = pl.program_id(0), pl.program_id(1), pl.program_id(2)
    group, hd = q_ref.shape[1], q_ref.shape[3]
    l_ref[...] = jnp.zeros(l_ref.shape, F32)
    acc_ref[...] = jnp.zeros(acc_ref.shape, F32)
    q0 = pl.multiple_of(qi * tq, tq)
    row_q = lax.broadcasted_iota(jnp.int32, (tq, tq), 0)
    col_q = lax.broadcasted_iota(jnp.int32, (tq, tq), 1)
    row_terms = [jnp.sum(jnp.where(row_q == col_q, r2_ref[0, e, :, pl.ds(q0, tq)], 0.0), axis=1, keepdims=True)
                 for e in range(group)]
    row = lax.broadcasted_iota(jnp.int32, (tq, tk), 0)
    col = lax.broadcasted_iota(jnp.int32, (tq, tk), 1)

    def step(e, k0, diag_offset=None):
        k = k_ref[0, e, pl.ds(k0, tk), :]
        s = lax.dot_general(q_ref[0, e], k, NT_DIMS, preferred_element_type=F32)
        z = (s * (scale * LOG2E) - c2_ref[0, e, :, pl.ds(k0, tk)]) + row_terms[e]
        if diag_offset is not None:
            z = jnp.where(col + diag_offset <= row, z, MASKED_LOGIT)
        p = jnp.exp2(z)
        part = p[:, 0:LANES]
        for c0 in range(LANES, tk, LANES):
            part = part + p[:, c0:c0 + LANES]
        l_ref[e] += part
        acc_ref[e] += jnp.dot(p.astype(BF16), v_ref[0, e, pl.ds(k0, tk), :], preferred_element_type=F32)

    def body(j, carry):
        k0 = pl.multiple_of(j * tk, tk)
        for e in range(group):
            step(e, k0)
        return carry

    first = first_ref[(b * pl.num_programs(1) + hg) * pl.num_programs(2) + qi]
    lax.fori_loop(first, qi * (tq // tk), body, 0)
    for e in range(group):
        for off in range(0, tq, tk):
            step(e, pl.multiple_of(q0 + off, tk), off)
        l = jnp.sum(l_ref[e], axis=1, keepdims=True)
        o_ref[:, e * hd:(e + 1) * hd] = (acc_ref[e] / l).astype(o_ref.dtype)


def _qk_logit_bound(qn, kn, hd, scale):
    bf16_slack = (1.0 + 2.0 ** -8) ** 2
    return (hd * scale * LOG2E * bf16_slack) * jnp.max(jnp.abs(qn)) * jnp.max(jnp.abs(kn))


def _first_key_tiles(c_row, bound, *, tq, tk, group):
    batch, n_heads, seq = c_row.shape
    c_first_row = c_row[:, :, 0::tq]
    c_last_key = c_row[:, :, tk - 1::tk]
    gap = (c_last_key[:, :, None, :] - c_first_row[:, :, :, None]) * LOG2E
    nq = seq // tq
    earlier = (jnp.arange(seq // tk)[None, :] + 1) * tk <= jnp.arange(nq)[:, None] * tq
    dead = (gap > 2.0 * bound + F32_EXP2_UNDERFLOW) & earlier
    first = jnp.sum(dead, axis=-1).astype(jnp.int32)
    return jnp.min(first.reshape(batch, n_heads // group, group, nq), axis=2).reshape(-1)


def _fox_prompt(q, k, v, c_row, qn, kn):
    batch, n_heads, seq, hd = q.shape
    tq = min(FOX_QUERY_TILE, seq)
    tk = min(FOX_KEY_TILE, tq)
    nq = seq // tq
    group = FOX_HEADS_PER_STEP
    scale = hd ** -0.5
    bound = _qk_logit_bound(qn, kn, hd, scale)
    first = _first_key_tiles(c_row, bound, tq=tq, tk=tk, group=group)
    kv_spec = pl.BlockSpec((1, group, seq, hd), lambda b, h, i, first: (b, h, 0, 0))
    row_spec = pl.BlockSpec((1, group, 1, seq), lambda b, h, i, first: (b, h, 0, 0))
    q_spec = pl.BlockSpec((1, group, tq, hd), lambda b, h, i, first: (b, h, i, 0))
    out_spec = pl.BlockSpec((tq, group * hd), lambda b, h, i, first: (b * nq + i, h))
    out_shape = jax.ShapeDtypeStruct((batch * seq, n_heads * hd), BF16)
    stats = pltpu.VMEM((group, tq, LANES), F32)
    acc = pltpu.VMEM((group, tq, hd), F32)

    def running_max(c4):
        return pl.pallas_call(
            functools.partial(_fox_prompt_kernel, tq=tq, tk=tk, scale=scale),
            grid_spec=pltpu.PrefetchScalarGridSpec(
                num_scalar_prefetch=1, grid=(batch, n_heads // group, nq),
                in_specs=[q_spec, kv_spec, kv_spec, row_spec], out_specs=out_spec,
                scratch_shapes=[stats, stats, acc]),
            out_shape=out_shape, compiler_params=_params(3),
        )(first, q, k, v, c4)

    def bounded(c4):
        return pl.pallas_call(
            functools.partial(_fox_prompt_bounded_kernel, tq=tq, tk=tk, scale=scale),
            grid_spec=pltpu.PrefetchScalarGridSpec(
                num_scalar_prefetch=1, grid=(batch, n_heads // group, nq),
                in_specs=[q_spec, kv_spec, kv_spec, row_spec, row_spec], out_specs=out_spec,
                scratch_shapes=[stats, acc]),
            out_shape=out_shape, compiler_params=_params(3),
        )(first, q, k, v, c4 * LOG2E, c4 * LOG2E - bound)

    return lax.cond(2.0 * bound <= FOX_BOUNDED_LIMIT, bounded, running_max,
                    c_row.reshape(batch, n_heads, 1, seq))


def _fox_sample_kernel(q_ref, kn_ref, vn_ref, kc_ref, vc_ref, c_ref, o_ref, *, past, scale):
    n_heads, new, hd = q_ref.shape[1:]
    row = lax.broadcasted_iota(jnp.int32, (new, new), 0)
    col = lax.broadcasted_iota(jnp.int32, (new, new), 1)
    for h in range(n_heads):
        q = q_ref[0, h]
        kc = kc_ref[0, pl.ds(h, past, stride=n_heads), :].astype(BF16)
        vc = vc_ref[0, pl.ds(h, past, stride=n_heads), :].astype(BF16)
        z_c = (lax.dot_general(q, kc, NT_DIMS, preferred_element_type=F32) * scale
               - c_ref[0, h:h + 1, 0:past])
        z_n = (lax.dot_general(q, kn_ref[0, h], NT_DIMS, preferred_element_type=F32) * scale
               - c_ref[0, h:h + 1, past:past + new])
        z_n = jnp.where(col <= row, z_n, MASKED_LOGIT)
        m = jnp.maximum(jnp.max(z_c, axis=1, keepdims=True), jnp.max(z_n, axis=1, keepdims=True))
        p_c = jnp.exp(z_c - m)
        p_n = jnp.exp(z_n - m)
        l = jnp.sum(p_c, axis=1, keepdims=True) + jnp.sum(p_n, axis=1, keepdims=True)
        acc = (jnp.dot(p_c.astype(BF16), vc, preferred_element_type=F32)
               + jnp.dot(p_n.astype(BF16), vn_ref[0, h], preferred_element_type=F32))
        o_ref[:, h * hd:(h + 1) * hd] = (acc / l).astype(o_ref.dtype)


def _fox_sample(q, k_new, v_new, cache_k, cache_v, c_row, *, layer):
    batch, n_heads, new, hd = q.shape
    past = cache_k.shape[1] // n_heads
    new_spec = pl.BlockSpec((1, n_heads, new, hd), lambda b: (b, 0, 0, 0))
    cache_spec = pl.BlockSpec((1, past * n_heads, hd), lambda b: (layer * batch + b, 0, 0))
    return pl.pallas_call(
        functools.partial(_fox_sample_kernel, past=past, scale=hd ** -0.5),
        grid=(batch,),
        in_specs=[new_spec, new_spec, new_spec, cache_spec, cache_spec,
                  pl.BlockSpec((1, n_heads, c_row.shape[2]), lambda b: (b, 0, 0))],
        out_specs=pl.BlockSpec((new, n_heads * hd), lambda b: (b, 0)),
        out_shape=jax.ShapeDtypeStruct((batch * new, n_heads * hd), BF16),
        compiler_params=_params(1),
    )(q, k_new, v_new, cache_k, cache_v, c_row)


def _retention_kernel(q_ref, k_ref, v_ref, g_ref, s0_ref, gn_ref, lg_ref, o_ref, s_out_ref,
                      s_ref, decay_ref, inter_ref, upd_ref, keep_ref):
    b = pl.program_id(0)
    c = pl.program_id(1)
    n_heads, span, hd = q_ref.shape[1:]
    chunk = decay_ref.shape[1]

    @pl.when((b == 0) & (c == 0))
    def _():
        i = lax.broadcasted_iota(jnp.int32, (chunk, chunk), 0)
        jj = lax.broadcasted_iota(jnp.int32, (chunk, chunk), 1)
        diff = (i - jj).astype(F32)
        pos = lax.broadcasted_iota(jnp.int32, (chunk, hd), 0).astype(F32)
        for h in range(n_heads):
            lg = lg_ref[h:h + 1, :]
            decay_ref[h] = jnp.where(diff >= 0.0, jnp.exp(jnp.maximum(diff, 0.0) * lg[:, 0:1]), 0.0)
            inter_ref[h] = jnp.exp((pos + 1.0) * lg)
            upd_ref[h] = jnp.exp((chunk - 1.0 - pos) * lg)
            keep_ref[h:h + 1, :] = jnp.exp(chunk * lg)

    @pl.when(c == 0)
    def _():
        s_ref[...] = s0_ref[0]

    for r0 in range(0, span, chunk):
        for h in range(n_heads):
            q = q_ref[0, h, r0:r0 + chunk, :]
            k = k_ref[0, h, r0:r0 + chunk, :]
            v = v_ref[0, h, r0:r0 + chunk, :]
            s_prev = s_ref[h]
            scores = lax.dot_general(q, k, NT_DIMS, preferred_element_type=F32) * decay_ref[h]
            o = jnp.dot(scores.astype(BF16), v, preferred_element_type=F32)
            o = o + jnp.dot(q, s_prev.astype(BF16), preferred_element_type=F32) * inter_ref[h]
            kd = (k.astype(F32) * upd_ref[h]).astype(BF16)
            s_ref[h] = keep_ref[h:h + 1, :] * s_prev + lax.dot_general(kd, v, TN_DIMS,
                                                                       preferred_element_type=F32)
            g = g_ref[0, h, r0:r0 + chunk, :].astype(F32)
            y = _rms(o) * gn_ref[h:h + 1, :]
            o_ref[r0:r0 + chunk, h * hd:(h + 1) * hd] = (y * (g * _sigmoid(g))).astype(o_ref.dtype)

    @pl.when(c == pl.num_programs(1) - 1)
    def _():
        s_out_ref[0] = s_ref[...]


def _retention(q, k, v, g, s0, gn, *, s0_block0):
    batch, n_heads, seq, hd = q.shape
    chunk = min(RETENTION_CHUNK, seq)
    span = chunk * RETENTION_CHUNKS_PER_STEP if seq % (chunk * RETENTION_CHUNKS_PER_STEP) == 0 else chunk
    nc = seq // span
    log_gamma = jnp.log(1.0 - jnp.power(2.0, -5.0 - jnp.arange(n_heads, dtype=F32)))
    lg = jnp.broadcast_to(log_gamma[:, None], (n_heads, hd))
    qkv_spec = pl.BlockSpec((1, n_heads, span, hd), lambda b, c: (b, 0, c, 0))
    return pl.pallas_call(
        _retention_kernel,
        grid=(batch, nc),
        in_specs=[qkv_spec] * 4 + [pl.BlockSpec((1, n_heads, hd, hd), lambda b, c: (s0_block0 + b, 0, 0, 0)),
                                   _resident((n_heads, hd), lambda b, c: (0, 0)),
                                   _resident((n_heads, hd), lambda b, c: (0, 0))],
        out_specs=[pl.BlockSpec((span, n_heads * hd), lambda b, c: (b * nc + c, 0)),
                   pl.BlockSpec((1, n_heads, hd, hd), lambda b, c: (b, 0, 0, 0))],
        out_shape=[jax.ShapeDtypeStruct((batch * seq, n_heads * hd), BF16),
                   jax.ShapeDtypeStruct((batch, n_heads, hd, hd), F32)],
        scratch_shapes=[pltpu.VMEM((n_heads, hd, hd), F32), pltpu.VMEM((n_heads, chunk, chunk), F32),
                        pltpu.VMEM((n_heads, chunk, hd), F32), pltpu.VMEM((n_heads, chunk, hd), F32),
                        pltpu.VMEM((n_heads, hd), F32)],
        compiler_params=_params(2),
    )(q, k, v, g, s0, gn, lg)


def _outproj_kernel(of_ref, or_ref, x_ref, w_ref, ln2_ref, xm_ref, h2_ref, wb_ref):
    @pl.when(pl.program_id(0) == 0)
    def _():
        wb_ref[...] = w_ref[...].astype(BF16)

    half = of_ref.shape[1]
    y = jnp.dot(of_ref[...], wb_ref[0:half, :], preferred_element_type=F32)
    y = y + jnp.dot(or_ref[...], wb_ref[half:2 * half, :], preferred_element_type=F32)
    xm = x_ref[...] + y
    xm_ref[...] = xm
    h2_ref[...] = (_rms(xm) * ln2_ref[...]).astype(BF16)


def _outproj(o_fox, o_ret, x, w_out, ln2):
    tokens, d = x.shape
    half = o_fox.shape[1]
    tm = min(OUTPROJ_TOKEN_TILE, tokens)
    row = lambda i: (i, 0)
    return pl.pallas_call(
        _outproj_kernel,
        grid=(tokens // tm,),
        in_specs=[pl.BlockSpec((tm, half), row), pl.BlockSpec((tm, half), row),
                  pl.BlockSpec((tm, d), row),
                  _resident((2 * half, d), lambda i: (0, 0)), _resident((1, d), lambda i: (0, 0))],
        out_specs=[pl.BlockSpec((tm, d), row), pl.BlockSpec((tm, d), row)],
        out_shape=[jax.ShapeDtypeStruct((tokens, d), F32), jax.ShapeDtypeStruct((tokens, d), BF16)],
        scratch_shapes=[pltpu.VMEM((2 * half, d), BF16)],
        compiler_params=_params(1),
    )(o_fox, o_ret, x, w_out, ln2)


ROW_BLOCK = 8
DOWN_CHUNK = 512
SUB_SPLIT = (1, 1)
FFN_TOKEN_TILE = 1024
FFN_WIDTH_TILE = 512


def _convffn_kernel(*refs, n_seg, seg, tps, carried):
    if carried:
        (h2_ref, xm_ref, wa_ref, wg_ref, cwa_ref, cwg_ref, cba_ref, cbg_ref, wd_ref,
         out_ref, ta_ref, tg_ref, z_ref, ca_ref, cg_ref) = refs
        hists = (None, None)
        carries = (ca_ref, cg_ref)
    else:
        (h2_ref, xm_ref, wa_ref, wg_ref, cwa_ref, cwg_ref, cba_ref, cbg_ref, wd_ref, ha_ref, hg_ref,
         out_ref, ta_ref, tg_ref, z_ref) = refs
        hists = (ha_ref, hg_ref)
        carries = (None, None)
    i = pl.program_id(0)
    j = pl.program_id(1)
    tf = wa_ref.shape[1]
    d = out_ref.shape[1]
    tm = n_seg * seg
    col0 = pl.multiple_of(j * tf, tf)

    xw = xm_ref.shape[1]

    @pl.when(j == 0)
    def _():
        out_ref[:, 0:xw] = xm_ref[...]
        out_ref[:, xw:d] = jnp.zeros((tm, d - xw), F32)

    @pl.when((j > 0) & (j < d // xw))
    def _():
        out_ref[:, pl.ds(pl.multiple_of(j * xw, xw), xw)] += xm_ref[...]

    if carried:
        @pl.when(i % tps == 0)
        def _():
            for carry_ref in carries:
                carry_ref[0:2, pl.ds(col0, tf)] = jnp.zeros((2, tf), F32)

    unit = tm // sum(SUB_SPLIT)
    starts = [unit * sum(SUB_SPLIT[:n]) for n in range(len(SUB_SPLIT) + 1)]
    halves = ((wa_ref, cwa_ref, cba_ref, ta_ref, carries[0], hists[0]),
              (wg_ref, cwg_ref, cbg_ref, tg_ref, carries[1], hists[1]))
    ups = [[jnp.dot(h2_ref[lo:hi, :], w_ref[...], preferred_element_type=F32)
            for w_ref, *_ in halves] for lo, hi in zip(starts[:-1], starts[1:])]

    row = lax.broadcasted_iota(jnp.int32, (ROW_BLOCK, tf), 0)
    taps = [[jnp.broadcast_to(cw_ref[t:t + 1, :], (ROW_BLOCK, tf)) for t in range(3)]
            for _, cw_ref, *_ in halves]
    bias = [jnp.broadcast_to(cb_ref[...], (ROW_BLOCK, tf)) for _, _, cb_ref, *_ in halves]
    prev = [None, None]

    def conv_block(x, r0):
        _, _, _, tail_ref, carry_ref, hist_ref = halves[x]
        s, off = divmod(r0, seg)
        if off == 0:
            if carried:
                h0 = carry_ref[0:1, pl.ds(col0, tf)]
                h1 = carry_ref[1:2, pl.ds(col0, tf)]
            else:
                h0 = hist_ref[s, 0:1, :]
                h1 = hist_ref[s, 1:2, :]
            prev[x] = (jnp.broadcast_to(h1, (ROW_BLOCK, tf)), jnp.where(row == 0, h0, h1))
        si = max(n for n, lo in enumerate(starts[:-1]) if lo <= r0)
        cur = ups[si][x][r0 - starts[si]:r0 - starts[si] + ROW_BLOCK]
        rot1 = pltpu.roll(cur, 1, 0)
        rot2 = pltpu.roll(cur, 2, 0)
        um1 = jnp.where(row == 0, prev[x][0], rot1)
        um2 = jnp.where(row <= 1, prev[x][1], rot2)
        prev[x] = (rot1, rot2)
        if off == seg - ROW_BLOCK:
            tail_ref[s] = cur[ROW_BLOCK - 2:ROW_BLOCK, :]
        if carried and r0 == tm - ROW_BLOCK:
            carry_ref[0:2, pl.ds(col0, tf)] = cur[ROW_BLOCK - 2:ROW_BLOCK, :]
        return bias[x] + (um2 * taps[x][0] + um1 * taps[x][1] + cur * taps[x][2])

    pack = 2 * ROW_BLOCK
    for lo, hi in zip(starts[:-1], starts[1:]):
        for r0 in range(lo, hi, pack):
            a = jnp.concatenate([conv_block(0, r0), conv_block(0, r0 + ROW_BLOCK)], axis=0)
            g = jnp.concatenate([conv_block(1, r0), conv_block(1, r0 + ROW_BLOCK)], axis=0)
            z_ref[r0:r0 + pack, :] = ((g * _sigmoid(g)) * a).astype(BF16)
        z = z_ref[lo:hi, :]
        for c0 in range(0, d, DOWN_CHUNK):
            out_ref[lo:hi, c0:c0 + DOWN_CHUNK] += jnp.dot(
                z, wd_ref[:, c0:c0 + DOWN_CHUNK], preferred_element_type=F32)


def _convffn(h2, xm, w_up, conv_w, conv_b, w_down, hist, *, batch, seq):
    tokens, d = xm.shape
    f = w_down.shape[0]
    tf = min(FFN_WIDTH_TILE, f)
    nf = f // tf
    tm = min(FFN_TOKEN_TILE, tokens)
    seg = min(seq, tm)
    n_seg = tm // seg
    tps = seq // seg
    carried = hist is None
    assert not carried or n_seg == 1, "a carried conv state needs one sequence per token tile"
    row = lambda i, j: (i, 0)
    a_col = lambda i, j: (0, j)
    g_col = lambda i, j: (0, nf + j)
    n_slabs = 1
    while 2 * n_slabs <= nf and d % (2 * n_slabs * LANES) == 0:
        n_slabs *= 2
    in_specs = [pl.BlockSpec((tm, d), row),
                pl.BlockSpec((tm, d // n_slabs), lambda i, j: (i, jnp.minimum(j, n_slabs - 1))),
                pl.BlockSpec((d, tf), a_col), pl.BlockSpec((d, tf), g_col),
                pl.BlockSpec((3, tf), a_col), pl.BlockSpec((3, tf), g_col),
                pl.BlockSpec((1, tf), a_col), pl.BlockSpec((1, tf), g_col),
                pl.BlockSpec((tf, d), lambda i, j: (j, 0))]
    args = [h2, xm, w_up, w_up, conv_w, conv_w, conv_b, conv_b, w_down]
    scratch = [pltpu.VMEM((tm, tf), BF16)]
    if carried:
        scratch += [pltpu.VMEM((8, f), F32), pltpu.VMEM((8, f), F32)]
    else:
        in_specs += [pl.BlockSpec((n_seg, 2, tf), lambda i, j: (i, 0, j)),
                     pl.BlockSpec((n_seg, 2, tf), lambda i, j: (i, 0, nf + j))]
        args += [hist, hist]
    tail_spec = pl.BlockSpec((n_seg, 2, tf), lambda i, j: (i, 0, j))
    tail_shape = jax.ShapeDtypeStruct((batch * tps, 2, f), F32)
    out, tails_a, tails_g = pl.pallas_call(
        functools.partial(_convffn_kernel, n_seg=n_seg, seg=seg, tps=tps, carried=carried),
        grid=(tokens // tm, nf),
        in_specs=in_specs,
        out_specs=[pl.BlockSpec((tm, d), row), tail_spec, tail_spec],
        out_shape=[jax.ShapeDtypeStruct((tokens, d), F32), tail_shape, tail_shape],
        scratch_shapes=scratch,
        compiler_params=_params(2),
    )(*args)
    last = lambda t: t.reshape(batch, tps, 2, f)[:, tps - 1]
    return out, jnp.concatenate([last(tails_a), last(tails_g)], axis=-1)


def _rope_tables(pos, rows):
    half = LANES // 2
    inv = jnp.power(ROPE_BASE, -jnp.arange(half, dtype=F32) / half)
    ang = pos.astype(F32)[:, None] * inv[None, :]
    reps = rows // pos.shape[0]
    return jnp.tile(jnp.cos(ang), (reps, 1)), jnp.tile(jnp.sin(ang), (reps, 1))


def kernel(x_prompt, x_sample, cache_fox_k, cache_fox_v, cache_fox_logf, state_ret, state_conv,
           ln1, w_in, b_f, fox_qn, fox_kn, ret_gn, w_out, ln2, w_up, conv_w, conv_b, w_down):
    depth = ln1.shape[0]
    bp, tp, d = x_prompt.shape
    bs, ts, _ = x_sample.shape
    past = cache_fox_k.shape[2]
    n_heads, hd = cache_fox_k.shape[3], cache_fox_k.shape[4]
    d_head = n_heads * hd
    assert hd == LANES and n_heads % HEADS_PER_STEP == 0 and ret_gn.shape[1:] == (n_heads, hd)

    cos_p, sin_p = _rope_tables(jnp.arange(tp), max(tp, min(INPROJ_TOKEN_TILE, bp * tp)))
    cos_s, sin_s = _rope_tables(past + jnp.arange(ts), max(ts, min(INPROJ_TOKEN_TILE, bs * ts)))
    cache_k = cache_fox_k.reshape(depth * bs, past * n_heads, hd)
    cache_v = cache_fox_v.reshape(depth * bs, past * n_heads, hd)
    state_r = state_ret.reshape(depth * bs, n_heads, hd, hd)
    zero_state = jnp.zeros((bp, n_heads, hd, hd), F32)

    xp = x_prompt.reshape(bp * tp, d)
    xs = x_sample.reshape(bs * ts, d)
    st_p, st_s = [], []
    for l in range(depth):
        r0 = N_FOX_KINDS * d_head + n_heads
        w_fox = w_in[l][:, :N_FOX_KINDS * d_head].astype(BF16)
        w_ret = w_in[l][:, r0:].astype(BF16)
        wf = jnp.pad(w_in[l][:, N_FOX_KINDS * d_head:r0], ((0, 0), (0, hd - n_heads))).astype(BF16)
        bfp = jnp.pad(b_f[l], (0, hd - n_heads)).reshape(1, hd)
        qn = fox_qn[l].reshape(1, hd)
        kn = fox_kn[l].reshape(1, hd)
        wu = w_up[l].astype(BF16)
        wd = w_down[l].astype(BF16)
        cw = conv_w[l]
        cb = conv_b[l].reshape(1, -1)
        ln1_l = ln1[l].reshape(1, d)
        ln2_l = ln2[l].reshape(1, d)

        def group(x, cos_t, sin_t, batch, seq):
            return _inproj(x, ln1_l, w_fox, w_ret, wf, bfp, qn, kn, cos_t, sin_t,
                           batch=batch, seq=seq, n_heads=n_heads)

        fq, fkb, fvb, rq, rk, rv, rg, fk, fv, lf = group(xp, cos_p, sin_p, bp, tp)
        c_row = _cumsum(lf.reshape(bp, tp, n_heads).transpose(0, 2, 1))
        o_fox = _fox_prompt(fq, fkb, fvb, c_row, qn, kn)
        o_ret, s_ret_p = _retention(rq, rk, rv, rg, zero_state, ret_gn[l], s0_block0=0)
        xm, h2 = _outproj(o_fox, o_ret, xp, w_out[l], ln2_l)
        xp, conv_p = _convffn(h2, xm, wu, cw, cb, wd, None, batch=bp, seq=tp)
        st_p.append((fk.reshape(bp, tp, n_heads, hd), fv.reshape(bp, tp, n_heads, hd),
                     lf.reshape(bp, tp, n_heads), s_ret_p, conv_p))

        fq, fkb, fvb, rq, rk, rv, rg, fk, fv, lf = group(xs, cos_s, sin_s, bs, ts)
        lf_all = jnp.concatenate([cache_fox_logf[l], lf.reshape(bs, ts, n_heads)], axis=1)
        pad = (-lf_all.shape[1]) % LANES
        lf_all = jnp.pad(lf_all, ((0, 0), (0, pad), (0, 0)))
        c_row = _cumsum(lf_all.transpose(0, 2, 1))
        o_fox = _fox_sample(fq, fkb, fvb, cache_k, cache_v, c_row, layer=l)
        o_ret, s_ret_s = _retention(rq, rk, rv, rg, state_r, ret_gn[l], s0_block0=l * bs)
        xm, h2 = _outproj(o_fox, o_ret, xs, w_out[l], ln2_l)
        xs, conv_s = _convffn(h2, xm, wu, cw, cb, wd, state_conv[l], batch=bs, seq=ts)
        st_s.append((fk.reshape(bs, ts, n_heads, hd), fv.reshape(bs, ts, n_heads, hd),
                     lf.reshape(bs, ts, n_heads), s_ret_s, conv_s))

    stack = lambda st, k: jnp.stack([s[k] for s in st])
    return (xp.reshape(bp, tp, d), xs.reshape(bs, ts, d),
            stack(st_p, 0), stack(st_p, 1), stack(st_p, 2), stack(st_p, 3), stack(st_p, 4),
            stack(st_s, 0), stack(st_s, 1), stack(st_s, 2), stack(st_s, 3), stack(st_s, 4))
```

```python
import functools
import math

import jax
import jax.numpy as jnp
from jax import lax
from jax.experimental import pallas as pl
from jax.experimental.pallas import tpu as pltpu

EPS = 1e-6
ROPE_BASE = 10000.0
MASKED_LOGIT = -1e30
LANES = 128
VMEM_LIMIT_BYTES = 56 * 1024 * 1024
LOG2E = math.log2(math.e)
FOX_BOUNDED_LIMIT = 100.0
F32_EXP2_UNDERFLOW = 152.0

F32 = jnp.float32
BF16 = jnp.bfloat16

N_FOX_KINDS = 3
N_RET_KINDS = 4
HEADS_PER_STEP = 2
INPROJ_TOKEN_TILE = 1024
CUMSUM_ROWS = 64
FOX_HEADS_PER_STEP = 4
FOX_QUERY_TILE = 512
RETENTION_CHUNK = 256
RETENTION_CHUNKS_PER_STEP = 4
OUTPROJ_TOKEN_TILE = 512
CUMSUM_CHUNK = 256
FOX_KEY_TILE = 256

NT_DIMS = (((1,), (1,)), ((), ()))
TN_DIMS = (((0,), (0,)), ((), ()))


def _params(n_axes):
    return pltpu.CompilerParams(dimension_semantics=("arbitrary",) * n_axes,
                                vmem_limit_bytes=VMEM_LIMIT_BYTES)


def _resident(block_shape, index_map):
    return pl.BlockSpec(block_shape, index_map, pipeline_mode=pl.Buffered(1))


def _rms(x):
    return x * lax.rsqrt(jnp.mean(x * x, axis=-1, keepdims=True) + EPS)


def _sigmoid(x):
    return 1.0 / (1.0 + jnp.exp(-x))


def _inproj_kernel(x_ref, ln1_ref, wfq_ref, wfk_ref, wfv_ref, wrq_ref, wrk_ref, wrv_ref, wrg_ref,
                   wf_ref, bf_ref, qn_ref, kn_ref, cos_ref, sin_ref,
                   fq_ref, fkb_ref, fvb_ref, rq_ref, rk_ref, rv_ref, rg_ref, fk_ref, fv_ref, logf_ref,
                   h_ref, cosf_ref, sinf_ref, *, nb, seg, n_fox):
    j = pl.program_id(1)

    @pl.when(j == 0)
    def _():
        h = (_rms(x_ref[...]) * ln1_ref[...]).astype(BF16)
        h_ref[...] = h
        z = jnp.dot(h, wf_ref[...], preferred_element_type=F32) + bf_ref[...]
        logf = jnp.minimum(z, 0.0) - jnp.log1p(jnp.exp(-jnp.abs(z)))
        logf_ref[...] = logf[:, :n_fox]
        cos_half = cos_ref[...]
        sin_half = sin_ref[...]
        cosf_ref[...] = jnp.concatenate([cos_half, cos_half], axis=1)
        sinf_ref[...] = jnp.concatenate([-sin_half, sin_half], axis=1)

    cos = cosf_ref[...]
    sin = sinf_ref[...]
    hd = LANES

    def project(w_ref):
        p = jnp.dot(h_ref[...], w_ref[...], preferred_element_type=F32)
        return [p[:, e * hd:(e + 1) * hd] for e in range(HEADS_PER_STEP)]

    def head_major(a):
        return a.reshape(nb, seg, hd).astype(BF16)

    def rope(a):
        return a * cos + pltpu.roll(a, hd // 2, 1) * sin

    for e, a in enumerate(project(wfq_ref)):
        fq_ref[:, e] = head_major(_rms(a) * qn_ref[...])
    for e, a in enumerate(project(wfk_ref)):
        fk = _rms(a) * kn_ref[...]
        fk_ref[:, e * hd:(e + 1) * hd] = fk
        fkb_ref[:, e] = head_major(fk)
    for e, a in enumerate(project(wfv_ref)):
        fv_ref[:, e * hd:(e + 1) * hd] = a
        fvb_ref[:, e] = head_major(a)
    for e, a in enumerate(project(wrq_ref)):
        rq_ref[:, e] = head_major(rope(a))
    for e, a in enumerate(project(wrk_ref)):
        rk_ref[:, e] = head_major(rope(a) * (hd ** -0.5))
    for e, a in enumerate(project(wrv_ref)):
        rv_ref[:, e] = head_major(a)
    for e, a in enumerate(project(wrg_ref)):
        rg_ref[:, e] = head_major(a)


def _inproj(x, ln1, w_fox, w_ret, wf, bfp, qn, kn, cos_t, sin_t, *, batch, seq, n_heads):
    tokens, d = x.shape
    hd = LANES
    tm = min(INPROJ_TOKEN_TILE, tokens)
    seg = min(seq, tm)
    nb = tm // seg
    tps = seq // seg
    n_groups = n_heads // HEADS_PER_STEP
    gcols = HEADS_PER_STEP * hd
    grid = (tokens // tm, n_groups)

    def w_spec(kind):
        return pl.BlockSpec((d, gcols), lambda i, j: (0, kind * n_groups + j))

    hm_shape = jax.ShapeDtypeStruct((batch, n_heads, seq, hd), BF16)
    hm_spec = pl.BlockSpec((nb, HEADS_PER_STEP, seg, hd), lambda i, j: (i // tps, j, i % tps, 0))
    tok_shape = jax.ShapeDtypeStruct((tokens, n_heads * hd), F32)
    tok_spec = pl.BlockSpec((tm, gcols), lambda i, j: (i, j))

    return pl.pallas_call(
        functools.partial(_inproj_kernel, nb=nb, seg=seg, n_fox=n_heads),
        grid=grid,
        in_specs=[pl.BlockSpec((tm, d), lambda i, j: (i, 0)),
                  _resident((1, d), lambda i, j: (0, 0))]
                 + [w_spec(k) for k in range(N_FOX_KINDS)] + [w_spec(k) for k in range(N_RET_KINDS)]
                 + [_resident((d, hd), lambda i, j: (0, 0)),
                    _resident((1, hd), lambda i, j: (0, 0)),
                    _resident((1, hd), lambda i, j: (0, 0)),
                    _resident((1, hd), lambda i, j: (0, 0)),
                    pl.BlockSpec((tm, hd // 2), lambda i, j: (i % tps, 0)),
                    pl.BlockSpec((tm, hd // 2), lambda i, j: (i % tps, 0))],
        out_specs=[hm_spec] * 7 + [tok_spec, tok_spec,
                                   pl.BlockSpec((tm, n_heads), lambda i, j: (i, 0))],
        out_shape=[hm_shape] * 7 + [tok_shape, tok_shape,
                                    jax.ShapeDtypeStruct((tokens, n_heads), F32)],
        scratch_shapes=[pltpu.VMEM((tm, d), BF16), pltpu.VMEM((tm, hd), F32), pltpu.VMEM((tm, hd), F32)],
        compiler_params=_params(2),
    )(x, ln1, w_fox, w_fox, w_fox, w_ret, w_ret, w_ret, w_ret, wf, bfp, qn, kn, cos_t, sin_t)


def _cumsum_kernel(lf_ref, c_ref, *, ch):
    rows, total = lf_ref.shape
    r = lax.broadcasted_iota(jnp.int32, (ch, ch), 0)
    c = lax.broadcasted_iota(jnp.int32, (ch, ch), 1)
    upper = (r <= c).astype(F32)
    carry = jnp.zeros((rows, 1), F32)
    for k in range(total // ch):
        x = lf_ref[:, k * ch:(k + 1) * ch]
        cs = jnp.dot(x, upper, precision=lax.Precision.HIGHEST, preferred_element_type=F32) + carry
        c_ref[:, k * ch:(k + 1) * ch] = cs
        carry = cs[:, ch - 1:ch]


def _cumsum(lf_t):
    batch, n_heads, total = lf_t.shape
    rows = batch * n_heads
    band = min(CUMSUM_ROWS, rows)
    ch = CUMSUM_CHUNK if total % CUMSUM_CHUNK == 0 else LANES
    spec = pl.BlockSpec((band, total), lambda b: (b, 0))
    out = pl.pallas_call(
        functools.partial(_cumsum_kernel, ch=ch),
        grid=(rows // band,), in_specs=[spec], out_specs=spec,
        out_shape=jax.ShapeDtypeStruct((rows, total), F32),
        compiler_params=_params(1),
    )(lf_t.reshape(rows, total))
    return out.reshape(batch, n_heads, total)


def _softmax_step(z, v, m_ref, l_ref, acc_ref):
    tk = z.shape[1]
    m_prev = m_ref[...]
    m_next = jnp.maximum(m_prev, jnp.max(z, axis=1, keepdims=True))
    p = jnp.exp2(z - pltpu.repeat(m_next, tk // LANES, 1))
    alpha = jnp.exp2(m_prev - m_next)
    l_ref[...] = alpha * l_ref[...] + jnp.sum(p, axis=1, keepdims=True)
    acc_ref[...] = alpha * acc_ref[...] + jnp.dot(p.astype(BF16), v, preferred_element_type=F32)
    m_ref[...] = m_next


def _fox_prompt_kernel(first_ref, q_ref, k_ref, v_ref, c_ref, o_ref, m_ref, l_ref, acc_ref, *, tq, tk, scale):
    b, hg, qi = pl.program_id(0), pl.program_id(1), pl.program_id(2)
    group, hd = q_ref.shape[1], q_ref.shape[3]
    m_ref[...] = jnp.full(m_ref.shape, MASKED_LOGIT, F32)
    l_ref[...] = jnp.zeros(l_ref.shape, F32)
    acc_ref[...] = jnp.zeros(acc_ref.shape, F32)

    def logits(e, k0):
        k = k_ref[0, e, pl.ds(k0, tk), :]
        s = lax.dot_general(q_ref[0, e], k, NT_DIMS, preferred_element_type=F32)
        return s * (scale * LOG2E) - c_ref[0, e, :, pl.ds(k0, tk)] * LOG2E

    def body(j, carry):
        k0 = pl.multiple_of(j * tk, tk)
        for e in range(group):
            _softmax_step(logits(e, k0), v_ref[0, e, pl.ds(k0, tk), :],
                          m_ref.at[e], l_ref.at[e], acc_ref.at[e])
        return carry

    first = first_ref[(b * pl.num_programs(1) + hg) * pl.num_programs(2) + qi]
    lax.fori_loop(first, qi * (tq // tk), body, 0)

    row = lax.broadcasted_iota(jnp.int32, (tq, tk), 0)
    col = lax.broadcasted_iota(jnp.int32, (tq, tk), 1)
    for e in range(group):
        for off in range(0, tq, tk):
            k0 = pl.multiple_of(qi * tq + off, tk)
            z = jnp.where(col + off <= row, logits(e, k0), MASKED_LOGIT)
            _softmax_step(z, v_ref[0, e, pl.ds(k0, tk), :], m_ref.at[e], l_ref.at[e], acc_ref.at[e])
        o_ref[:, e * hd:(e + 1) * hd] = (acc_ref[e] / l_ref[e]).astype(o_ref.dtype)


def _fox_prompt_bounded_kernel(first_ref, q_ref, k_ref, v_ref, c2_ref, r2_ref, o_ref, l_ref, acc_ref, *, tq, tk, scale):
    b, hg, qi = pl.program_id(0), pl.program_id(1), pl.program_id(2)
    group, hd = q_ref.shape[1], q_ref.shape[3]
    l_ref[...] = jnp.zeros(l_ref.shape, F32)
    acc_ref[...] = jnp.zeros(acc_ref.shape, F32)
    q0 = pl.multiple_of(qi * tq, tq)
    row_q = lax.broadcasted_iota(jnp.int32, (tq, tq), 0)
    col_q = lax.broadcasted_iota(jnp.int32, (tq, tq), 1)
    row_terms = [jnp.sum(jnp.where(row_q == col_q, r2_ref[0, e, :, pl.ds(q0, tq)], 0.0), axis=1, keepdims=True)
                 for e in range(group)]
    row = lax.broadcasted_iota(jnp.int32, (tq, tk), 0)
    col = lax.broadcasted_iota(jnp.int32, (tq, tk), 1)

    def step(e, k0, diag_offset=None):
        k = k_ref[0, e, pl.ds(k0, tk), :]
        s = lax.dot_general(q_ref[0, e], k, NT_DIMS, preferred_element_type=F32)
        z = (s * (scale * LOG2E) - c2_ref[0, e, :, pl.ds(k0, tk)]) + row_terms[e]
        if diag_offset is not None:
            z = jnp.where(col + diag_offset <= row, z, MASKED_LOGIT)
        p = jnp.exp2(z)
        part = p[:, 0:LANES]
        for c0 in range(LANES, tk, LANES):
            part = part + p[:, c0:c0 + LANES]
        l_ref[e] += part
        acc_ref[e] += jnp.dot(p.astype(BF16), v_ref[0, e, pl.ds(k0, tk), :], preferred_element_type=F32)

    def body(j, carry):
        k0 = pl.multiple_of(j * tk, tk)
        for e in range(group):
            step(e, k0)
        return carry

    first = first_ref[(b * pl.num_programs(1) + hg) * pl.num_programs(2) + qi]
    lax.fori_loop(first, qi * (tq // tk), body, 0)
    for e in range(group):
        for off in range(0, tq, tk):
            step(e, pl.multiple_of(q0 + off, tk), off)
        l = jnp.sum(l_ref[e], axis=1, keepdims=True)
        o_ref[:, e * hd:(e + 1) * hd] = (acc_ref[e] / l).astype(o_ref.dtype)


def _qk_logit_bound(qn, kn, hd, scale):
    bf16_slack = (1.0 + 2.0 ** -8) ** 2
    return (hd * scale * LOG2E * bf16_slack) * jnp.max(jnp.abs(qn)) * jnp.max(jnp.abs(kn))


def _first_key_tiles(c_row, bound, *, tq, tk, group):
    batch, n_heads, seq = c_row.shape
    c_first_row = c_row[:, :, 0::tq]
    c_last_key = c_row[:, :, tk - 1::tk]
    gap = (c_last_key[:, :, None, :] - c_first_row[:, :, :, None]) * LOG2E
    nq = seq // tq
    earlier = (jnp.arange(seq // tk)[None, :] + 1) * tk <= jnp.arange(nq)[:, None] * tq
    dead = (gap > 2.0 * bound + F32_EXP2_UNDERFLOW) & earlier
    first = jnp.sum(dead, axis=-1).astype(jnp.int32)
    return jnp.min(first.reshape(batch, n_heads // group, group, nq), axis=2).reshape(-1)


def _fox_prompt(q, k, v, c_row, qn, kn):
    batch, n_heads, seq, hd = q.shape
    tq = min(FOX_QUERY_TILE, seq)
    tk = min(FOX_KEY_TILE, tq)
    nq = seq // tq
    group = FOX_HEADS_PER_STEP
    scale = hd ** -0.5
    bound = _qk_logit_bound(qn, kn, hd, scale)
    first = _first_key_tiles(c_row, bound, tq=tq, tk=tk, group=group)
    kv_spec = pl.BlockSpec((1, group, seq, hd), lambda b, h, i, first: (b, h, 0, 0))
    row_spec = pl.BlockSpec((1, group, 1, seq), lambda b, h, i, first: (b, h, 0, 0))
    q_spec = pl.BlockSpec((1, group, tq, hd), lambda b, h, i, first: (b, h, i, 0))
    out_spec = pl.BlockSpec((tq, group * hd), lambda b, h, i, first: (b * nq + i, h))
    out_shape = jax.ShapeDtypeStruct((batch * seq, n_heads * hd), BF16)
    stats = pltpu.VMEM((group, tq, LANES), F32)
    acc = pltpu.VMEM((group, tq, hd), F32)

    def running_max(c4):
        return pl.pallas_call(
            functools.partial(_fox_prompt_kernel, tq=tq, tk=tk, scale=scale),
            grid_spec=pltpu.PrefetchScalarGridSpec(
                num_scalar_prefetch=1, grid=(batch, n_heads // group, nq),
                in_specs=[q_spec, kv_spec, kv_spec, row_spec], out_specs=out_spec,
                scratch_shapes=[stats, stats, acc]),
            out_shape=out_shape, compiler_params=_params(3),
        )(first, q, k, v, c4)

    def bounded(c4):
        return pl.pallas_call(
            functools.partial(_fox_prompt_bounded_kernel, tq=tq, tk=tk, scale=scale),
            grid_spec=pltpu.PrefetchScalarGridSpec(
                num_scalar_prefetch=1, grid=(batch, n_heads // group, nq),
                in_specs=[q_spec, kv_spec, kv_spec, row_spec, row_spec], out_specs=out_spec,
                scratch_shapes=[stats, acc]),
            out_shape=out_shape, compiler_params=_params(3),
        )(first, q, k, v, c4 * LOG2E, c4 * LOG2E - bound)

    return lax.cond(2.0 * bound <= FOX_BOUNDED_LIMIT, bounded, running_max,
                    c_row.reshape(batch, n_heads, 1, seq))


def _fox_sample_kernel(q_ref, kn_ref, vn_ref, kc_ref, vc_ref, c_ref, o_ref, *, past, scale):
    n_heads, new, hd = q_ref.shape[1:]
    row = lax.broadcasted_iota(jnp.int32, (new, new), 0)
    col = lax.broadcasted_iota(jnp.int32, (new, new), 1)
    for h in range(n_heads):
        q = q_ref[0, h]
        kc = kc_ref[0, pl.ds(h, past, stride=n_heads), :].astype(BF16)
        vc = vc_ref[0, pl.ds(h, past, stride=n_heads), :].astype(BF16)
        z_c = (lax.dot_general(q, kc, NT_DIMS, preferred_element_type=F32) * scale
               - c_ref[0, h:h + 1, 0:past])
        z_n = (lax.dot_general(q, kn_ref[0, h], NT_DIMS, preferred_element_type=F32) * scale
               - c_ref[0, h:h + 1, past:past + new])
        z_n = jnp.where(col <= row, z_n, MASKED_LOGIT)
        m = jnp.maximum(jnp.max(z_c, axis=1, keepdims=True), jnp.max(z_n, axis=1, keepdims=True))
        p_c = jnp.exp(z_c - m)
        p_n = jnp.exp(z_n - m)
        l = jnp.sum(p_c, axis=1, keepdims=True) + jnp.sum(p_n, axis=1, keepdims=True)
        acc = (jnp.dot(p_c.astype(BF16), vc, preferred_element_type=F32)
               + jnp.dot(p_n.astype(BF16), vn_ref[0, h], preferred_element_type=F32))
        o_ref[:, h * hd:(h + 1) * hd] = (acc / l).astype(o_ref.dtype)


def _fox_sample(q, k_new, v_new, cache_k, cache_v, c_row, *, layer):
    batch, n_heads, new, hd = q.shape
    past = cache_k.shape[1] // n_heads
    new_spec = pl.BlockSpec((1, n_heads, new, hd), lambda b: (b, 0, 0, 0))
    cache_spec = pl.BlockSpec((1, past * n_heads, hd), lambda b: (layer * batch + b, 0, 0))
    return pl.pallas_call(
        functools.partial(_fox_sample_kernel, past=past, scale=hd ** -0.5),
        grid=(batch,),
        in_specs=[new_spec, new_spec, new_spec, cache_spec, cache_spec,
                  pl.BlockSpec((1, n_heads, c_row.shape[2]), lambda b: (b, 0, 0))],
        out_specs=pl.BlockSpec((new, n_heads * hd), lambda b: (b, 0)),
        out_shape=jax.ShapeDtypeStruct((batch * new, n_heads * hd), BF16),
        compiler_params=_params(1),
    )(q, k_new, v_new, cache_k, cache_v, c_row)


def _retention_kernel(q_ref, k_ref, v_ref, g_ref, s0_ref, gn_ref, lg_ref, o_ref, s_out_ref,
                      s_ref, decay_ref, inter_ref, upd_ref, keep_ref):
    b = pl.program_id(0)
    c = pl.program_id(1)
    n_heads, span, hd = q_ref.shape[1:]
    chunk = decay_ref.shape[1]

    @pl.when((b == 0) & (c == 0))
    def _():
        i = lax.broadcasted_iota(jnp.int32, (chunk, chunk), 0)
        jj = lax.broadcasted_iota(jnp.int32, (chunk, chunk), 1)
        diff = (i - jj).astype(F32)
        pos = lax.broadcasted_iota(jnp.int32, (chunk, hd), 0).astype(F32)
        for h in range(n_heads):
            lg = lg_ref[h:h + 1, :]
            decay_ref[h] = jnp.where(diff >= 0.0, jnp.exp(jnp.maximum(diff, 0.0) * lg[:, 0:1]), 0.0)
            inter_ref[h] = jnp.exp((pos + 1.0) * lg)
            upd_ref[h] = jnp.exp((chunk - 1.0 - pos) * lg)
            keep_ref[h:h + 1, :] = jnp.exp(chunk * lg)

    @pl.when(c == 0)
    def _():
        s_ref[...] = s0_ref[0]

    for r0 in range(0, span, chunk):
        for h in range(n_heads):
            q = q_ref[0, h, r0:r0 + chunk, :]
            k = k_ref[0, h, r0:r0 + chunk, :]
            v = v_ref[0, h, r0:r0 + chunk, :]
            s_prev = s_ref[h]
            scores = lax.dot_general(q, k, NT_DIMS, preferred_element_type=F32) * decay_ref[h]
            o = jnp.dot(scores.astype(BF16), v, preferred_element_type=F32)
            o = o + jnp.dot(q, s_prev.astype(BF16), preferred_element_type=F32) * inter_ref[h]
            kd = (k.astype(F32) * upd_ref[h]).astype(BF16)
            s_ref[h] = keep_ref[h:h + 1, :] * s_prev + lax.dot_general(kd, v, TN_DIMS,
                                                                       preferred_element_type=F32)
            g = g_ref[0, h, r0:r0 + chunk, :].astype(F32)
            y = _rms(o) * gn_ref[h:h + 1, :]
            o_ref[r0:r0 + chunk, h * hd:(h + 1) * hd] = (y * (g * _sigmoid(g))).astype(o_ref.dtype)

    @pl.when(c == pl.num_programs(1) - 1)
    def _():
        s_out_ref[0] = s_ref[...]


def _retention(q, k, v, g, s0, gn, *, s0_block0):
    batch, n_heads, seq, hd = q.shape
    chunk = min(RETENTION_CHUNK, seq)
    span = chunk * RETENTION_CHUNKS_PER_STEP if seq % (chunk * RETENTION_CHUNKS_PER_STEP) == 0 else chunk
    nc = seq // span
    log_gamma = jnp.log(1.0 - jnp.power(2.0, -5.0 - jnp.arange(n_heads, dtype=F32)))
    lg = jnp.broadcast_to(log_gamma[:, None], (n_heads, hd))
    qkv_spec = pl.BlockSpec((1, n_heads, span, hd), lambda b, c: (b, 0, c, 0))
    return pl.pallas_call(
        _retention_kernel,
        grid=(batch, nc),
        in_specs=[qkv_spec] * 4 + [pl.BlockSpec((1, n_heads, hd, hd), lambda b, c: (s0_block0 + b, 0, 0, 0)),
                                   _resident((n_heads, hd), lambda b, c: (0, 0)),
                                   _resident((n_heads, hd), lambda b, c: (0, 0))],
        out_specs=[pl.BlockSpec((span, n_heads * hd), lambda b, c: (b * nc + c, 0)),
                   pl.BlockSpec((1, n_heads, hd, hd), lambda b, c: (b, 0, 0, 0))],
        out_shape=[jax.ShapeDtypeStruct((batch * seq, n_heads * hd), BF16),
                   jax.ShapeDtypeStruct((batch, n_heads, hd, hd), F32)],
        scratch_shapes=[pltpu.VMEM((n_heads, hd, hd), F32), pltpu.VMEM((n_heads, chunk, chunk), F32),
                        pltpu.VMEM((n_heads, chunk, hd), F32), pltpu.VMEM((n_heads, chunk, hd), F32),
                        pltpu.VMEM((n_heads, hd), F32)],
        compiler_params=_params(2),
    )(q, k, v, g, s0, gn, lg)


def _outproj_kernel(of_ref, or_ref, x_ref, w_ref, ln2_ref, xm_ref, h2_ref, wb_ref):
    @pl.when(pl.program_id(0) == 0)
    def _():
        wb_ref[...] = w_ref[...].astype(BF16)

    half = of_ref.shape[1]
    y = jnp.dot(of_ref[...], wb_ref[0:half, :], preferred_element_type=F32)
    y = y + jnp.dot(or_ref[...], wb_ref[half:2 * half, :], preferred_element_type=F32)
    xm = x_ref[...] + y
    xm_ref[...] = xm
    h2_ref[...] = (_rms(xm) * ln2_ref[...]).astype(BF16)


def _outproj(o_fox, o_ret, x, w_out, ln2):
    tokens, d = x.shape
    half = o_fox.shape[1]
    tm = min(OUTPROJ_TOKEN_TILE, tokens)
    row = lambda i: (i, 0)
    return pl.pallas_call(
        _outproj_kernel,
        grid=(tokens // tm,),
        in_specs=[pl.BlockSpec((tm, half), row), pl.BlockSpec((tm, half), row),
                  pl.BlockSpec((tm, d), row),
                  _resident((2 * half, d), lambda i: (0, 0)), _resident((1, d), lambda i: (0, 0))],
        out_specs=[pl.BlockSpec((tm, d), row), pl.BlockSpec((tm, d), row)],
        out_shape=[jax.ShapeDtypeStruct((tokens, d), F32), jax.ShapeDtypeStruct((tokens, d), BF16)],
        scratch_shapes=[pltpu.VMEM((2 * half, d), BF16)],
        compiler_params=_params(1),
    )(o_fox, o_ret, x, w_out, ln2)


ROW_BLOCK = 8
DOWN_CHUNK = 512
SUB_SPLIT = (1, 1)
FFN_TOKEN_TILE = 1024
FFN_WIDTH_TILE = 512


def _convffn_kernel(*refs, n_seg, seg, tps, carried):
    if carried:
        (h2_ref, xm_ref, wa_ref, wg_ref, cwa_ref, cwg_ref, cba_ref, cbg_ref, wd_ref,
         out_ref, ta_ref, tg_ref, z_ref, ca_ref, cg_ref) = refs
        hists = (None, None)
        carries = (ca_ref, cg_ref)
    else:
        (h2_ref, xm_ref, wa_ref, wg_ref, cwa_ref, cwg_ref, cba_ref, cbg_ref, wd_ref, ha_ref, hg_ref,
         out_ref, ta_ref, tg_ref, z_ref) = refs
        hists = (ha_ref, hg_ref)
        carries = (None, None)
    i = pl.program_id(0)
    j = pl.program_id(1)
    tf = wa_ref.shape[1]
    d = out_ref.shape[1]
    tm = n_seg * seg
    col0 = pl.multiple_of(j * tf, tf)

    xw = xm_ref.shape[1]

    @pl.when(j == 0)
    def _():
        out_ref[:, 0:xw] = xm_ref[...]
        out_ref[:, xw:d] = jnp.zeros((tm, d - xw), F32)

    @pl.when((j > 0) & (j < d // xw))
    def _():
        out_ref[:, pl.ds(pl.multiple_of(j * xw, xw), xw)] += xm_ref[...]

    if carried:
        @pl.when(i % tps == 0)
        def _():
            for carry_ref in carries:
                carry_ref[0:2, pl.ds(col0, tf)] = jnp.zeros((2, tf), F32)

    unit = tm // sum(SUB_SPLIT)
    starts = [unit * sum(SUB_SPLIT[:n]) for n in range(len(SUB_SPLIT) + 1)]
    halves = ((wa_ref, cwa_ref, cba_ref, ta_ref, carries[0], hists[0]),
              (wg_ref, cwg_ref, cbg_ref, tg_ref, carries[1], hists[1]))
    ups = [[jnp.dot(h2_ref[lo:hi, :], w_ref[...], preferred_element_type=F32)
            for w_ref, *_ in halves] for lo, hi in zip(starts[:-1], starts[1:])]

    row = lax.broadcasted_iota(jnp.int32, (ROW_BLOCK, tf), 0)
    taps = [[jnp.broadcast_to(cw_ref[t:t + 1, :], (ROW_BLOCK, tf)) for t in range(3)]
            for _, cw_ref, *_ in halves]
    bias = [jnp.broadcast_to(cb_ref[...], (ROW_BLOCK, tf)) for _, _, cb_ref, *_ in halves]
    prev = [None, None]

    def conv_block(x, r0):
        _, _, _, tail_ref, carry_ref, hist_ref = halves[x]
        s, off = divmod(r0, seg)
        if off == 0:
            if carried:
                h0 = carry_ref[0:1, pl.ds(col0, tf)]
                h1 = carry_ref[1:2, pl.ds(col0, tf)]
            else:
                h0 = hist_ref[s, 0:1, :]
                h1 = hist_ref[s, 1:2, :]
            prev[x] = (jnp.broadcast_to(h1, (ROW_BLOCK, tf)), jnp.where(row == 0, h0, h1))
        si = max(n for n, lo in enumerate(starts[:-1]) if lo <= r0)
        cur = ups[si][x][r0 - starts[si]:r0 - starts[si] + ROW_BLOCK]
        rot1 = pltpu.roll(cur, 1, 0)
        rot2 = pltpu.roll(cur, 2, 0)
        um1 = jnp.where(row == 0, prev[x][0], rot1)
        um2 = jnp.where(row <= 1, prev[x][1], rot2)
        prev[x] = (rot1, rot2)
        if off == seg - ROW_BLOCK:
            tail_ref[s] = cur[ROW_BLOCK - 2:ROW_BLOCK, :]
        if carried and r0 == tm - ROW_BLOCK:
            carry_ref[0:2, pl.ds(col0, tf)] = cur[ROW_BLOCK - 2:ROW_BLOCK, :]
        return bias[x] + (um2 * taps[x][0] + um1 * taps[x][1] + cur * taps[x][2])

    pack = 2 * ROW_BLOCK
    for lo, hi in zip(starts[:-1], starts[1:]):
        for r0 in range(lo, hi, pack):
            a = jnp.concatenate([conv_block(0, r0), conv_block(0, r0 + ROW_BLOCK)], axis=0)
            g = jnp.concatenate([conv_block(1, r0), conv_block(1, r0 + ROW_BLOCK)], axis=0)
            z_ref[r0:r0 + pack, :] = ((g * _sigmoid(g)) * a).astype(BF16)
        z = z_ref[lo:hi, :]
        for c0 in range(0, d, DOWN_CHUNK):
            out_ref[lo:hi, c0:c0 + DOWN_CHUNK] += jnp.dot(
                z, wd_ref[:, c0:c0 + DOWN_CHUNK], preferred_element_type=F32)


def _convffn(h2, xm, w_up, conv_w, conv_b, w_down, hist, *, batch, seq):
    tokens, d = xm.shape
    f = w_down.shape[0]
    tf = min(FFN_WIDTH_TILE, f)
    nf = f // tf
    tm = min(FFN_TOKEN_TILE, tokens)
    seg = min(seq, tm)
    n_seg = tm // seg
    tps = seq // seg
    carried = hist is None
    assert not carried or n_seg == 1, "a carried conv state needs one sequence per token tile"
    row = lambda i, j: (i, 0)
    a_col = lambda i, j: (0, j)
    g_col = lambda i, j: (0, nf + j)
    n_slabs = 1
    while 2 * n_slabs <= nf and d % (2 * n_slabs * LANES) == 0:
        n_slabs *= 2
    in_specs = [pl.BlockSpec((tm, d), row),
                pl.BlockSpec((tm, d // n_slabs), lambda i, j: (i, jnp.minimum(j, n_slabs - 1))),
                pl.BlockSpec((d, tf), a_col), pl.BlockSpec((d, tf), g_col),
                pl.BlockSpec((3, tf), a_col), pl.BlockSpec((3, tf), g_col),
                pl.BlockSpec((1, tf), a_col), pl.BlockSpec((1, tf), g_col),
                pl.BlockSpec((tf, d), lambda i, j: (j, 0))]
    args = [h2, xm, w_up, w_up, conv_w, conv_w, conv_b, conv_b, w_down]
    scratch = [pltpu.VMEM((tm, tf), BF16)]
    if carried:
        scratch += [pltpu.VMEM((8, f), F32), pltpu.VMEM((8, f), F32)]
    else:
        in_specs += [pl.BlockSpec((n_seg, 2, tf), lambda i, j: (i, 0, j)),
                     pl.BlockSpec((n_seg, 2, tf), lambda i, j: (i, 0, nf + j))]
        args += [hist, hist]
    tail_spec = pl.BlockSpec((n_seg, 2, tf), lambda i, j: (i, 0, j))
    tail_shape = jax.ShapeDtypeStruct((batch * tps, 2, f), F32)
    out, tails_a, tails_g = pl.pallas_call(
        functools.partial(_convffn_kernel, n_seg=n_seg, seg=seg, tps=tps, carried=carried),
        grid=(tokens // tm, nf),
        in_specs=in_specs,
        out_specs=[pl.BlockSpec((tm, d), row), tail_spec, tail_spec],
        out_shape=[jax.ShapeDtypeStruct((tokens, d), F32), tail_shape, tail_shape],
        scratch_shapes=scratch,
        compiler_params=_params(2),
    )(*args)
    last = lambda t: t.reshape(batch, tps, 2, f)[:, tps - 1]
    return out, jnp.concatenate([last(tails_a), last(tails_g)], axis=-1)


def _rope_tables(pos, rows):
    half = LANES // 2
    inv = jnp.power(ROPE_BASE, -jnp.arange(half, dtype=F32) / half)
    ang = pos.astype(F32)[:, None] * inv[None, :]
    reps = rows // pos.shape[0]
    return jnp.tile(jnp.cos(ang), (reps, 1)), jnp.tile(jnp.sin(ang), (reps, 1))


def kernel(x_prompt, x_sample, cache_fox_k, cache_fox_v, cache_fox_logf, state_ret, state_conv,
           ln1, w_in, b_f, fox_qn, fox_kn, ret_gn, w_out, ln2, w_up, conv_w, conv_b, w_down):
    depth = ln1.shape[0]
    bp, tp, d = x_prompt.shape
    bs, ts, _ = x_sample.shape
    past = cache_fox_k.shape[2]
    n_heads, hd = cache_fox_k.shape[3], cache_fox_k.shape[4]
    d_head = n_heads * hd
    assert hd == LANES and n_heads % HEADS_PER_STEP == 0 and ret_gn.shape[1:] == (n_heads, hd)

    cos_p, sin_p = _rope_tables(jnp.arange(tp), max(tp, min(INPROJ_TOKEN_TILE, bp * tp)))
    cos_s, sin_s = _rope_tables(past + jnp.arange(ts), max(ts, min(INPROJ_TOKEN_TILE, bs * ts)))
    cache_k = cache_fox_k.reshape(depth * bs, past * n_heads, hd)
    cache_v = cache_fox_v.reshape(depth * bs, past * n_heads, hd)
    state_r = state_ret.reshape(depth * bs, n_heads, hd, hd)
    zero_state = jnp.zeros((bp, n_heads, hd, hd), F32)

    xp = x_prompt.reshape(bp * tp, d)
    xs = x_sample.reshape(bs * ts, d)
    st_p, st_s = [], []
    for l in range(depth):
        r0 = N_FOX_KINDS * d_head + n_heads
        w_fox = w_in[l][:, :N_FOX_KINDS * d_head].astype(BF16)
        w_ret = w_in[l][:, r0:].astype(BF16)
        wf = jnp.pad(w_in[l][:, N_FOX_KINDS * d_head:r0], ((0, 0), (0, hd - n_heads))).astype(BF16)
        bfp = jnp.pad(b_f[l], (0, hd - n_heads)).reshape(1, hd)
        qn = fox_qn[l].reshape(1, hd)
        kn = fox_kn[l].reshape(1, hd)
        wu = w_up[l].astype(BF16)
        wd = w_down[l].astype(BF16)
        cw = conv_w[l]
        cb = conv_b[l].reshape(1, -1)
        ln1_l = ln1[l].reshape(1, d)
        ln2_l = ln2[l].reshape(1, d)

        def group(x, cos_t, sin_t, batch, seq):
            return _inproj(x, ln1_l, w_fox, w_ret, wf, bfp, qn, kn, cos_t, sin_t,
                           batch=batch, seq=seq, n_heads=n_heads)

        fq, fkb, fvb, rq, rk, rv, rg, fk, fv, lf = group(xp, cos_p, sin_p, bp, tp)
        c_row = _cumsum(lf.reshape(bp, tp, n_heads).transpose(0, 2, 1))
        o_fox = _fox_prompt(fq, fkb, fvb, c_row, qn, kn)
        o_ret, s_ret_p = _retention(rq, rk, rv, rg, zero_state, ret_gn[l], s0_block0=0)
        xm, h2 = _outproj(o_fox, o_ret, xp, w_out[l], ln2_l)
        xp, conv_p = _convffn(h2, xm, wu, cw, cb, wd, None, batch=bp, seq=tp)
        st_p.append((fk.reshape(bp, tp, n_heads, hd), fv.reshape(bp, tp, n_heads, hd),
                     lf.reshape(bp, tp, n_heads), s_ret_p, conv_p))

        fq, fkb, fvb, rq, rk, rv, rg, fk, fv, lf = group(xs, cos_s, sin_s, bs, ts)
        lf_all = jnp.concatenate([cache_fox_logf[l], lf.reshape(bs, ts, n_heads)], axis=1)
        pad = (-lf_all.shape[1]) % LANES
        lf_all = jnp.pad(lf_all, ((0, 0), (0, pad), (0, 0)))
        c_row = _cumsum(lf_all.transpose(0, 2, 1))
        o_fox = _fox_sample(fq, fkb, fvb, cache_k, cache_v, c_row, layer=l)
        o_ret, s_ret_s = _retention(rq, rk, rv, rg, state_r, ret_gn[l], s0_block0=l * bs)
        xm, h2 = _outproj(o_fox, o_ret, xs, w_out[l], ln2_l)
        xs, conv_s = _convffn(h2, xm, wu, cw, cb, wd, state_conv[l], batch=bs, seq=ts)
        st_s.append((fk.reshape(bs, ts, n_heads, hd), fv.reshape(bs, ts, n_heads, hd),
                     lf.reshape(bs, ts, n_heads), s_ret_s, conv_s))

    stack = lambda st, k: jnp.stack([s[k] for s in st])
    return (xp.reshape(bp, tp, d), xs.reshape(bs, ts, d),
            stack(st_p, 0), stack(st_p, 1), stack(st_p, 2), stack(st_p, 3), stack(st_p, 4),
            stack(st_s, 0), stack(st_s, 1), stack(st_s, 2), stack(st_s, 3), stack(st_s, 4))
```
